```python
import jax, jax.numpy as jnp
from jax import lax
import numpy as np

D_MODEL = 1024
BATCH = 8
SEQ = 2048
DEPTH = 4
DEC_BATCH = 128
DEC_SEQ = 8
PAST_LEN = 16384
PAGE_SIZE = 128

POOL_WIDTH = D_MODEL // 2
POOL_WINDOWS = (2, 4, 8, 16)
N_POOL_GROUPS = len(POOL_WINDOWS)
POOL_GROUP_DIM = POOL_WIDTH // N_POOL_GROUPS
POOL_BUF = max(POOL_WINDOWS) - 1
SG_WIDTH = D_MODEL - POOL_WIDTH
SG_HEADS = 4
SG_HEAD_DIM = SG_WIDTH // SG_HEADS
CHUNK = 128
MIX_WIDTH = POOL_WIDTH + SG_WIDTH
IN_WIDTH = POOL_WIDTH + 2 * SG_WIDTH
D_FF = ((8 * D_MODEL // 3 + 127) // 128) * 128
CONV_WIDTH = 3
CONV_BUF = CONV_WIDTH - 1
EPS = 1e-6

kernel_name = "hybrid_pool_sgmlp_convffn_step"


def rmsnorm(x, g):
    x32 = x.astype(jnp.float32)
    y = x32 * lax.rsqrt(jnp.mean(x32 * x32, axis=-1, keepdims=True) + EPS)
    return (y * g.astype(jnp.float32)).astype(x.dtype)


def pool_mixer(p, past, pos0, w_pool, scale):
    B, T, _ = p.shape
    full = jnp.concatenate([past.astype(p.dtype), p], axis=1)
    csum = jnp.cumsum(full.astype(jnp.float32), axis=1)
    c = jnp.concatenate([jnp.zeros((B, 1, POOL_WIDTH), jnp.float32), csum], axis=1)
    pos = pos0 + jnp.arange(T)
    end = c[:, POOL_BUF + 1:POOL_BUF + 1 + T]
    means = []
    for gi, w in enumerate(POOL_WINDOWS):
        sl = slice(gi * POOL_GROUP_DIM, (gi + 1) * POOL_GROUP_DIM)
        start = c[:, POOL_BUF + 1 - w:POOL_BUF + 1 - w + T, sl]
        cnt = jnp.minimum(pos + 1, w).astype(jnp.float32)[None, :, None]
        means.append((end[..., sl] - start) / cnt)
    d = jnp.concatenate(means, axis=-1) - p.astype(jnp.float32)
    d = d.reshape(B, T, N_POOL_GROUPS, POOL_GROUP_DIM)
    out = jnp.einsum('btgc,gcd->btgd', d, w_pool.astype(jnp.float32)).reshape(B, T, POOL_WIDTH)
    out = out * scale.astype(jnp.float32)
    return out.astype(p.dtype), full[:, -POOL_BUF:]


def spatial_gate(u, v, w_s, b_s):
    B, T, _ = u.shape
    L = min(T, CHUNK)
    nc = T // L
    u = u.reshape(B, nc, L, SG_HEADS, SG_HEAD_DIM)
    v = v.reshape(B, nc, L, SG_HEADS, SG_HEAD_DIM)
    mask = jnp.tril(jnp.ones((L, L), dtype=bool))
    w = jnp.where(mask[None], w_s[:, :L, :L], jnp.zeros((), w_s.dtype))
    mixed = jnp.einsum('hts,bcshd->bcthd', w, v) + b_s[:, :L].T[None, None, :, :, None]
    return (u * mixed).reshape(B, T, SG_WIDTH)


def conv_ffn(h, past, w_gate, w_up, conv_w, conv_b, w_down):
    T = h.shape[1]
    g = h @ w_gate
    full = jnp.concatenate([past.astype(g.dtype), g], axis=1)
    conv = conv_b + sum(conv_w[k] * full[:, k:k + T] for k in range(CONV_WIDTH))
    y = (jax.nn.gelu(conv, approximate=False) * (h @ w_up)) @ w_down
    return y, full[:, -CONV_BUF:]


def layer(x, pool_past, conv_past, pos0, norm1_g, w_in, pool_w, pool_scale, v_norm_g,
          w_spatial, b_spatial, w_out, norm2_g, w_gate, w_up, conv_w, conv_b, w_down):
    h = rmsnorm(x, norm1_g)
    z = h @ w_in
    p = z[..., :POOL_WIDTH]
    uv = jax.nn.gelu(z[..., POOL_WIDTH:], approximate=False)
    u = uv[..., :SG_WIDTH]
    v = rmsnorm(uv[..., SG_WIDTH:], v_norm_g)
    a, pool_state = pool_mixer(p, pool_past, pos0, pool_w, pool_scale)
    s = spatial_gate(u, v, w_spatial, b_spatial)
    x = x + jnp.concatenate([a, s], axis=-1) @ w_out
    y, conv_state = conv_ffn(rmsnorm(x, norm2_g), conv_past, w_gate, w_up, conv_w, conv_b, w_down)
    return x + y, pool_state, conv_state, v


def setup_inputs(seed: int = 0) -> dict:
    key = jax.random.key(seed)
    ks = jax.random.split(key, 20)
    f32 = jnp.float32
    nrm = lambda k, shape, s: (jax.random.normal(k, shape, f32) * s)
    return {
        "x_prompt": nrm(ks[0], (BATCH, SEQ, D_MODEL), 1.0),
        "x_sample": nrm(ks[1], (DEC_BATCH, DEC_SEQ, D_MODEL), 1.0),
        "state_pool": nrm(ks[2], (DEPTH, DEC_BATCH, POOL_BUF, POOL_WIDTH), 1.0),
        "state_conv": nrm(ks[3], (DEPTH, DEC_BATCH, CONV_BUF, D_FF), 1.0),
        "norm1_g": 1.0 + nrm(ks[4], (DEPTH, D_MODEL), 0.05),
        "w_in": nrm(ks[5], (DEPTH, D_MODEL, IN_WIDTH), D_MODEL ** -0.5),
        "pool_w": nrm(ks[6], (DEPTH, N_POOL_GROUPS, POOL_GROUP_DIM, POOL_GROUP_DIM), POOL_GROUP_DIM ** -0.5),
        "pool_scale": 1.0 + nrm(ks[7], (DEPTH, POOL_WIDTH), 0.1),
        "v_norm_g": 1.0 + nrm(ks[8], (DEPTH, SG_WIDTH), 0.05),
        "w_spatial": nrm(ks[9], (DEPTH, SG_HEADS, CHUNK, CHUNK), 0.5 * CHUNK ** -0.5),
        "b_spatial": 1.0 + nrm(ks[10], (DEPTH, SG_HEADS, CHUNK), 0.1),
        "w_out": nrm(ks[11], (DEPTH, MIX_WIDTH, D_MODEL), MIX_WIDTH ** -0.5),
        "norm2_g": 1.0 + nrm(ks[12], (DEPTH, D_MODEL), 0.05),
        "w_gate": nrm(ks[13], (DEPTH, D_MODEL, D_FF), D_MODEL ** -0.5),
        "w_up": nrm(ks[14], (DEPTH, D_MODEL, D_FF), D_MODEL ** -0.5),
        "conv_w": nrm(ks[15], (DEPTH, CONV_WIDTH, D_FF), 0.5),
        "conv_b": nrm(ks[16], (DEPTH, D_FF), 0.01),
        "w_down": nrm(ks[17], (DEPTH, D_FF, D_MODEL), D_FF ** -0.5),
        "final_norm_g": 1.0 + nrm(ks[18], (D_MODEL,), 0.05),
    }


def reference(x_prompt, x_sample, state_pool, state_conv, norm1_g, w_in, pool_w, pool_scale,
              v_norm_g, w_spatial, b_spatial, w_out, norm2_g, w_gate, w_up, conv_w, conv_b,
              w_down, final_norm_g):
    xp, xs = x_prompt, x_sample
    zero_pool = jnp.zeros((BATCH, POOL_BUF, POOL_WIDTH), x_prompt.dtype)
    zero_conv = jnp.zeros((BATCH, CONV_BUF, D_FF), x_prompt.dtype)
    pool_p, pool_s, conv_p, conv_s, v_s = [], [], [], [], []
    for i in range(DEPTH):
        params = (norm1_g[i], w_in[i], pool_w[i], pool_scale[i], v_norm_g[i], w_spatial[i],
                  b_spatial[i], w_out[i], norm2_g[i], w_gate[i], w_up[i], conv_w[i], conv_b[i], w_down[i])
        xp, ps, cs, _ = layer(xp, zero_pool, zero_conv, 0, *params)
        pool_p.append(ps)
        conv_p.append(cs)
        xs, ps, cs, vs = layer(xs, state_pool[i], state_conv[i], PAST_LEN, *params)
        pool_s.append(ps)
        conv_s.append(cs)
        v_s.append(vs)
    y_prompt = rmsnorm(xp, final_norm_g)
    y_sample = rmsnorm(xs, final_norm_g)
    return (y_prompt, y_sample, jnp.stack(pool_p), jnp.stack(pool_s), jnp.stack(conv_p),
            jnp.stack(conv_s), jnp.stack(v_s))
```

```python
import functools

import numpy as np
import jax
import jax.numpy as jnp
from jax import lax
from jax.experimental import pallas as pl
from jax.experimental.pallas import tpu as pltpu

D_MODEL = 1024
POOL_WIDTH = 512
POOL_WINDOWS = (2, 4, 8, 16)
POOL_GROUP_DIM = 128
POOL_PAD = 16
SG_WIDTH = 512
SG_HEADS = 4
SG_HEAD_DIM = 128
CHUNK = 128
IN_WIDTH = POOL_WIDTH + 2 * SG_WIDTH
D_FF = 2816
CONV_PAD = 8
EPS = 1e-6

TM = 256
FF_CHUNK = 256
VMEM_LIMIT_BYTES = 58 * 1024 * 1024

_INV_SQRT2 = 0.7071067811865476


def _gelu(x):
    return 0.5 * x * (1.0 + lax.erf(x * _INV_SQRT2))


def _rms(x, g):
    ms = jnp.mean(x * x, axis=-1, keepdims=True)
    return x * lax.rsqrt(ms + EPS) * g


def _bdot(a, b):
    return jnp.dot(a, b, preferred_element_type=jnp.float32)


def _layer_body(S, T, is_sample, is_last, *refs):
    if is_sample:
        (x_ref, spool_ref, sconv_ref, invcnt_ref, n1_ref, win_ref, pw_ref, pscale_ref, vng_ref, sp_ref,
         bfull_ref, wout_ref, n2_ref, wg_ref, wu_ref, cw_ref, cb_ref, wd_ref, fn_ref,
         y_ref, pool_out_ref, conv_out_ref, v_out_ref,
         pbuf, st8, st4, st2, gbuf, mixbuf, h2buf, actbuf) = refs
    else:
        (x_ref, invcnt_ref, n1_ref, win_ref, pw_ref, pscale_ref, vng_ref, sp_ref,
         bfull_ref, wout_ref, n2_ref, wg_ref, wu_ref, cw_ref, cb_ref, wd_ref, fn_ref,
         y_ref, pool_out_ref, conv_out_ref,
         pbuf, st8, st4, st2, gbuf, mixbuf, h2buf, actbuf, gcarry) = refs
        j = pl.program_id(1)

        @pl.when(j == 0)
        def _():
            pbuf[:, 0:POOL_PAD, :] = jnp.zeros((S, POOL_PAD, POOL_WIDTH), jnp.float32)
            gcarry[...] = jnp.zeros((CONV_PAD, D_FF), jnp.float32)

        @pl.when(j > 0)
        def _():
            pbuf[:, 0:POOL_PAD, :] = pbuf[:, T:T + POOL_PAD, :]

    R = POOL_PAD + T
    x = x_ref[...]
    h = _rms(x, n1_ref[...]).astype(jnp.bfloat16)

    p = _bdot(h, win_ref[:, 0:POOL_WIDTH])
    u = _gelu(_bdot(h, win_ref[:, POOL_WIDTH:POOL_WIDTH + SG_WIDTH]))
    v = _rms(_gelu(_bdot(h, win_ref[:, POOL_WIDTH + SG_WIDTH:IN_WIDTH])), vng_ref[...])
    if is_sample:
        v_out_ref[...] = v
        pbuf[:, 0:POOL_PAD, :] = spool_ref[...]
    pbuf[:, POOL_PAD:R, :] = p.reshape(S, T, POOL_WIDTH)
    pool_out_ref[...] = pbuf[:, T:R, :]

    zeros8 = jnp.zeros((S, 8, POOL_WIDTH), jnp.float32)
    st8[:, 0:8, :] = zeros8
    st4[:, 0:8, :] = zeros8
    st2[:, 0:8, :] = zeros8
    G = POOL_GROUP_DIM
    st8[:, 8:R, 3 * G:4 * G] = pbuf[:, 8:R, 3 * G:4 * G] + pbuf[:, 0:R - 8, 3 * G:4 * G]
    st4[:, 8:R, 2 * G:3 * G] = pbuf[:, 8:R, 2 * G:3 * G] + pbuf[:, 4:R - 4, 2 * G:3 * G]
    st4[:, 8:R, 3 * G:4 * G] = st8[:, 8:R, 3 * G:4 * G] + st8[:, 4:R - 4, 3 * G:4 * G]
    st2[:, 8:R, 1 * G:2 * G] = pbuf[:, 8:R, 1 * G:2 * G] + pbuf[:, 6:R - 2, 1 * G:2 * G]
    st2[:, 8:R, 2 * G:4 * G] = st4[:, 8:R, 2 * G:4 * G] + st4[:, 6:R - 2, 2 * G:4 * G]
    sum0 = pbuf[:, POOL_PAD:R, 0:G] + pbuf[:, POOL_PAD - 1:R - 1, 0:G]
    sum123 = st2[:, POOL_PAD:R, G:4 * G] + st2[:, POOL_PAD - 1:R - 1, G:4 * G]
    wsum = jnp.concatenate([sum0, sum123], axis=-1).reshape(TM, POOL_WIDTH)
    dpool = wsum * invcnt_ref[...] - p
    a = _bdot(dpool.astype(jnp.bfloat16), pw_ref[...]) * pscale_ref[...]
    mixbuf[:, 0:POOL_WIDTH] = a.astype(jnp.bfloat16)

    for hd in range(SG_HEADS):
        sl = slice(hd * SG_HEAD_DIM, (hd + 1) * SG_HEAD_DIM)
        mixed = _bdot(sp_ref[hd], v[:, sl].astype(jnp.bfloat16)) + bfull_ref[:, sl]
        mixbuf[:, POOL_WIDTH + hd * SG_HEAD_DIM:POOL_WIDTH + (hd + 1) * SG_HEAD_DIM] = (
            (u[:, sl] * mixed).astype(jnp.bfloat16))

    x1 = x + _bdot(mixbuf[...], wout_ref[...])
    y_ref[...] = x1
    h2buf[...] = _rms(x1, n2_ref[...]).astype(jnp.bfloat16)

    for c in range(D_FF // FF_CHUNK):
        cs = slice(c * FF_CHUNK, (c + 1) * FF_CHUNK)
        g = _bdot(h2buf[...], wg_ref[:, cs])
        up = _bdot(h2buf[...], wu_ref[:, cs])
        g3 = g.reshape(S, T, FF_CHUNK)
        if is_sample:
            gbuf[:, 0:CONV_PAD, :] = sconv_ref[:, :, cs]
        else:
            gbuf[0, 0:CONV_PAD, :] = gcarry[:, cs]
            gcarry[:, cs] = g3[0, T - CONV_PAD:T, :]
        gbuf[:, CONV_PAD:CONV_PAD + T, :] = g3
        conv_out_ref[:, :, cs] = g3[:, T - CONV_PAD:T, :]
        conv = (cb_ref[:, cs]
                + cw_ref[0:1, cs] * gbuf[:, CONV_PAD - 2:CONV_PAD - 2 + T, :]
                + cw_ref[1:2, cs] * gbuf[:, CONV_PAD - 1:CONV_PAD - 1 + T, :]
                + cw_ref[2:3, cs] * g3)
        act = _gelu(conv).reshape(TM, FF_CHUNK) * up
        actbuf[:, cs] = act.astype(jnp.bfloat16)

    out = y_ref[...] + _bdot(actbuf[...], wd_ref[...])
    if is_last:
        out = _rms(out, fn_ref[...])
    y_ref[...] = out


def _const_spec(shape):
    nd = len(shape)
    return pl.BlockSpec(shape, lambda *_: (0,) * nd)


def _layer_spec(shape, layer):
    nd = len(shape)
    return pl.BlockSpec((None,) + tuple(shape), lambda *_: (layer,) + (0,) * nd)


def _run_layer(layer, is_sample, is_last, x, state, invcnt, wts, sp, bfull):
    (n1, win, pw, pscale, vng, wout, n2, wg, wu, cw, cb, wd, fn) = wts
    if is_sample:
        n_seq = x.shape[0] // 8
        S, T = TM // 8, 8
        grid = (x.shape[0] // TM,)
        row_map = lambda i: (i, 0)
        st_map = lambda i: (i, 0, 0)
        x_spec = pl.BlockSpec((TM, D_MODEL), row_map)
        in_specs = [x_spec,
                    pl.BlockSpec((None, S, POOL_PAD, POOL_WIDTH), lambda i: (layer, i, 0, 0)),
                    pl.BlockSpec((None, S, CONV_PAD, D_FF), lambda i: (layer, i, 0, 0)),
                    _const_spec((TM, POOL_WIDTH))]
        operands = [x, state[0], state[1], invcnt]
        out_shape = [jax.ShapeDtypeStruct((x.shape[0], D_MODEL), jnp.float32),
                     jax.ShapeDtypeStruct((n_seq, POOL_PAD, POOL_WIDTH), jnp.float32),
                     jax.ShapeDtypeStruct((n_seq, CONV_PAD, D_FF), jnp.float32),
                     jax.ShapeDtypeStruct((x.shape[0], SG_WIDTH), jnp.float32)]
        out_specs = [pl.BlockSpec((TM, D_MODEL), row_map),
                     pl.BlockSpec((S, POOL_PAD, POOL_WIDTH), st_map),
                     pl.BlockSpec((S, CONV_PAD, D_FF), st_map),
                     pl.BlockSpec((TM, SG_WIDTH), row_map)]
        semantics = ("arbitrary",)
    else:
        B, L, _ = x.shape
        S, T = 1, TM
        grid = (B, L // TM)
        x_spec = pl.BlockSpec((None, TM, D_MODEL), lambda b, j: (b, j, 0))
        in_specs = [x_spec,
                    pl.BlockSpec((None, TM, POOL_WIDTH), lambda b, j: (jnp.minimum(j, 1), 0, 0))]
        operands = [x, invcnt]
        out_shape = [jax.ShapeDtypeStruct((B, L, D_MODEL), jnp.float32),
                     jax.ShapeDtypeStruct((B, POOL_PAD, POOL_WIDTH), jnp.float32),
                     jax.ShapeDtypeStruct((B, CONV_PAD, D_FF), jnp.float32)]
        out_specs = [pl.BlockSpec((None, TM, D_MODEL), lambda b, j: (b, j, 0)),
                     pl.BlockSpec((1, POOL_PAD, POOL_WIDTH), lambda b, j: (b, 0, 0)),
                     pl.BlockSpec((1, CONV_PAD, D_FF), lambda b, j: (b, 0, 0))]
        semantics = ("arbitrary", "arbitrary")

    in_specs += [
        _layer_spec((1, D_MODEL), layer),
        _layer_spec((D_MODEL, IN_WIDTH), layer),
        _layer_spec((POOL_WIDTH, POOL_WIDTH), layer),
        _layer_spec((1, POOL_WIDTH), layer),
        _layer_spec((1, SG_WIDTH), layer),
        _layer_spec((SG_HEADS, TM, TM), layer),
        _layer_spec((TM, SG_WIDTH), layer),
        _layer_spec((D_MODEL, D_MODEL), layer),
        _layer_spec((1, D_MODEL), layer),
        _layer_spec((D_MODEL, D_FF), layer),
        _layer_spec((D_MODEL, D_FF), layer),
        _layer_spec((3, D_FF), layer),
        _layer_spec((1, D_FF), layer),
        _layer_spec((D_FF, D_MODEL), layer),
        _const_spec((1, D_MODEL)),
    ]
    operands += [n1, win, pw, pscale, vng, sp, bfull, wout, n2, wg, wu, cw, cb, wd, fn]

    R = POOL_PAD + T
    scratch = [pltpu.VMEM((S, R, POOL_WIDTH), jnp.float32)] * 4
    scratch += [pltpu.VMEM((S, CONV_PAD + T, FF_CHUNK), jnp.float32),
                pltpu.VMEM((TM, D_MODEL), jnp.bfloat16),
                pltpu.VMEM((TM, D_MODEL), jnp.bfloat16),
                pltpu.VMEM((TM, D_FF), jnp.bfloat16)]
    if not is_sample:
        scratch.append(pltpu.VMEM((CONV_PAD, D_FF), jnp.float32))

    return pl.pallas_call(
        functools.partial(_layer_body, S, T, is_sample, is_last),
        grid=grid,
        in_specs=in_specs,
        out_specs=out_specs,
        out_shape=out_shape,
        scratch_shapes=scratch,
        compiler_params=pltpu.CompilerParams(dimension_semantics=semantics,
                                             vmem_limit_bytes=VMEM_LIMIT_BYTES),
        name=("sample" if is_sample else "prompt") + f"_layer{layer}",
    )(*operands)


def _inv_counts():
    win = np.repeat(np.asarray(POOL_WINDOWS, np.float32), POOL_GROUP_DIM)[None, :]
    pos = np.arange(TM, dtype=np.float32)[:, None]
    first = 1.0 / np.minimum(pos + 1.0, win)
    later = np.broadcast_to(1.0 / win, (TM, POOL_WIDTH))
    return jnp.asarray(np.stack([first, later]).astype(np.float32))


def _spatial_operands(w_spatial, b_spatial, L):
    depth = w_spatial.shape[0]
    mask = jnp.tril(jnp.ones((L, L), dtype=bool))
    w = jnp.where(mask, w_spatial[:, :, :L, :L], 0.0)
    eye = jnp.eye(TM // L, dtype=w.dtype)
    sp = jnp.einsum('ab,lhts->lhatbs', eye, w).reshape(depth, SG_HEADS, TM, TM).astype(jnp.bfloat16)
    b = jnp.swapaxes(b_spatial[:, :, :L], 1, 2)
    b = jnp.repeat(b, SG_HEAD_DIM, axis=2)
    return sp, jnp.tile(b, (1, TM // L, 1))


def kernel(x_prompt, x_sample, state_pool, state_conv, norm1_g, w_in, pool_w, pool_scale, v_norm_g, w_spatial, b_spatial, w_out, norm2_g, w_gate, w_up, conv_w, conv_b, w_down, final_norm_g):
    depth = w_in.shape[0]
    bf = jnp.bfloat16
    eye = jnp.eye(len(POOL_WINDOWS), dtype=pool_w.dtype)
    pw = jnp.einsum('ab,lacd->lacbd', eye, pool_w).reshape(depth, POOL_WIDTH, POOL_WIDTH).astype(bf)
    wts = (norm1_g[:, None, :], w_in.astype(bf), pw, pool_scale[:, None, :], v_norm_g[:, None, :],
           w_out.astype(bf), norm2_g[:, None, :], w_gate.astype(bf), w_up.astype(bf), conv_w,
           conv_b[:, None, :], w_down.astype(bf), final_norm_g[None, :])
    sp_p, b_p = _spatial_operands(w_spatial, b_spatial, CHUNK)
    sp_s, b_s = _spatial_operands(w_spatial, b_spatial, x_sample.shape[1])
    invcnt = _inv_counts()
    spool = jnp.pad(state_pool, ((0, 0), (0, 0), (POOL_PAD - state_pool.shape[2], 0), (0, 0)))
    sconv = jnp.pad(state_conv, ((0, 0), (0, 0), (CONV_PAD - state_conv.shape[2], 0), (0, 0)))

    n_dec, t_dec, _ = x_sample.shape
    xp, xs = x_prompt, x_sample.reshape(n_dec * t_dec, D_MODEL)
    pool_p, pool_s, conv_p, conv_s, v_s = [], [], [], [], []
    for i in range(depth):
        last = i == depth - 1
        xp, pp, cp = _run_layer(i, False, last, xp, None, invcnt, wts, sp_p, b_p)
        xs, ps, cs, vs = _run_layer(i, True, last, xs, (spool, sconv), invcnt[1], wts, sp_s, b_s)
        pool_p.append(pp)
        conv_p.append(cp)
        pool_s.append(ps)
        conv_s.append(cs)
        v_s.append(vs.reshape(n_dec, t_dec, SG_WIDTH))
    keep_pool = slice(POOL_PAD - state_pool.shape[2], POOL_PAD)
    keep_conv = slice(CONV_PAD - state_conv.shape[2], CONV_PAD)
    return (xp, xs.reshape(n_dec, t_dec, D_MODEL),
            jnp.stack(pool_p)[:, :, keep_pool], jnp.stack(pool_s)[:, :, keep_pool],
            jnp.stack(conv_p)[:, :, keep_conv], jnp.stack(conv_s)[:, :, keep_conv],
            jnp.stack(v_s))
```

```python
import functools

import numpy as np
import jax
import jax.numpy as jnp
from jax import lax
from jax.experimental import pallas as pl
from jax.experimental.pallas import tpu as pltpu

D_MODEL = 1024
POOL_WIDTH = 512
POOL_WINDOWS = (2, 4, 8, 16)
POOL_GROUP_DIM = 128
POOL_BUF = 15
POOL_PAD = 16
SG_WIDTH = 512
SG_HEADS = 4
SG_HEAD_DIM = 128
CHUNK = 128
IN_WIDTH = POOL_WIDTH + 2 * SG_WIDTH
D_FF = 2816
CONV_BUF = 2
CONV_PAD = 8
EPS = 1e-6

TM = 256
FF_CHUNK = 256
VMEM_LIMIT_BYTES = 58 * 1024 * 1024

_INV_SQRT2 = 0.7071067811865476


def _gelu(x):
    return 0.5 * x * (1.0 + lax.erf(x * _INV_SQRT2))


def _rms(x, g):
    ms = jnp.mean(x * x, axis=-1, keepdims=True)
    return x * lax.rsqrt(ms + EPS) * g


def _bdot(a, b):
    return jnp.dot(a, b, preferred_element_type=jnp.float32)


def _layer_body(S, T, is_sample, is_last, n_cast, *refs):
    refs = list(refs)
    x_ref = refs.pop(0)
    if is_sample:
        spool_ref = refs.pop(0)
        sconv_ref = refs.pop(0)
    (invcnt_ref, n1_ref, win_ref, pw_ref, pscale_ref, vng_ref, sp_ref, bfull_ref, wout_ref, n2_ref,
     wg_ref, wu_ref, cw_ref, cb_ref, wd_ref, fn_ref) = refs[:16]
    refs = refs[16:]
    cast_in, refs = refs[:n_cast], refs[n_cast:]
    y_ref, pool_out_ref, conv_out_ref = refs[:3]
    refs = refs[3:]
    if is_sample:
        v_out_ref = refs.pop(0)
    cast_out, refs = refs[:n_cast], refs[n_cast:]
    pbuf, st8, st4, st2, gbuf, mixbuf, h2buf, actbuf = refs[:8]

    for src, dst in zip(cast_in, cast_out):
        dst[...] = src[...].astype(jnp.bfloat16)

    R = POOL_PAD + T
    if is_sample:
        pbuf[:, 0:POOL_PAD - POOL_BUF, :] = jnp.zeros((S, POOL_PAD - POOL_BUF, POOL_WIDTH), jnp.float32)
        pbuf[:, POOL_PAD - POOL_BUF:POOL_PAD, :] = spool_ref[...]
    else:
        gcarry = refs[8]
        j = pl.program_id(1)

        @pl.when(j == 0)
        def _():
            pbuf[:, 0:POOL_PAD, :] = jnp.zeros((S, POOL_PAD, POOL_WIDTH), jnp.float32)
            gcarry[...] = jnp.zeros((CONV_PAD, D_FF), jnp.float32)

        @pl.when(j > 0)
        def _():
            pbuf[:, 0:POOL_PAD, :] = pbuf[:, T:T + POOL_PAD, :]

    x = x_ref[...]
    h = _rms(x, n1_ref[...]).astype(jnp.bfloat16)

    p = _bdot(h, win_ref[:, 0:POOL_WIDTH])
    u = _gelu(_bdot(h, win_ref[:, POOL_WIDTH:POOL_WIDTH + SG_WIDTH]))
    v = _rms(_gelu(_bdot(h, win_ref[:, POOL_WIDTH + SG_WIDTH:IN_WIDTH])), vng_ref[...])
    if is_sample:
        v_out_ref[...] = v
    pbuf[:, POOL_PAD:R, :] = p.reshape(S, T, POOL_WIDTH)
    pool_out_ref[...] = pbuf[:, R - POOL_BUF:R, :]

    zeros8 = jnp.zeros((S, 8, POOL_WIDTH), jnp.float32)
    st8[:, 0:8, :] = zeros8
    st4[:, 0:8, :] = zeros8
    st2[:, 0:8, :] = zeros8
    G = POOL_GROUP_DIM
    st8[:, 8:R, 3 * G:4 * G] = pbuf[:, 8:R, 3 * G:4 * G] + pbuf[:, 0:R - 8, 3 * G:4 * G]
    st4[:, 8:R, 2 * G:3 * G] = pbuf[:, 8:R, 2 * G:3 * G] + pbuf[:, 4:R - 4, 2 * G:3 * G]
    st4[:, 8:R, 3 * G:4 * G] = st8[:, 8:R, 3 * G:4 * G] + st8[:, 4:R - 4, 3 * G:4 * G]
    st2[:, 8:R, 1 * G:2 * G] = pbuf[:, 8:R, 1 * G:2 * G] + pbuf[:, 6:R - 2, 1 * G:2 * G]
    st2[:, 8:R, 2 * G:4 * G] = st4[:, 8:R, 2 * G:4 * G] + st4[:, 6:R - 2, 2 * G:4 * G]
    sum0 = pbuf[:, POOL_PAD:R, 0:G] + pbuf[:, POOL_PAD - 1:R - 1, 0:G]
    sum123 = st2[:, POOL_PAD:R, G:4 * G] + st2[:, POOL_PAD - 1:R - 1, G:4 * G]
    wsum = jnp.concatenate([sum0, sum123], axis=-1).reshape(TM, POOL_WIDTH)
    dpool = wsum * invcnt_ref[...] - p
    a = _bdot(dpool.astype(jnp.bfloat16), pw_ref[...]) * pscale_ref[...]
    mixbuf[:, 0:POOL_WIDTH] = a.astype(jnp.bfloat16)

    for hd in range(SG_HEADS):
        sl = slice(hd * SG_HEAD_DIM, (hd + 1) * SG_HEAD_DIM)
        mixed = _bdot(sp_ref[hd], v[:, sl].astype(jnp.bfloat16)) + bfull_ref[:, sl]
        mixbuf[:, POOL_WIDTH + hd * SG_HEAD_DIM:POOL_WIDTH + (hd + 1) * SG_HEAD_DIM] = (
            (u[:, sl] * mixed).astype(jnp.bfloat16))

    x1 = x + _bdot(mixbuf[...], wout_ref[...])
    y_ref[...] = x1
    h2buf[...] = _rms(x1, n2_ref[...]).astype(jnp.bfloat16)

    for c in range(D_FF // FF_CHUNK):
        cs = slice(c * FF_CHUNK, (c + 1) * FF_CHUNK)
        g = _bdot(h2buf[...], wg_ref[:, cs])
        up = _bdot(h2buf[...], wu_ref[:, cs])
        g3 = g.reshape(S, T, FF_CHUNK)
        if is_sample:
            gbuf[:, CONV_PAD - CONV_BUF:CONV_PAD, :] = sconv_ref[:, :, cs]
        else:
            gbuf[0, 0:CONV_PAD, :] = gcarry[:, cs]
            gcarry[:, cs] = g3[0, T - CONV_PAD:T, :]
        gbuf[:, CONV_PAD:CONV_PAD + T, :] = g3
        conv_out_ref[:, :, cs] = g3[:, T - CONV_BUF:T, :]
        conv = (cb_ref[:, cs]
                + cw_ref[0:1, cs] * gbuf[:, CONV_PAD - 2:CONV_PAD - 2 + T, :]
                + cw_ref[1:2, cs] * gbuf[:, CONV_PAD - 1:CONV_PAD - 1 + T, :]
                + cw_ref[2:3, cs] * g3)
        act = _gelu(conv).reshape(TM, FF_CHUNK) * up
        actbuf[:, cs] = act.astype(jnp.bfloat16)

    out = y_ref[...] + _bdot(actbuf[...], wd_ref[...])
    if is_last:
        out = _rms(out, fn_ref[...])
    y_ref[...] = out


def _const_spec(shape):
    nd = len(shape)
    return pl.BlockSpec(shape, lambda *_: (0,) * nd)


def _layer_spec(shape, layer):
    nd = len(shape)
    return pl.BlockSpec((None,) + tuple(shape), lambda *_: (layer,) + (0,) * nd)


_CAST_ROWS = {D_MODEL: (16, 1), D_FF: (176, 4)}


def _run_layer(layer, is_sample, is_last, x, state, invcnt, small, mats, spatial, next_f32):
    (n1, pscale, vng, n2, cw, cb, pw, fn) = small
    (win, wout, wg, wu, wd) = mats
    sp, bfull = spatial
    if is_sample:
        n_seq = x.shape[0] // 8
        S, T = TM // 8, 8
        grid = (x.shape[0] // TM,)
        row_map = lambda i: (i, 0)
        st_map = lambda i: (i, 0, 0)
        in_specs = [pl.BlockSpec((TM, D_MODEL), row_map),
                    pl.BlockSpec((None, S, POOL_BUF, POOL_WIDTH), lambda i: (layer, i, 0, 0)),
                    pl.BlockSpec((None, S, CONV_BUF, D_FF), lambda i: (layer, i, 0, 0)),
                    _const_spec((TM, POOL_WIDTH))]
        operands = [x, state[0], state[1], invcnt]
        out_shape = [jax.ShapeDtypeStruct((x.shape[0], D_MODEL), jnp.float32),
                     jax.ShapeDtypeStruct((n_seq, POOL_BUF, POOL_WIDTH), jnp.float32),
                     jax.ShapeDtypeStruct((n_seq, CONV_BUF, D_FF), jnp.float32),
                     jax.ShapeDtypeStruct((x.shape[0], SG_WIDTH), jnp.float32)]
        out_specs = [pl.BlockSpec((TM, D_MODEL), row_map),
                     pl.BlockSpec((S, POOL_BUF, POOL_WIDTH), st_map),
                     pl.BlockSpec((S, CONV_BUF, D_FF), st_map),
                     pl.BlockSpec((TM, SG_WIDTH), row_map)]
        semantics = ("arbitrary",)
    else:
        B, L, _ = x.shape
        S, T = 1, TM
        steps = L // TM
        grid = (B, steps)
        in_specs = [pl.BlockSpec((None, TM, D_MODEL), lambda b, j: (b, j, 0)),
                    pl.BlockSpec((None, TM, POOL_WIDTH), lambda b, j: (jnp.minimum(j, 1), 0, 0))]
        operands = [x, invcnt]
        out_shape = [jax.ShapeDtypeStruct((B, L, D_MODEL), jnp.float32),
                     jax.ShapeDtypeStruct((B, POOL_BUF, POOL_WIDTH), jnp.float32),
                     jax.ShapeDtypeStruct((B, CONV_BUF, D_FF), jnp.float32)]
        out_specs = [pl.BlockSpec((None, TM, D_MODEL), lambda b, j: (b, j, 0)),
                     pl.BlockSpec((1, POOL_BUF, POOL_WIDTH), lambda b, j: (b, 0, 0)),
                     pl.BlockSpec((1, CONV_BUF, D_FF), lambda b, j: (b, 0, 0))]
        semantics = ("arbitrary", "arbitrary")

    in_specs += [
        _layer_spec((1, D_MODEL), layer),
        _const_spec((D_MODEL, IN_WIDTH)),
        _layer_spec((POOL_WIDTH, POOL_WIDTH), layer),
        _layer_spec((1, POOL_WIDTH), layer),
        _layer_spec((1, SG_WIDTH), layer),
        _layer_spec((SG_HEADS, TM, TM), layer),
        _layer_spec((TM, SG_WIDTH), layer),
        _const_spec((D_MODEL, D_MODEL)),
        _layer_spec((1, D_MODEL), layer),
        _const_spec((D_MODEL, D_FF)),
        _const_spec((D_MODEL, D_FF)),
        _layer_spec((3, D_FF), layer),
        _layer_spec((1, D_FF), layer),
        _const_spec((D_FF, D_MODEL)),
        _const_spec((1, D_MODEL)),
    ]
    operands += [n1, win, pw, pscale, vng, sp, bfull, wout, n2, wg, wu, cw, cb, wd, fn]

    n_cast = 0
    if next_f32 is not None:
        assert not is_sample and grid[0] * grid[1] * 16 == D_MODEL
        for w in next_f32:
            rows, share = _CAST_ROWS[w.shape[1]]
            cols = w.shape[2]
            in_specs.append(pl.BlockSpec((None, rows, cols),
                                         lambda b, j, share=share: (layer + 1, (b * steps + j) // share, 0)))
            out_specs.append(pl.BlockSpec((rows, cols), lambda b, j, share=share: ((b * steps + j) // share, 0)))
            out_shape.append(jax.ShapeDtypeStruct(w.shape[1:], jnp.bfloat16))
            operands.append(w)
            n_cast += 1

    R = POOL_PAD + T
    scratch = [pltpu.VMEM((S, R, POOL_WIDTH), jnp.float32)] * 4
    scratch += [pltpu.VMEM((S, CONV_PAD + T, FF_CHUNK), jnp.float32),
                pltpu.VMEM((TM, D_MODEL), jnp.bfloat16),
                pltpu.VMEM((TM, D_MODEL), jnp.bfloat16),
                pltpu.VMEM((TM, D_FF), jnp.bfloat16)]
    if not is_sample:
        scratch.append(pltpu.VMEM((CONV_PAD, D_FF), jnp.float32))

    return pl.pallas_call(
        functools.partial(_layer_body, S, T, is_sample, is_last, n_cast),
        grid=grid,
        in_specs=in_specs,
        out_specs=out_specs,
        out_shape=out_shape,
        scratch_shapes=scratch,
        compiler_params=pltpu.CompilerParams(dimension_semantics=semantics,
                                             vmem_limit_bytes=VMEM_LIMIT_BYTES),
        name=("sample" if is_sample else "prompt") + f"_layer{layer}",
    )(*operands)


def _inv_counts():
    win = np.repeat(np.asarray(POOL_WINDOWS, np.float32), POOL_GROUP_DIM)[None, :]
    pos = np.arange(TM, dtype=np.float32)[:, None]
    first = 1.0 / np.minimum(pos + 1.0, win)
    later = np.broadcast_to(1.0 / win, (TM, POOL_WIDTH))
    return jnp.asarray(np.stack([first, later]).astype(np.float32))


def _spatial_operands(w_spatial, b_spatial, L):
    mask = jnp.tril(jnp.ones((L, L), dtype=bool))
    w = jnp.where(mask, w_spatial[:, :, :L, :L], 0.0)
    rows = np.arange(TM)
    expand = jnp.asarray((rows[:, None] % L == np.arange(L)[None, :]).astype(np.float32))
    same_block = jnp.asarray(rows[:, None] // L == rows[None, :] // L)
    sp = jnp.einsum('rt,lhts,cs->lhrc', expand, w, expand, precision=lax.Precision.HIGHEST)
    sp = jnp.where(same_block, sp, 0.0).astype(jnp.bfloat16)
    b = jnp.swapaxes(b_spatial[:, :, :L], 1, 2)
    b = jnp.einsum('rt,lth->lrh', expand, b, precision=lax.Precision.HIGHEST)
    return sp, jnp.repeat(b, SG_HEAD_DIM, axis=2)


def kernel(x_prompt, x_sample, state_pool, state_conv, norm1_g, w_in, pool_w, pool_scale, v_norm_g, w_spatial, b_spatial, w_out, norm2_g, w_gate, w_up, conv_w, conv_b, w_down, final_norm_g):
    depth = w_in.shape[0]
    bf = jnp.bfloat16
    eye = jnp.eye(len(POOL_WINDOWS), dtype=pool_w.dtype)
    pw = jnp.einsum('ab,lacd->lacbd', eye, pool_w).reshape(depth, POOL_WIDTH, POOL_WIDTH).astype(bf)
    small = (norm1_g[:, None, :], pool_scale[:, None, :], v_norm_g[:, None, :], norm2_g[:, None, :], conv_w,
             conv_b[:, None, :], pw, final_norm_g[None, :])
    big_f32 = (w_in, w_out, w_gate, w_up, w_down)
    spatial_p = _spatial_operands(w_spatial, b_spatial, CHUNK)
    spatial_s = _spatial_operands(w_spatial, b_spatial, x_sample.shape[1])
    invcnt = _inv_counts()

    n_dec, t_dec, _ = x_sample.shape
    xp, xs = x_prompt, x_sample.reshape(n_dec * t_dec, D_MODEL)
    mats = tuple(w[0].astype(bf) for w in big_f32)
    pool_p, pool_s, conv_p, conv_s, v_s = [], [], [], [], []
    for i in range(depth):
        last = i == depth - 1
        res = _run_layer(i, False, last, xp, None, invcnt, small, mats, spatial_p, None if last else big_f32)
        xp, pp, cp = res[:3]
        xs, ps, cs, vs = _run_layer(i, True, last, xs, (state_pool, state_conv), invcnt[1], small, mats,
                                    spatial_s, None)
        mats = tuple(res[3:])
        pool_p.append(pp)
        conv_p.append(cp)
        pool_s.append(ps)
        conv_s.append(cs)
        v_s.append(vs.reshape(n_dec, t_dec, SG_WIDTH))
    return (xp, xs.reshape(n_dec, t_dec, D_MODEL), jnp.stack(pool_p), jnp.stack(pool_s), jnp.stack(conv_p),
            jnp.stack(conv_s), jnp.stack(v_s))
```

```python
import functools

import numpy as np
import jax
import jax.numpy as jnp
from jax import lax
from jax.experimental import pallas as pl
from jax.experimental.pallas import tpu as pltpu

D_MODEL = 1024
POOL_WIDTH = 512
POOL_WINDOWS = (2, 4, 8, 16)
POOL_GROUP_DIM = 128
POOL_BUF = 15
POOL_PAD = 16
SG_WIDTH = 512
SG_HEADS = 4
SG_HEAD_DIM = 128
CHUNK = 128
IN_WIDTH = POOL_WIDTH + 2 * SG_WIDTH
D_FF = 2816
CONV_BUF = 2
CONV_PAD = 8
EPS = 1e-6

TM_PROMPT = 512
TM_SAMPLE = 256
SAMPLE_SP_ROWS = 256
FF_CHUNK = 256
VMEM_LIMIT_BYTES = 58 * 1024 * 1024

_INV_SQRT2 = 0.7071067811865476


def _gelu(x):
    return 0.5 * x * (1.0 + lax.erf(x * _INV_SQRT2))


def _rms(x, g):
    ms = jnp.mean(x * x, axis=-1, keepdims=True)
    return x * lax.rsqrt(ms + EPS) * g


def _bdot(a, b):
    return jnp.dot(a, b, preferred_element_type=jnp.float32)


def _layer_body(S, T, is_sample, is_last, n_cast, *refs):
    refs = list(refs)
    x_ref = refs.pop(0)
    if is_sample:
        spool_ref = refs.pop(0)
        sconv_ref = refs.pop(0)
    (invcnt_ref, n1_ref, win_ref, pw_ref, pscale_ref, vng_ref, sp_ref, bfull_ref, wout_ref, n2_ref,
     wg_ref, wu_ref, cw_ref, cb_ref, wd_ref, fn_ref) = refs[:16]
    refs = refs[16:]
    cast_in, refs = refs[:n_cast], refs[n_cast:]
    y_ref, pool_out_ref, conv_out_ref = refs[:3]
    refs = refs[3:]
    if is_sample:
        v_out_ref = refs.pop(0)
    cast_out, refs = refs[:n_cast], refs[n_cast:]
    pbuf, st8, st4, st2, gbuf, mixbuf, h2buf, actbuf = refs[:8]

    for src, dst in zip(cast_in, cast_out):
        dst[...] = src[...].astype(jnp.bfloat16)

    TM = S * T
    R = POOL_PAD + T
    if is_sample:
        pbuf[:, 0:POOL_PAD - POOL_BUF, :] = jnp.zeros((S, POOL_PAD - POOL_BUF, POOL_WIDTH), jnp.float32)
        pbuf[:, POOL_PAD - POOL_BUF:POOL_PAD, :] = spool_ref[...]
    else:
        gcarry = refs[8]
        j = pl.program_id(1)

        @pl.when(j == 0)
        def _():
            pbuf[:, 0:POOL_PAD, :] = jnp.zeros((S, POOL_PAD, POOL_WIDTH), jnp.float32)
            gcarry[...] = jnp.zeros((CONV_PAD, D_FF), jnp.float32)

        @pl.when(j > 0)
        def _():
            pbuf[:, 0:POOL_PAD, :] = pbuf[:, T:T + POOL_PAD, :]

    x = x_ref[...]
    h = _rms(x, n1_ref[...]).astype(jnp.bfloat16)

    p = _bdot(h, win_ref[:, 0:POOL_WIDTH])
    u = _gelu(_bdot(h, win_ref[:, POOL_WIDTH:POOL_WIDTH + SG_WIDTH]))
    v = _rms(_gelu(_bdot(h, win_ref[:, POOL_WIDTH + SG_WIDTH:IN_WIDTH])), vng_ref[...])
    if is_sample:
        v_out_ref[...] = v
    pbuf[:, POOL_PAD:R, :] = p.reshape(S, T, POOL_WIDTH)
    pool_out_ref[...] = pbuf[:, R - POOL_BUF:R, :]

    zeros8 = jnp.zeros((S, 8, POOL_WIDTH), jnp.float32)
    st8[:, 0:8, :] = zeros8
    st4[:, 0:8, :] = zeros8
    st2[:, 0:8, :] = zeros8
    G = POOL_GROUP_DIM
    st8[:, 8:R, 3 * G:4 * G] = pbuf[:, 8:R, 3 * G:4 * G] + pbuf[:, 0:R - 8, 3 * G:4 * G]
    st4[:, 8:R, 2 * G:3 * G] = pbuf[:, 8:R, 2 * G:3 * G] + pbuf[:, 4:R - 4, 2 * G:3 * G]
    st4[:, 8:R, 3 * G:4 * G] = st8[:, 8:R, 3 * G:4 * G] + st8[:, 4:R - 4, 3 * G:4 * G]
    st2[:, 8:R, 1 * G:2 * G] = pbuf[:, 8:R, 1 * G:2 * G] + pbuf[:, 6:R - 2, 1 * G:2 * G]
    st2[:, 8:R, 2 * G:4 * G] = st4[:, 8:R, 2 * G:4 * G] + st4[:, 6:R - 2, 2 * G:4 * G]
    sum0 = pbuf[:, POOL_PAD:R, 0:G] + pbuf[:, POOL_PAD - 1:R - 1, 0:G]
    sum123 = st2[:, POOL_PAD:R, G:4 * G] + st2[:, POOL_PAD - 1:R - 1, G:4 * G]
    wsum = jnp.concatenate([sum0, sum123], axis=-1).reshape(TM, POOL_WIDTH)
    dpool = wsum * invcnt_ref[...] - p
    d16 = dpool.astype(jnp.bfloat16)
    half = POOL_WIDTH // 2
    for k in range(2):
        ks = slice(k * half, (k + 1) * half)
        a = _bdot(d16[:, ks], pw_ref[k]) * pscale_ref[:, ks]
        mixbuf[:, ks] = a.astype(jnp.bfloat16)

    v16 = v.astype(jnp.bfloat16)
    for hd in range(SG_HEADS):
        sl = slice(hd * SG_HEAD_DIM, (hd + 1) * SG_HEAD_DIM)
        osl = slice(POOL_WIDTH + hd * SG_HEAD_DIM, POOL_WIDTH + (hd + 1) * SG_HEAD_DIM)
        if is_sample:
            for r in range(TM // SAMPLE_SP_ROWS):
                rs = slice(r * SAMPLE_SP_ROWS, (r + 1) * SAMPLE_SP_ROWS)
                mixed = _bdot(sp_ref[hd], v16[rs, sl]) + bfull_ref[:, sl]
                mixbuf[rs, osl] = (u[rs, sl] * mixed).astype(jnp.bfloat16)
        else:
            chunks = [slice(c * CHUNK, (c + 1) * CHUNK) for c in range(TM // CHUNK)]
            mixed_all = _bdot(sp_ref[hd], jnp.concatenate([v16[rs, sl] for rs in chunks], axis=1))
            for c, rs in enumerate(chunks):
                mixed = mixed_all[:, c * SG_HEAD_DIM:(c + 1) * SG_HEAD_DIM] + bfull_ref[:, sl]
                mixbuf[rs, osl] = (u[rs, sl] * mixed).astype(jnp.bfloat16)

    x1 = x + _bdot(mixbuf[...], wout_ref[...])
    y_ref[...] = x1
    h2buf[...] = _rms(x1, n2_ref[...]).astype(jnp.bfloat16)

    for c in range(D_FF // FF_CHUNK):
        cs = slice(c * FF_CHUNK, (c + 1) * FF_CHUNK)
        g = _bdot(h2buf[...], wg_ref[:, cs])
        up = _bdot(h2buf[...], wu_ref[:, cs])
        g3 = g.reshape(S, T, FF_CHUNK)
        if is_sample:
            gbuf[:, CONV_PAD - CONV_BUF:CONV_PAD, :] = sconv_ref[:, :, cs]
        else:
            gbuf[0, 0:CONV_PAD, :] = gcarry[:, cs]
            gcarry[:, cs] = g3[0, T - CONV_PAD:T, :]
        gbuf[:, CONV_PAD:CONV_PAD + T, :] = g3
        conv_out_ref[:, :, cs] = g3[:, T - CONV_BUF:T, :]
        conv = (cb_ref[:, cs]
                + cw_ref[0:1, cs] * gbuf[:, CONV_PAD - 2:CONV_PAD - 2 + T, :]
                + cw_ref[1:2, cs] * gbuf[:, CONV_PAD - 1:CONV_PAD - 1 + T, :]
                + cw_ref[2:3, cs] * g3)
        act = _gelu(conv).reshape(TM, FF_CHUNK) * up
        actbuf[:, cs] = act.astype(jnp.bfloat16)

    out = y_ref[...] + _bdot(actbuf[...], wd_ref[...])
    if is_last:
        out = _rms(out, fn_ref[...])
    y_ref[...] = out


def _const_spec(shape):
    nd = len(shape)
    return pl.BlockSpec(shape, lambda *_: (0,) * nd)


def _layer_spec(shape, layer):
    nd = len(shape)
    return pl.BlockSpec((None,) + tuple(shape), lambda *_: (layer,) + (0,) * nd)


DOWN_CAST_ROWS = 176


def _run_layer(layer, is_sample, is_last, x, state, invcnt, small, mats, spatial, next_f32):
    (n1, pscale, vng, n2, cw, cb, pw, fn) = small
    (win, wout, wg, wu, wd) = mats
    sp, bfull = spatial
    TM = TM_SAMPLE if is_sample else TM_PROMPT
    if is_sample:
        n_seq = x.shape[0] // 8
        S, T = TM // 8, 8
        grid = (x.shape[0] // TM,)
        row_map = lambda i: (i, 0)
        st_map = lambda i: (i, 0, 0)
        in_specs = [pl.BlockSpec((TM, D_MODEL), row_map),
                    pl.BlockSpec((None, S, POOL_BUF, POOL_WIDTH), lambda i: (layer, i, 0, 0)),
                    pl.BlockSpec((None, S, CONV_BUF, D_FF), lambda i: (layer, i, 0, 0)),
                    _const_spec((TM, POOL_WIDTH))]
        operands = [x, state[0], state[1], invcnt]
        out_shape = [jax.ShapeDtypeStruct((x.shape[0], D_MODEL), jnp.float32),
                     jax.ShapeDtypeStruct((n_seq, POOL_BUF, POOL_WIDTH), jnp.float32),
                     jax.ShapeDtypeStruct((n_seq, CONV_BUF, D_FF), jnp.float32),
                     jax.ShapeDtypeStruct((x.shape[0], SG_WIDTH), jnp.float32)]
        out_specs = [pl.BlockSpec((TM, D_MODEL), row_map),
                     pl.BlockSpec((S, POOL_BUF, POOL_WIDTH), st_map),
                     pl.BlockSpec((S, CONV_BUF, D_FF), st_map),
                     pl.BlockSpec((TM, SG_WIDTH), row_map)]
        semantics = ("arbitrary",)
    else:
        B, L, _ = x.shape
        S, T = 1, TM
        steps = L // TM
        grid = (B, steps)
        in_specs = [pl.BlockSpec((None, TM, D_MODEL), lambda b, j: (b, j, 0)),
                    pl.BlockSpec((None, TM, POOL_WIDTH), lambda b, j: (jnp.minimum(j, 1), 0, 0))]
        operands = [x, invcnt]
        out_shape = [jax.ShapeDtypeStruct((B, L, D_MODEL), jnp.float32),
                     jax.ShapeDtypeStruct((B, POOL_BUF, POOL_WIDTH), jnp.float32),
                     jax.ShapeDtypeStruct((B, CONV_BUF, D_FF), jnp.float32)]
        out_specs = [pl.BlockSpec((None, TM, D_MODEL), lambda b, j: (b, j, 0)),
                     pl.BlockSpec((1, POOL_BUF, POOL_WIDTH), lambda b, j: (b, 0, 0)),
                     pl.BlockSpec((1, CONV_BUF, D_FF), lambda b, j: (b, 0, 0))]
        semantics = ("arbitrary", "arbitrary")

    in_specs += [
        _layer_spec((1, D_MODEL), layer),
        _const_spec((D_MODEL, IN_WIDTH)),
        _layer_spec((2, POOL_WIDTH // 2, POOL_WIDTH // 2), layer),
        _layer_spec((1, POOL_WIDTH), layer),
        _layer_spec((1, SG_WIDTH), layer),
        _layer_spec(sp.shape[1:], layer),
        _layer_spec(bfull.shape[1:], layer),
        _const_spec((D_MODEL, D_MODEL)),
        _layer_spec((1, D_MODEL), layer),
        _const_spec((D_MODEL, D_FF)),
        _const_spec((D_MODEL, D_FF)),
        _layer_spec((3, D_FF), layer),
        _layer_spec((1, D_FF), layer),
        _const_spec((D_FF, D_MODEL)),
        _const_spec((1, D_MODEL)),
    ]
    operands += [n1, win, pw, pscale, vng, sp, bfull, wout, n2, wg, wu, cw, cb, wd, fn]

    n_cast = 0
    if next_f32 is not None:
        assert not is_sample
        n_steps = grid[0] * grid[1]
        for w in next_f32:
            rows, share = ((D_MODEL // n_steps, 1) if w.shape[1] == D_MODEL
                           else (DOWN_CAST_ROWS, n_steps * DOWN_CAST_ROWS // D_FF))
            cols = w.shape[2]
            in_specs.append(pl.BlockSpec((None, rows, cols),
                                         lambda b, j, share=share: (layer + 1, (b * steps + j) // share, 0)))
            out_specs.append(pl.BlockSpec((rows, cols), lambda b, j, share=share: ((b * steps + j) // share, 0)))
            out_shape.append(jax.ShapeDtypeStruct(w.shape[1:], jnp.bfloat16))
            operands.append(w)
            n_cast += 1

    R = POOL_PAD + T
    scratch = [pltpu.VMEM((S, R, POOL_WIDTH), jnp.float32)] * 4
    scratch += [pltpu.VMEM((S, CONV_PAD + T, FF_CHUNK), jnp.float32),
                pltpu.VMEM((TM, D_MODEL), jnp.bfloat16),
                pltpu.VMEM((TM, D_MODEL), jnp.bfloat16),
                pltpu.VMEM((TM, D_FF), jnp.bfloat16)]
    if not is_sample:
        scratch.append(pltpu.VMEM((CONV_PAD, D_FF), jnp.float32))

    return pl.pallas_call(
        functools.partial(_layer_body, S, T, is_sample, is_last, n_cast),
        grid=grid,
        in_specs=in_specs,
        out_specs=out_specs,
        out_shape=out_shape,
        scratch_shapes=scratch,
        compiler_params=pltpu.CompilerParams(dimension_semantics=semantics,
                                             vmem_limit_bytes=VMEM_LIMIT_BYTES),
        name=("sample" if is_sample else "prompt") + f"_layer{layer}",
    )(*operands)


def _inv_counts(n_rows):
    win = np.repeat(np.asarray(POOL_WINDOWS, np.float32), POOL_GROUP_DIM)[None, :]
    pos = np.arange(n_rows, dtype=np.float32)[:, None]
    first = 1.0 / np.minimum(pos + 1.0, win)
    later = np.broadcast_to(1.0 / win, (n_rows, POOL_WIDTH))
    return jnp.asarray(np.stack([first, later]).astype(np.float32))


def _spatial_operands(w_spatial, b_spatial, L, n_rows):
    mask = jnp.tril(jnp.ones((L, L), dtype=bool))
    w = jnp.where(mask, w_spatial[:, :, :L, :L], 0.0)
    rows = np.arange(n_rows)
    expand = jnp.asarray((rows[:, None] % L == np.arange(L)[None, :]).astype(np.float32))
    same_block = jnp.asarray(rows[:, None] // L == rows[None, :] // L)
    sp = jnp.einsum('rt,lhts,cs->lhrc', expand, w, expand, precision=lax.Precision.HIGHEST)
    sp = jnp.where(same_block, sp, 0.0).astype(jnp.bfloat16)
    b = jnp.swapaxes(b_spatial[:, :, :L], 1, 2)
    b = jnp.einsum('rt,lth->lrh', expand, b, precision=lax.Precision.HIGHEST)
    return sp, jnp.repeat(b, SG_HEAD_DIM, axis=2)


def kernel(x_prompt, x_sample, state_pool, state_conv, norm1_g, w_in, pool_w, pool_scale, v_norm_g, w_spatial, b_spatial, w_out, norm2_g, w_gate, w_up, conv_w, conv_b, w_down, final_norm_g):
    depth = w_in.shape[0]
    bf = jnp.bfloat16
    eye = jnp.eye(2, dtype=pool_w.dtype)
    pw = jnp.einsum('ab,lkacd->lkacbd', eye, pool_w.reshape(depth, 2, 2, POOL_GROUP_DIM, POOL_GROUP_DIM))
    pw = pw.reshape(depth, 2, POOL_WIDTH // 2, POOL_WIDTH // 2).astype(bf)
    small = (norm1_g[:, None, :], pool_scale[:, None, :], v_norm_g[:, None, :], norm2_g[:, None, :], conv_w,
             conv_b[:, None, :], pw, final_norm_g[None, :])
    big_f32 = (w_in, w_out, w_gate, w_up, w_down)
    spatial_p = _spatial_operands(w_spatial, b_spatial, CHUNK, CHUNK)
    spatial_s = _spatial_operands(w_spatial, b_spatial, x_sample.shape[1], SAMPLE_SP_ROWS)
    invcnt = _inv_counts(TM_PROMPT)

    n_dec, t_dec, _ = x_sample.shape
    xp, xs = x_prompt, x_sample.reshape(n_dec * t_dec, D_MODEL)
    mats = tuple(w[0].astype(bf) for w in big_f32)
    pool_p, pool_s, conv_p, conv_s, v_s = [], [], [], [], []
    for i in range(depth):
        last = i == depth - 1
        res = _run_layer(i, False, last, xp, None, invcnt, small, mats, spatial_p, None if last else big_f32)
        xp, pp, cp = res[:3]
        xs, ps, cs, vs = _run_layer(i, True, last, xs, (state_pool, state_conv), invcnt[1, :TM_SAMPLE], small,
                                    mats, spatial_s, None)
        mats = tuple(res[3:])
        pool_p.append(pp)
        conv_p.append(cp)
        pool_s.append(ps)
        conv_s.append(cs)
        v_s.append(vs.reshape(n_dec, t_dec, SG_WIDTH))
    return (xp, xs.reshape(n_dec, t_dec, D_MODEL), jnp.stack(pool_p), jnp.stack(pool_s), jnp.stack(conv_p),
            jnp.stack(conv_s), jnp.stack(v_s))
```

```python
import functools

import numpy as np
import jax
import jax.numpy as jnp
from jax import lax
from jax.experimental import pallas as pl
from jax.experimental.pallas import tpu as pltpu

D_MODEL = 1024
POOL_WIDTH = 512
POOL_WINDOWS = (2, 4, 8, 16)
POOL_GROUP_DIM = 128
POOL_BUF = 15
POOL_PAD = 16
SG_WIDTH = 512
SG_HEADS = 4
SG_HEAD_DIM = 128
CHUNK = 128
IN_WIDTH = POOL_WIDTH + 2 * SG_WIDTH
D_FF = 2816
CONV_BUF = 2
CONV_PAD = 8
EPS = 1e-6

TM_PROMPT = 512
TM_SAMPLE = 256
SAMPLE_SP_ROWS = 256
FF_CHUNK = 256
TAIL_PARTS = 4
TAIL_COLS = D_MODEL // TAIL_PARTS
VMEM_LIMIT_BYTES = 58 * 1024 * 1024

_INV_SQRT2 = 0.7071067811865476


def _gelu(x):
    return 0.5 * x * (1.0 + lax.erf(x * _INV_SQRT2))


def _rms(x, g):
    ms = jnp.mean(x * x, axis=-1, keepdims=True)
    return x * lax.rsqrt(ms + EPS) * g


def _bdot(a, b):
    return jnp.dot(a, b, preferred_element_type=jnp.float32)


def _layer_body(S, T, is_sample, is_last, n_cast, seq_steps, n_tiles, *refs):
    refs = list(refs)
    x_ref = refs.pop(0)
    if is_sample:
        spool_ref = refs.pop(0)
        sconv_ref = refs.pop(0)
    (invcnt_ref, n1_ref, win_ref, pw_ref, pscale_ref, vng_ref, sp_ref, bfull_ref, wout_ref, n2_ref,
     wg_ref, wu_ref, cw_ref, cb_ref, wd_ref, fn_ref) = refs[:16]
    refs = refs[16:]
    cast_in, refs = refs[:n_cast], refs[n_cast:]
    y_ref, pool_out_ref, conv_out_ref = refs[:3]
    refs = refs[3:]
    if is_sample:
        v_out_ref = refs.pop(0)
    cast_out, refs = refs[:n_cast], refs[n_cast:]
    pbuf, st8, st4, st2, gbuf, mixbuf, h2buf, actbuf, x1buf = refs[:9]

    for src, dst in zip(cast_in, cast_out):
        dst[...] = src[...].astype(jnp.bfloat16)

    def tail_begin():
        y_ref[...] = x1buf[...]

    def tail_part(k):
        cols = slice(k * TAIL_COLS, (k + 1) * TAIL_COLS)
        y_ref[:, cols] += _bdot(actbuf[...], wd_ref[:, cols])

    def tail_end():
        if is_last:
            y_ref[...] = _rms(y_ref[...], fn_ref[...])

    def tail_all():
        tail_begin()
        for k in range(TAIL_PARTS):
            tail_part(k)
        tail_end()

    front = functools.partial(
        _tile_front, S, T, is_sample, x_ref, invcnt_ref, n1_ref, win_ref, pw_ref, pscale_ref, vng_ref, sp_ref,
        bfull_ref, wout_ref, n2_ref, wg_ref, wu_ref, cw_ref, cb_ref, pool_out_ref, conv_out_ref,
        pbuf, st8, st4, st2, gbuf, mixbuf, h2buf, actbuf, x1buf)

    if is_sample:
        pbuf[:, 0:POOL_PAD - POOL_BUF, :] = jnp.zeros((S, POOL_PAD - POOL_BUF, POOL_WIDTH), jnp.float32)
        pbuf[:, POOL_PAD - POOL_BUF:POOL_PAD, :] = spool_ref[...]
        front(sconv_ref, v_out_ref, None, lambda k: None)
        tail_all()
        return

    gcarry = refs[9]
    s = pl.program_id(0)
    j = lax.rem(s, seq_steps)

    @pl.when(s == 0)
    def _():
        x1buf[...] = jnp.zeros(x1buf.shape, x1buf.dtype)
        actbuf[...] = jnp.zeros(actbuf.shape, actbuf.dtype)

    @pl.when(j == 0)
    def _():
        pbuf[:, 0:POOL_PAD, :] = jnp.zeros((S, POOL_PAD, POOL_WIDTH), jnp.float32)
        gcarry[...] = jnp.zeros((CONV_PAD, D_FF), jnp.float32)

    @pl.when(j > 0)
    def _():
        pbuf[:, 0:POOL_PAD, :] = pbuf[:, T:T + POOL_PAD, :]

    @pl.when(s < n_tiles)
    def _():
        tail_begin()
        front(None, None, gcarry, tail_part)
        tail_end()

    @pl.when(s == n_tiles)
    def _():
        tail_all()


def _tile_front(S, T, is_sample, x_ref, invcnt_ref, n1_ref, win_ref, pw_ref, pscale_ref, vng_ref, sp_ref,
                bfull_ref, wout_ref, n2_ref, wg_ref, wu_ref, cw_ref, cb_ref, pool_out_ref, conv_out_ref,
                pbuf, st8, st4, st2, gbuf, mixbuf, h2buf, actbuf, x1buf, sconv_ref, v_out_ref, gcarry, fill):
    TM = S * T
    R = POOL_PAD + T
    x = x_ref[...]
    fill(0)
    h = _rms(x, n1_ref[...]).astype(jnp.bfloat16)

    p = _bdot(h, win_ref[:, 0:POOL_WIDTH])
    u = _gelu(_bdot(h, win_ref[:, POOL_WIDTH:POOL_WIDTH + SG_WIDTH]))
    v = _rms(_gelu(_bdot(h, win_ref[:, POOL_WIDTH + SG_WIDTH:IN_WIDTH])), vng_ref[...])
    fill(1)
    if is_sample:
        v_out_ref[...] = v
    pbuf[:, POOL_PAD:R, :] = p.reshape(S, T, POOL_WIDTH)
    pool_out_ref[...] = pbuf[:, R - POOL_BUF:R, :]

    zeros8 = jnp.zeros((S, 8, POOL_WIDTH), jnp.float32)
    st8[:, 0:8, :] = zeros8
    st4[:, 0:8, :] = zeros8
    st2[:, 0:8, :] = zeros8
    G = POOL_GROUP_DIM
    st8[:, 8:R, 3 * G:4 * G] = pbuf[:, 8:R, 3 * G:4 * G] + pbuf[:, 0:R - 8, 3 * G:4 * G]
    st4[:, 8:R, 2 * G:3 * G] = pbuf[:, 8:R, 2 * G:3 * G] + pbuf[:, 4:R - 4, 2 * G:3 * G]
    st4[:, 8:R, 3 * G:4 * G] = st8[:, 8:R, 3 * G:4 * G] + st8[:, 4:R - 4, 3 * G:4 * G]
    st2[:, 8:R, 1 * G:2 * G] = pbuf[:, 8:R, 1 * G:2 * G] + pbuf[:, 6:R - 2, 1 * G:2 * G]
    st2[:, 8:R, 2 * G:4 * G] = st4[:, 8:R, 2 * G:4 * G] + st4[:, 6:R - 2, 2 * G:4 * G]
    sum0 = pbuf[:, POOL_PAD:R, 0:G] + pbuf[:, POOL_PAD - 1:R - 1, 0:G]
    sum123 = st2[:, POOL_PAD:R, G:4 * G] + st2[:, POOL_PAD - 1:R - 1, G:4 * G]
    wsum = jnp.concatenate([sum0, sum123], axis=-1).reshape(TM, POOL_WIDTH)
    dpool = wsum * invcnt_ref[...] - p
    d16 = dpool.astype(jnp.bfloat16)
    half = POOL_WIDTH // 2
    for k in range(2):
        ks = slice(k * half, (k + 1) * half)
        a = _bdot(d16[:, ks], pw_ref[k]) * pscale_ref[:, ks]
        mixbuf[:, ks] = a.astype(jnp.bfloat16)

    fill(2)

    v16 = v.astype(jnp.bfloat16)
    for hd in range(SG_HEADS):
        sl = slice(hd * SG_HEAD_DIM, (hd + 1) * SG_HEAD_DIM)
        osl = slice(POOL_WIDTH + hd * SG_HEAD_DIM, POOL_WIDTH + (hd + 1) * SG_HEAD_DIM)
        if is_sample:
            for r in range(TM // SAMPLE_SP_ROWS):
                rs = slice(r * SAMPLE_SP_ROWS, (r + 1) * SAMPLE_SP_ROWS)
                mixed = _bdot(sp_ref[hd], v16[rs, sl]) + bfull_ref[:, sl]
                mixbuf[rs, osl] = (u[rs, sl] * mixed).astype(jnp.bfloat16)
        else:
            chunks = [slice(c * CHUNK, (c + 1) * CHUNK) for c in range(TM // CHUNK)]
            mixed_all = _bdot(sp_ref[hd], jnp.concatenate([v16[rs, sl] for rs in chunks], axis=1))
            for c, rs in enumerate(chunks):
                mixed = mixed_all[:, c * SG_HEAD_DIM:(c + 1) * SG_HEAD_DIM] + bfull_ref[:, sl]
                mixbuf[rs, osl] = (u[rs, sl] * mixed).astype(jnp.bfloat16)

    x1 = x + _bdot(mixbuf[...], wout_ref[...])
    x1buf[...] = x1
    fill(3)
    h2buf[...] = _rms(x1, n2_ref[...]).astype(jnp.bfloat16)

    for c in range(D_FF // FF_CHUNK):
        cs = slice(c * FF_CHUNK, (c + 1) * FF_CHUNK)
        g = _bdot(h2buf[...], wg_ref[:, cs])
        up = _bdot(h2buf[...], wu_ref[:, cs])
        g3 = g.reshape(S, T, FF_CHUNK)
        if is_sample:
            gbuf[:, CONV_PAD - CONV_BUF:CONV_PAD, :] = sconv_ref[:, :, cs]
        else:
            gbuf[0, 0:CONV_PAD, :] = gcarry[:, cs]
            gcarry[:, cs] = g3[0, T - CONV_PAD:T, :]
        gbuf[:, CONV_PAD:CONV_PAD + T, :] = g3
        conv_out_ref[:, :, cs] = g3[:, T - CONV_BUF:T, :]
        conv = (cb_ref[:, cs]
                + cw_ref[0:1, cs] * gbuf[:, CONV_PAD - 2:CONV_PAD - 2 + T, :]
                + cw_ref[1:2, cs] * gbuf[:, CONV_PAD - 1:CONV_PAD - 1 + T, :]
                + cw_ref[2:3, cs] * g3)
        act = _gelu(conv).reshape(TM, FF_CHUNK) * up
        actbuf[:, cs] = act.astype(jnp.bfloat16)


def _const_spec(shape):
    nd = len(shape)
    return pl.BlockSpec(shape, lambda *_: (0,) * nd)


def _layer_spec(shape, layer):
    nd = len(shape)
    return pl.BlockSpec((None,) + tuple(shape), lambda *_: (layer,) + (0,) * nd)


DOWN_CAST_ROWS = 176


def _run_layer(layer, is_sample, is_last, x, state, invcnt, small, mats, spatial, next_f32):
    (n1, pscale, vng, n2, cw, cb, pw, fn) = small
    (win, wout, wg, wu, wd) = mats
    sp, bfull = spatial
    TM = TM_SAMPLE if is_sample else TM_PROMPT
    if is_sample:
        n_seq = x.shape[0] // 8
        S, T = TM // 8, 8
        grid = (x.shape[0] // TM,)
        row_map = lambda i: (i, 0)
        st_map = lambda i: (i, 0, 0)
        in_specs = [pl.BlockSpec((TM, D_MODEL), row_map),
                    pl.BlockSpec((None, S, POOL_BUF, POOL_WIDTH), lambda i: (layer, i, 0, 0)),
                    pl.BlockSpec((None, S, CONV_BUF, D_FF), lambda i: (layer, i, 0, 0)),
                    _const_spec((TM, POOL_WIDTH))]
        operands = [x, state[0], state[1], invcnt]
        out_shape = [jax.ShapeDtypeStruct((x.shape[0], D_MODEL), jnp.float32),
                     jax.ShapeDtypeStruct((n_seq, POOL_BUF, POOL_WIDTH), jnp.float32),
                     jax.ShapeDtypeStruct((n_seq, CONV_BUF, D_FF), jnp.float32),
                     jax.ShapeDtypeStruct((x.shape[0], SG_WIDTH), jnp.float32)]
        out_specs = [pl.BlockSpec((TM, D_MODEL), row_map),
                     pl.BlockSpec((S, POOL_BUF, POOL_WIDTH), st_map),
                     pl.BlockSpec((S, CONV_BUF, D_FF), st_map),
                     pl.BlockSpec((TM, SG_WIDTH), row_map)]
        semantics = ("arbitrary",)
    else:
        B, L, _ = x.shape
        S, T = 1, TM
        steps = L // TM
        n_tiles = B * steps
        grid = (n_tiles + 1,)
        cur = lambda s: jnp.minimum(s, n_tiles - 1)
        prev = lambda s: jnp.maximum(s - 1, 0)
        in_specs = [pl.BlockSpec((None, TM, D_MODEL), lambda s: (cur(s) // steps, cur(s) % steps, 0)),
                    pl.BlockSpec((None, TM, POOL_WIDTH), lambda s: (jnp.minimum(cur(s) % steps, 1), 0, 0))]
        operands = [x, invcnt]
        out_shape = [jax.ShapeDtypeStruct((B, L, D_MODEL), jnp.float32),
                     jax.ShapeDtypeStruct((B, POOL_BUF, POOL_WIDTH), jnp.float32),
                     jax.ShapeDtypeStruct((B, CONV_BUF, D_FF), jnp.float32)]
        out_specs = [pl.BlockSpec((None, TM, D_MODEL), lambda s: (prev(s) // steps, prev(s) % steps, 0)),
                     pl.BlockSpec((1, POOL_BUF, POOL_WIDTH), lambda s: (cur(s) // steps, 0, 0)),
                     pl.BlockSpec((1, CONV_BUF, D_FF), lambda s: (cur(s) // steps, 0, 0))]
        semantics = ("arbitrary",)

    in_specs += [
        _layer_spec((1, D_MODEL), layer),
        _const_spec((D_MODEL, IN_WIDTH)),
        _layer_spec((2, POOL_WIDTH // 2, POOL_WIDTH // 2), layer),
        _layer_spec((1, POOL_WIDTH), layer),
        _layer_spec((1, SG_WIDTH), layer),
        _layer_spec(sp.shape[1:], layer),
        _layer_spec(bfull.shape[1:], layer),
        _const_spec((D_MODEL, D_MODEL)),
        _layer_spec((1, D_MODEL), layer),
        _const_spec((D_MODEL, D_FF)),
        _const_spec((D_MODEL, D_FF)),
        _layer_spec((3, D_FF), layer),
        _layer_spec((1, D_FF), layer),
        _const_spec((D_FF, D_MODEL)),
        _const_spec((1, D_MODEL)),
    ]
    operands += [n1, win, pw, pscale, vng, sp, bfull, wout, n2, wg, wu, cw, cb, wd, fn]

    n_cast = 0
    if next_f32 is not None:
        assert not is_sample
        for w in next_f32:
            rows, share = ((D_MODEL // n_tiles, 1) if w.shape[1] == D_MODEL
                           else (DOWN_CAST_ROWS, n_tiles * DOWN_CAST_ROWS // D_FF))
            cols = w.shape[2]
            in_specs.append(pl.BlockSpec((None, rows, cols),
                                         lambda s, share=share: (layer + 1, cur(s) // share, 0)))
            out_specs.append(pl.BlockSpec((rows, cols), lambda s, share=share: (cur(s) // share, 0)))
            out_shape.append(jax.ShapeDtypeStruct(w.shape[1:], jnp.bfloat16))
            operands.append(w)
            n_cast += 1

    R = POOL_PAD + T
    scratch = [pltpu.VMEM((S, R, POOL_WIDTH), jnp.float32)] * 4
    scratch += [pltpu.VMEM((S, CONV_PAD + T, FF_CHUNK), jnp.float32),
                pltpu.VMEM((TM, D_MODEL), jnp.bfloat16),
                pltpu.VMEM((TM, D_MODEL), jnp.bfloat16),
                pltpu.VMEM((TM, D_FF), jnp.bfloat16),
                pltpu.VMEM((TM, D_MODEL), jnp.float32)]
    if is_sample:
        seq_steps, n_tiles = 1, grid[0]
    else:
        seq_steps = steps
        scratch.append(pltpu.VMEM((CONV_PAD, D_FF), jnp.float32))

    return pl.pallas_call(
        functools.partial(_layer_body, S, T, is_sample, is_last, n_cast, seq_steps, n_tiles),
        grid=grid,
        in_specs=in_specs,
        out_specs=out_specs,
        out_shape=out_shape,
        scratch_shapes=scratch,
        compiler_params=pltpu.CompilerParams(dimension_semantics=semantics,
                                             vmem_limit_bytes=VMEM_LIMIT_BYTES),
        name=("sample" if is_sample else "prompt") + f"_layer{layer}",
    )(*operands)


def _inv_counts(n_rows):
    win = np.repeat(np.asarray(POOL_WINDOWS, np.float32), POOL_GROUP_DIM)[None, :]
    pos = np.arange(n_rows, dtype=np.float32)[:, None]
    first = 1.0 / np.minimum(pos + 1.0, win)
    later = np.broadcast_to(1.0 / win, (n_rows, POOL_WIDTH))
    return jnp.asarray(np.stack([first, later]).astype(np.float32))


def _spatial_operands(w_spatial, b_spatial, L, n_rows):
    mask = jnp.tril(jnp.ones((L, L), dtype=bool))
    w = jnp.where(mask, w_spatial[:, :, :L, :L], 0.0)
    rows = np.arange(n_rows)
    expand = jnp.asarray((rows[:, None] % L == np.arange(L)[None, :]).astype(np.float32))
    same_block = jnp.asarray(rows[:, None] // L == rows[None, :] // L)
    sp = jnp.einsum('rt,lhts,cs->lhrc', expand, w, expand, precision=lax.Precision.HIGHEST)
    sp = jnp.where(same_block, sp, 0.0).astype(jnp.bfloat16)
    b = jnp.swapaxes(b_spatial[:, :, :L], 1, 2)
    b = jnp.einsum('rt,lth->lrh', expand, b, precision=lax.Precision.HIGHEST)
    return sp, jnp.repeat(b, SG_HEAD_DIM, axis=2)


def kernel(x_prompt, x_sample, state_pool, state_conv, norm1_g, w_in, pool_w, pool_scale, v_norm_g, w_spatial, b_spatial, w_out, norm2_g, w_gate, w_up, conv_w, conv_b, w_down, final_norm_g):
    depth = w_in.shape[0]
    bf = jnp.bfloat16
    eye = jnp.eye(2, dtype=pool_w.dtype)
    pw = jnp.einsum('ab,lkacd->lkacbd', eye, pool_w.reshape(depth, 2, 2, POOL_GROUP_DIM, POOL_GROUP_DIM))
    pw = pw.reshape(depth, 2, POOL_WIDTH // 2, POOL_WIDTH // 2).astype(bf)
    small = (norm1_g[:, None, :], pool_scale[:, None, :], v_norm_g[:, None, :], norm2_g[:, None, :], conv_w,
             conv_b[:, None, :], pw, final_norm_g[None, :])
    big_f32 = (w_in, w_out, w_gate, w_up, w_down)
    spatial_p = _spatial_operands(w_spatial, b_spatial, CHUNK, CHUNK)
    spatial_s = _spatial_operands(w_spatial, b_spatial, x_sample.shape[1], SAMPLE_SP_ROWS)
    invcnt = _inv_counts(TM_PROMPT)

    n_dec, t_dec, _ = x_sample.shape
    xp, xs = x_prompt, x_sample.reshape(n_dec * t_dec, D_MODEL)
    mats = tuple(w[0].astype(bf) for w in big_f32)
    pool_p, pool_s, conv_p, conv_s, v_s = [], [], [], [], []
    for i in range(depth):
        last = i == depth - 1
        res = _run_layer(i, False, last, xp, None, invcnt, small, mats, spatial_p, None if last else big_f32)
        xp, pp, cp = res[:3]
        xs, ps, cs, vs = _run_layer(i, True, last, xs, (state_pool, state_conv), invcnt[1, :TM_SAMPLE], small,
                                    mats, spatial_s, None)
        mats = tuple(res[3:])
        pool_p.append(pp)
        conv_p.append(cp)
        pool_s.append(ps)
        conv_s.append(cs)
        v_s.append(vs.reshape(n_dec, t_dec, SG_WIDTH))
    return (xp, xs.reshape(n_dec, t_dec, D_MODEL), jnp.stack(pool_p), jnp.stack(pool_s), jnp.stack(conv_p),
            jnp.stack(conv_s), jnp.stack(v_s))
```

```python
import functools

import numpy as np
import jax
import jax.numpy as jnp
from jax import lax
from jax.experimental import pallas as pl
from jax.experimental.pallas import tpu as pltpu

D_MODEL = 1024
POOL_WIDTH = 512
POOL_WINDOWS = (2, 4, 8, 16)
POOL_GROUP_DIM = 128
POOL_BUF = 15
POOL_PAD = 16
SG_WIDTH = 512
SG_HEADS = 4
SG_HEAD_DIM = 128
CHUNK = 128
IN_WIDTH = POOL_WIDTH + 2 * SG_WIDTH
D_FF = 2816
CONV_BUF = 2
CONV_PAD = 8
EPS = 1e-6

TM_PROMPT = 512
TM_SAMPLE = 256
SAMPLE_SP_ROWS = 256
FF_CHUNK = 256
TAIL_PARTS = 4
TAIL_COLS = D_MODEL // TAIL_PARTS
VMEM_LIMIT_BYTES = 58 * 1024 * 1024

_INV_SQRT2 = 0.7071067811865476


def _gelu(x):
    return 0.5 * x * (1.0 + lax.erf(x * _INV_SQRT2))


def _rms(x, g):
    ms = jnp.mean(x * x, axis=-1, keepdims=True)
    return x * lax.rsqrt(ms + EPS) * g


def _bdot(a, b):
    return jnp.dot(a, b, preferred_element_type=jnp.float32)


def _layer_body(S, T, is_sample, is_last, n_cast, n_alias, seq_steps, n_tiles, *refs):
    refs = list(refs)
    x_ref = refs.pop(0)
    if is_sample:
        spool_ref = refs.pop(0)
        sconv_ref = refs.pop(0)
    (invcnt_ref, n1_ref, win_ref, pw_ref, pscale_ref, vng_ref, sp_ref, bfull_ref, wout_ref, n2_ref,
     wg_ref, wu_ref, cw_ref, cb_ref, wd_ref, fn_ref) = refs[:16]
    refs = refs[16:]
    cast_in, refs = refs[:n_cast], refs[n_cast:]
    refs = refs[n_alias:]
    y_ref, pool_out_ref, conv_out_ref = refs[:3]
    refs = refs[3:]
    if is_sample:
        v_out_ref = refs.pop(0)
    cast_out, refs = refs[:n_cast], refs[n_cast:]
    pbuf, st8, st4, st2, gbuf, mixbuf, h2buf, actbuf, x1buf = refs[:9]

    for src, dst in zip(cast_in, cast_out):
        dst[...] = src[...].astype(jnp.bfloat16)

    def tail_begin():
        y_ref[...] = x1buf[...]

    def tail_part(k):
        cols = slice(k * TAIL_COLS, (k + 1) * TAIL_COLS)
        y_ref[:, cols] += _bdot(actbuf[...], wd_ref[:, cols])

    def tail_end():
        if is_last:
            y_ref[...] = _rms(y_ref[...], fn_ref[...])

    def tail_all():
        tail_begin()
        for k in range(TAIL_PARTS):
            tail_part(k)
        tail_end()

    front = functools.partial(
        _tile_front, S, T, is_sample, x_ref, invcnt_ref, n1_ref, win_ref, pw_ref, pscale_ref, vng_ref, sp_ref,
        bfull_ref, wout_ref, n2_ref, wg_ref, wu_ref, cw_ref, cb_ref, pool_out_ref, conv_out_ref,
        pbuf, st8, st4, st2, gbuf, mixbuf, h2buf, actbuf, x1buf)

    if is_sample:
        pbuf[:, 0:POOL_PAD - POOL_BUF, :] = jnp.zeros((S, POOL_PAD - POOL_BUF, POOL_WIDTH), jnp.float32)
        pbuf[:, POOL_PAD - POOL_BUF:POOL_PAD, :] = spool_ref[...]
        front(sconv_ref, v_out_ref, None, lambda k: None)
        tail_all()
        return

    s = pl.program_id(0)
    gcarry = refs[9]
    j = lax.rem(s, seq_steps)

    @pl.when(s == 0)
    def _():
        x1buf[...] = jnp.zeros(x1buf.shape, x1buf.dtype)
        actbuf[...] = jnp.zeros(actbuf.shape, actbuf.dtype)

    @pl.when(j == 0)
    def _():
        pbuf[:, 0:POOL_PAD, :] = jnp.zeros((S, POOL_PAD, POOL_WIDTH), jnp.float32)
        gcarry[...] = jnp.zeros((CONV_PAD, D_FF), jnp.float32)

    @pl.when(j > 0)
    def _():
        pbuf[:, 0:POOL_PAD, :] = pbuf[:, T:T + POOL_PAD, :]

    @pl.when(s < n_tiles)
    def _():
        tail_begin()
        front(None, None, gcarry, tail_part)
        tail_end()

    @pl.when(s == n_tiles)
    def _():
        tail_all()


def _tile_front(S, T, is_sample, x_ref, invcnt_ref, n1_ref, win_ref, pw_ref, pscale_ref, vng_ref, sp_ref,
                bfull_ref, wout_ref, n2_ref, wg_ref, wu_ref, cw_ref, cb_ref, pool_out_ref, conv_out_ref,
                pbuf, st8, st4, st2, gbuf, mixbuf, h2buf, actbuf, x1buf, sconv_ref, v_out_ref, gcarry, fill):
    TM = S * T
    R = POOL_PAD + T
    x = x_ref[...]
    fill(0)
    h = _rms(x, n1_ref[...]).astype(jnp.bfloat16)

    p = _bdot(h, win_ref[:, 0:POOL_WIDTH])
    u = _gelu(_bdot(h, win_ref[:, POOL_WIDTH:POOL_WIDTH + SG_WIDTH]))
    v = _rms(_gelu(_bdot(h, win_ref[:, POOL_WIDTH + SG_WIDTH:IN_WIDTH])), vng_ref[...])
    fill(1)
    if is_sample:
        v_out_ref[...] = v
    pbuf[:, POOL_PAD:R, :] = p.reshape(S, T, POOL_WIDTH)
    pool_out_ref[...] = pbuf[:, R - POOL_BUF:R, :]

    zeros8 = jnp.zeros((S, 8, POOL_WIDTH), jnp.float32)
    st8[:, 0:8, :] = zeros8
    st4[:, 0:8, :] = zeros8
    st2[:, 0:8, :] = zeros8
    G = POOL_GROUP_DIM
    st8[:, 8:R, 3 * G:4 * G] = pbuf[:, 8:R, 3 * G:4 * G] + pbuf[:, 0:R - 8, 3 * G:4 * G]
    st4[:, 8:R, 2 * G:3 * G] = pbuf[:, 8:R, 2 * G:3 * G] + pbuf[:, 4:R - 4, 2 * G:3 * G]
    st4[:, 8:R, 3 * G:4 * G] = st8[:, 8:R, 3 * G:4 * G] + st8[:, 4:R - 4, 3 * G:4 * G]
    st2[:, 8:R, 1 * G:2 * G] = pbuf[:, 8:R, 1 * G:2 * G] + pbuf[:, 6:R - 2, 1 * G:2 * G]
    st2[:, 8:R, 2 * G:4 * G] = st4[:, 8:R, 2 * G:4 * G] + st4[:, 6:R - 2, 2 * G:4 * G]
    sum0 = pbuf[:, POOL_PAD:R, 0:G] + pbuf[:, POOL_PAD - 1:R - 1, 0:G]
    sum123 = st2[:, POOL_PAD:R, G:4 * G] + st2[:, POOL_PAD - 1:R - 1, G:4 * G]
    wsum = jnp.concatenate([sum0, sum123], axis=-1).reshape(TM, POOL_WIDTH)
    dpool = wsum * invcnt_ref[...] - p
    d16 = dpool.astype(jnp.bfloat16)
    half = POOL_WIDTH // 2
    for k in range(2):
        ks = slice(k * half, (k + 1) * half)
        a = _bdot(d16[:, ks], pw_ref[k]) * pscale_ref[:, ks]
        mixbuf[:, ks] = a.astype(jnp.bfloat16)

    fill(2)

    v16 = v.astype(jnp.bfloat16)
    for hd in range(SG_HEADS):
        sl = slice(hd * SG_HEAD_DIM, (hd + 1) * SG_HEAD_DIM)
        osl = slice(POOL_WIDTH + hd * SG_HEAD_DIM, POOL_WIDTH + (hd + 1) * SG_HEAD_DIM)
        if is_sample:
            for r in range(TM // SAMPLE_SP_ROWS):
                rs = slice(r * SAMPLE_SP_ROWS, (r + 1) * SAMPLE_SP_ROWS)
                mixed = _bdot(sp_ref[hd], v16[rs, sl]) + bfull_ref[:, sl]
                mixbuf[rs, osl] = (u[rs, sl] * mixed).astype(jnp.bfloat16)
        else:
            chunks = [slice(c * CHUNK, (c + 1) * CHUNK) for c in range(TM // CHUNK)]
            mixed_all = _bdot(sp_ref[hd], jnp.concatenate([v16[rs, sl] for rs in chunks], axis=1))
            for c, rs in enumerate(chunks):
                mixed = mixed_all[:, c * SG_HEAD_DIM:(c + 1) * SG_HEAD_DIM] + bfull_ref[:, sl]
                mixbuf[rs, osl] = (u[rs, sl] * mixed).astype(jnp.bfloat16)

    x1 = x + _bdot(mixbuf[...], wout_ref[...])
    x1buf[...] = x1
    fill(3)
    h2buf[...] = _rms(x1, n2_ref[...]).astype(jnp.bfloat16)

    for c in range(D_FF // FF_CHUNK):
        cs = slice(c * FF_CHUNK, (c + 1) * FF_CHUNK)
        g = _bdot(h2buf[...], wg_ref[:, cs])
        up = _bdot(h2buf[...], wu_ref[:, cs])
        g3 = g.reshape(S, T, FF_CHUNK)
        if is_sample:
            gbuf[:, CONV_PAD - CONV_BUF:CONV_PAD, :] = sconv_ref[:, :, cs]
        else:
            gbuf[0, 0:CONV_PAD, :] = gcarry[:, cs]
            gcarry[:, cs] = g3[0, T - CONV_PAD:T, :]
        gbuf[:, CONV_PAD:CONV_PAD + T, :] = g3
        conv_out_ref[:, :, cs] = g3[:, T - CONV_BUF:T, :]
        conv = (cb_ref[:, cs]
                + cw_ref[0:1, cs] * gbuf[:, CONV_PAD - 2:CONV_PAD - 2 + T, :]
                + cw_ref[1:2, cs] * gbuf[:, CONV_PAD - 1:CONV_PAD - 1 + T, :]
                + cw_ref[2:3, cs] * g3)
        act = _gelu(conv).reshape(TM, FF_CHUNK) * up
        actbuf[:, cs] = act.astype(jnp.bfloat16)


def _const_spec(shape):
    nd = len(shape)
    return pl.BlockSpec(shape, lambda *_: (0,) * nd)


def _layer_spec(shape, layer):
    nd = len(shape)
    return pl.BlockSpec((None,) + tuple(shape), lambda *_: (layer,) + (0,) * nd)


DOWN_CAST_ROWS = 176


def _run_layer(layer, depth, is_sample, is_last, x, state, invcnt, small, mats, spatial, next_f32, stacked):
    (n1, pscale, vng, n2, cw, cb, pw, fn) = small
    (win, wout, wg, wu, wd) = mats
    sp, bfull = spatial
    TM = TM_SAMPLE if is_sample else TM_PROMPT
    f32 = jnp.float32
    if is_sample:
        n_seq = x.shape[0] // 8
        S, T = TM // 8, 8
        seq_steps = 1
        n_tiles = x.shape[0] // TM
        grid = (n_tiles,)
        cur = prev = lambda s: s
        in_specs = [pl.BlockSpec((TM, D_MODEL), lambda s: (cur(s), 0)),
                    pl.BlockSpec((None, S, POOL_BUF, POOL_WIDTH), lambda s: (layer, cur(s), 0, 0)),
                    pl.BlockSpec((None, S, CONV_BUF, D_FF), lambda s: (layer, cur(s), 0, 0)),
                    _const_spec((TM, POOL_WIDTH))]
        operands = [x, state[0], state[1], invcnt]
        out_shape = [jax.ShapeDtypeStruct((x.shape[0], D_MODEL), f32),
                     jax.ShapeDtypeStruct((depth, n_seq, POOL_BUF, POOL_WIDTH), f32),
                     jax.ShapeDtypeStruct((depth, n_seq, CONV_BUF, D_FF), f32),
                     jax.ShapeDtypeStruct((depth, x.shape[0], SG_WIDTH), f32)]
        out_specs = [pl.BlockSpec((TM, D_MODEL), lambda s: (prev(s), 0)),
                     pl.BlockSpec((None, S, POOL_BUF, POOL_WIDTH), lambda s: (layer, cur(s), 0, 0)),
                     pl.BlockSpec((None, S, CONV_BUF, D_FF), lambda s: (layer, cur(s), 0, 0)),
                     pl.BlockSpec((None, TM, SG_WIDTH), lambda s: (layer, cur(s), 0))]
        n_stacked = 3
    else:
        B, L, _ = x.shape
        S, T = 1, TM
        seq_steps = steps = L // TM
        n_tiles = B * steps
        grid = (n_tiles + 1,)
        cur = lambda s: jnp.minimum(s, n_tiles - 1)
        prev = lambda s: jnp.maximum(s - 1, 0)
        in_specs = [pl.BlockSpec((None, TM, D_MODEL), lambda s: (cur(s) // steps, cur(s) % steps, 0)),
                    pl.BlockSpec((None, TM, POOL_WIDTH), lambda s: (jnp.minimum(cur(s) % steps, 1), 0, 0))]
        operands = [x, invcnt]
        out_shape = [jax.ShapeDtypeStruct((B, L, D_MODEL), f32),
                     jax.ShapeDtypeStruct((depth, B, POOL_BUF, POOL_WIDTH), f32),
                     jax.ShapeDtypeStruct((depth, B, CONV_BUF, D_FF), f32)]
        out_specs = [pl.BlockSpec((None, TM, D_MODEL), lambda s: (prev(s) // steps, prev(s) % steps, 0)),
                     pl.BlockSpec((None, 1, POOL_BUF, POOL_WIDTH), lambda s: (layer, cur(s) // steps, 0, 0)),
                     pl.BlockSpec((None, 1, CONV_BUF, D_FF), lambda s: (layer, cur(s) // steps, 0, 0))]
        n_stacked = 2
    semantics = ("arbitrary",)

    in_specs += [
        _layer_spec((1, D_MODEL), layer),
        _const_spec((D_MODEL, IN_WIDTH)),
        _layer_spec((2, POOL_WIDTH // 2, POOL_WIDTH // 2), layer),
        _layer_spec((1, POOL_WIDTH), layer),
        _layer_spec((1, SG_WIDTH), layer),
        _layer_spec(sp.shape[1:], layer),
        _layer_spec(bfull.shape[1:], layer),
        _const_spec((D_MODEL, D_MODEL)),
        _layer_spec((1, D_MODEL), layer),
        _const_spec((D_MODEL, D_FF)),
        _const_spec((D_MODEL, D_FF)),
        _layer_spec((3, D_FF), layer),
        _layer_spec((1, D_FF), layer),
        _const_spec((D_FF, D_MODEL)),
        _const_spec((1, D_MODEL)),
    ]
    operands += [n1, win, pw, pscale, vng, sp, bfull, wout, n2, wg, wu, cw, cb, wd, fn]

    n_cast = 0
    if next_f32 is not None:
        assert not is_sample
        for w in next_f32:
            rows, share = ((D_MODEL // n_tiles, 1) if w.shape[1] == D_MODEL
                           else (DOWN_CAST_ROWS, n_tiles * DOWN_CAST_ROWS // D_FF))
            cols = w.shape[2]
            in_specs.append(pl.BlockSpec((None, rows, cols),
                                         lambda s, share=share: (layer + 1, cur(s) // share, 0)))
            out_specs.append(pl.BlockSpec((rows, cols), lambda s, share=share: (cur(s) // share, 0)))
            out_shape.append(jax.ShapeDtypeStruct(w.shape[1:], jnp.bfloat16))
            operands.append(w)
            n_cast += 1

    aliases = {}
    n_alias = 0
    if stacked is not None:
        assert len(stacked) == n_stacked
        for k, arr in enumerate(stacked):
            aliases[len(operands)] = 1 + k
            in_specs.append(pl.BlockSpec(memory_space=pl.ANY))
            operands.append(arr)
            n_alias += 1

    R = POOL_PAD + T
    scratch = [pltpu.VMEM((S, R, POOL_WIDTH), jnp.float32)] * 4
    scratch += [pltpu.VMEM((S, CONV_PAD + T, FF_CHUNK), jnp.float32),
                pltpu.VMEM((TM, D_MODEL), jnp.bfloat16),
                pltpu.VMEM((TM, D_MODEL), jnp.bfloat16),
                pltpu.VMEM((TM, D_FF), jnp.bfloat16),
                pltpu.VMEM((TM, D_MODEL), jnp.float32)]
    if not is_sample:
        scratch.append(pltpu.VMEM((CONV_PAD, D_FF), jnp.float32))

    return pl.pallas_call(
        functools.partial(_layer_body, S, T, is_sample, is_last, n_cast, n_alias, seq_steps, n_tiles),
        grid=grid,
        in_specs=in_specs,
        out_specs=out_specs,
        out_shape=out_shape,
        scratch_shapes=scratch,
        input_output_aliases=aliases,
        compiler_params=pltpu.CompilerParams(dimension_semantics=semantics,
                                             vmem_limit_bytes=VMEM_LIMIT_BYTES),
        name=("sample" if is_sample else "prompt") + f"_layer{layer}",
    )(*operands)


def _inv_counts(n_rows):
    win = np.repeat(np.asarray(POOL_WINDOWS, np.float32), POOL_GROUP_DIM)[None, :]
    pos = np.arange(n_rows, dtype=np.float32)[:, None]
    first = 1.0 / np.minimum(pos + 1.0, win)
    later = np.broadcast_to(1.0 / win, (n_rows, POOL_WIDTH))
    return jnp.asarray(np.stack([first, later]).astype(np.float32))


def _spatial_operands(w_spatial, b_spatial, L, n_rows):
    mask = jnp.tril(jnp.ones((L, L), dtype=bool))
    w = jnp.where(mask, w_spatial[:, :, :L, :L], 0.0)
    rows = np.arange(n_rows)
    expand = jnp.asarray((rows[:, None] % L == np.arange(L)[None, :]).astype(np.float32))
    same_block = jnp.asarray(rows[:, None] // L == rows[None, :] // L)
    sp = jnp.einsum('rt,lhts,cs->lhrc', expand, w, expand, precision=lax.Precision.HIGHEST)
    sp = jnp.where(same_block, sp, 0.0).astype(jnp.bfloat16)
    b = jnp.swapaxes(b_spatial[:, :, :L], 1, 2)
    b = jnp.einsum('rt,lth->lrh', expand, b, precision=lax.Precision.HIGHEST)
    return sp, jnp.repeat(b, SG_HEAD_DIM, axis=2)


def kernel(x_prompt, x_sample, state_pool, state_conv, norm1_g, w_in, pool_w, pool_scale, v_norm_g, w_spatial, b_spatial, w_out, norm2_g, w_gate, w_up, conv_w, conv_b, w_down, final_norm_g):
    depth = w_in.shape[0]
    bf = jnp.bfloat16
    eye = jnp.eye(2, dtype=pool_w.dtype)
    pw = jnp.einsum('ab,lkacd->lkacbd', eye, pool_w.reshape(depth, 2, 2, POOL_GROUP_DIM, POOL_GROUP_DIM))
    pw = pw.reshape(depth, 2, POOL_WIDTH // 2, POOL_WIDTH // 2).astype(bf)
    small = (norm1_g[:, None, :], pool_scale[:, None, :], v_norm_g[:, None, :], norm2_g[:, None, :], conv_w,
             conv_b[:, None, :], pw, final_norm_g[None, :])
    big_f32 = (w_in, w_out, w_gate, w_up, w_down)
    spatial_p = _spatial_operands(w_spatial, b_spatial, CHUNK, CHUNK)
    spatial_s = _spatial_operands(w_spatial, b_spatial, x_sample.shape[1], SAMPLE_SP_ROWS)
    invcnt = _inv_counts(TM_PROMPT)

    n_dec, t_dec, _ = x_sample.shape
    xp, xs = x_prompt, x_sample.reshape(n_dec * t_dec, D_MODEL)
    mats = tuple(w[0].astype(bf) for w in big_f32)
    stacked_p = stacked_s = None
    for i in range(depth):
        last = i == depth - 1
        res = _run_layer(i, depth, False, last, xp, None, invcnt, small, mats, spatial_p,
                         None if last else big_f32, stacked_p)
        xp, stacked_p = res[0], res[1:3]
        res_s = _run_layer(i, depth, True, last, xs, (state_pool, state_conv), invcnt[1, :TM_SAMPLE], small,
                           mats, spatial_s, None, stacked_s)
        xs, stacked_s = res_s[0], res_s[1:4]
        mats = tuple(res[3:])
    pool_p, conv_p = stacked_p
    pool_s, conv_s, v_s = stacked_s
    return (xp, xs.reshape(n_dec, t_dec, D_MODEL), pool_p, pool_s, conv_p, conv_s,
            v_s.reshape(depth, n_dec, t_dec, SG_WIDTH))
```

```python
import functools

import numpy as np
import jax
import jax.numpy as jnp
from jax import lax
from jax.experimental import pallas as pl
from jax.experimental.pallas import tpu as pltpu

D_MODEL = 1024
POOL_WIDTH = 512
POOL_WINDOWS = (2, 4, 8, 16)
POOL_GROUP_DIM = 128
POOL_BUF = 15
POOL_PAD = 16
SG_WIDTH = 512
SG_HEADS = 4
SG_HEAD_DIM = 128
CHUNK = 128
IN_WIDTH = POOL_WIDTH + 2 * SG_WIDTH
D_FF = 2816
CONV_BUF = 2
CONV_PAD = 8
EPS = 1e-6

TM_PROMPT = 512
TM_SAMPLE = 256
SAMPLE_SP_ROWS = 256
FF_CHUNK = 256
TAIL_PARTS = 4
TAIL_COLS = D_MODEL // TAIL_PARTS
VMEM_LIMIT_BYTES = 58 * 1024 * 1024

_INV_SQRT2 = 0.7071067811865476


def _gelu(x):
    return 0.5 * x * (1.0 + lax.erf(x * _INV_SQRT2))


def _rms(x, g):
    ms = jnp.mean(x * x, axis=-1, keepdims=True)
    return x * lax.rsqrt(ms + EPS) * g


def _bdot(a, b):
    return jnp.dot(a, b, preferred_element_type=jnp.float32)


def _layer_body(S, T, is_sample, is_last, n_cast, n_alias, seq_steps, n_tiles, *refs):
    refs = list(refs)
    x_ref = refs.pop(0)
    if is_sample:
        spool_ref = refs.pop(0)
        sconv_ref = refs.pop(0)
    (invcnt_ref, n1_ref, win_ref, pw_ref, pscale_ref, vng_ref, sp_ref, bfull_ref, wout_ref, n2_ref,
     wg_ref, wu_ref, cw_ref, cb_ref, wd_ref, fn_ref) = refs[:16]
    refs = refs[16:]
    cast_in, refs = refs[:n_cast], refs[n_cast:]
    refs = refs[n_alias:]
    y_ref, pool_out_ref, conv_out_ref = refs[:3]
    refs = refs[3:]
    if is_sample:
        v_out_ref = refs.pop(0)
    cast_out, refs = refs[:n_cast], refs[n_cast:]
    pbuf, st8, st4, st2, gbuf, mixbuf, h2buf, actbuf, x1buf = refs[:9]

    for src, dst in zip(cast_in, cast_out):
        dst[...] = src[...].astype(jnp.bfloat16)

    def tail_begin():
        y_ref[...] = x1buf[...]

    def tail_part(k):
        cols = slice(k * TAIL_COLS, (k + 1) * TAIL_COLS)
        y_ref[:, cols] += _bdot(actbuf[...], wd_ref[:, cols])

    def tail_end():
        if is_last:
            y_ref[...] = _rms(y_ref[...], fn_ref[...])

    def tail_all():
        tail_begin()
        for k in range(TAIL_PARTS):
            tail_part(k)
        tail_end()

    front = functools.partial(
        _tile_front, S, T, is_sample, x_ref, invcnt_ref, n1_ref, win_ref, pw_ref, pscale_ref, vng_ref, sp_ref,
        bfull_ref, wout_ref, n2_ref, wg_ref, wu_ref, cw_ref, cb_ref, pool_out_ref, conv_out_ref,
        pbuf, st8, st4, st2, gbuf, mixbuf, h2buf, actbuf, x1buf)

    if is_sample:
        pbuf[:, 0:POOL_PAD - POOL_BUF, :] = jnp.zeros((S, POOL_PAD - POOL_BUF, POOL_WIDTH), jnp.float32)
        for r in range(POOL_BUF):
            pbuf[:, POOL_PAD - POOL_BUF + r, :] = spool_ref[r]
        front(sconv_ref, v_out_ref, None, lambda k: None)
        tail_all()
        return

    s = pl.program_id(0)
    gcarry = refs[9]
    j = lax.rem(s, seq_steps)

    @pl.when(s == 0)
    def _():
        x1buf[...] = jnp.zeros(x1buf.shape, x1buf.dtype)
        actbuf[...] = jnp.zeros(actbuf.shape, actbuf.dtype)

    @pl.when(j == 0)
    def _():
        pbuf[:, 0:POOL_PAD, :] = jnp.zeros((S, POOL_PAD, POOL_WIDTH), jnp.float32)
        gcarry[...] = jnp.zeros((CONV_PAD, D_FF), jnp.float32)

    @pl.when(j > 0)
    def _():
        pbuf[:, 0:POOL_PAD, :] = pbuf[:, T:T + POOL_PAD, :]

    @pl.when(s < n_tiles)
    def _():
        tail_begin()
        front(None, None, gcarry, tail_part)
        tail_end()

    @pl.when(s == n_tiles)
    def _():
        tail_all()


def _tile_front(S, T, is_sample, x_ref, invcnt_ref, n1_ref, win_ref, pw_ref, pscale_ref, vng_ref, sp_ref,
                bfull_ref, wout_ref, n2_ref, wg_ref, wu_ref, cw_ref, cb_ref, pool_out_ref, conv_out_ref,
                pbuf, st8, st4, st2, gbuf, mixbuf, h2buf, actbuf, x1buf, sconv_ref, v_out_ref, gcarry, fill):
    TM = S * T
    R = POOL_PAD + T
    x = x_ref[...]
    fill(0)
    h = _rms(x, n1_ref[...]).astype(jnp.bfloat16)

    p = _bdot(h, win_ref[:, 0:POOL_WIDTH])
    u = _gelu(_bdot(h, win_ref[:, POOL_WIDTH:POOL_WIDTH + SG_WIDTH]))
    v = _rms(_gelu(_bdot(h, win_ref[:, POOL_WIDTH + SG_WIDTH:IN_WIDTH])), vng_ref[...])
    fill(1)
    if is_sample:
        v_out_ref[...] = v
    pbuf[:, POOL_PAD:R, :] = p.reshape(S, T, POOL_WIDTH)
    pool_out_ref[...] = pbuf[:, R - POOL_BUF:R, :]

    zeros8 = jnp.zeros((S, 8, POOL_WIDTH), jnp.float32)
    st8[:, 0:8, :] = zeros8
    st4[:, 0:8, :] = zeros8
    st2[:, 0:8, :] = zeros8
    G = POOL_GROUP_DIM
    st8[:, 8:R, 3 * G:4 * G] = pbuf[:, 8:R, 3 * G:4 * G] + pbuf[:, 0:R - 8, 3 * G:4 * G]
    st4[:, 8:R, 2 * G:3 * G] = pbuf[:, 8:R, 2 * G:3 * G] + pbuf[:, 4:R - 4, 2 * G:3 * G]
    st4[:, 8:R, 3 * G:4 * G] = st8[:, 8:R, 3 * G:4 * G] + st8[:, 4:R - 4, 3 * G:4 * G]
    st2[:, 8:R, 1 * G:2 * G] = pbuf[:, 8:R, 1 * G:2 * G] + pbuf[:, 6:R - 2, 1 * G:2 * G]
    st2[:, 8:R, 2 * G:4 * G] = st4[:, 8:R, 2 * G:4 * G] + st4[:, 6:R - 2, 2 * G:4 * G]
    sum0 = pbuf[:, POOL_PAD:R, 0:G] + pbuf[:, POOL_PAD - 1:R - 1, 0:G]
    sum123 = st2[:, POOL_PAD:R, G:4 * G] + st2[:, POOL_PAD - 1:R - 1, G:4 * G]
    wsum = jnp.concatenate([sum0, sum123], axis=-1).reshape(TM, POOL_WIDTH)
    dpool = wsum * invcnt_ref[...] - p
    d16 = dpool.astype(jnp.bfloat16)
    half = POOL_WIDTH // 2
    for k in range(2):
        ks = slice(k * half, (k + 1) * half)
        a = _bdot(d16[:, ks], pw_ref[k]) * pscale_ref[:, ks]
        mixbuf[:, ks] = a.astype(jnp.bfloat16)

    fill(2)

    v16 = v.astype(jnp.bfloat16)
    for hd in range(SG_HEADS):
        sl = slice(hd * SG_HEAD_DIM, (hd + 1) * SG_HEAD_DIM)
        osl = slice(POOL_WIDTH + hd * SG_HEAD_DIM, POOL_WIDTH + (hd + 1) * SG_HEAD_DIM)
        if is_sample:
            for r in range(TM // SAMPLE_SP_ROWS):
                rs = slice(r * SAMPLE_SP_ROWS, (r + 1) * SAMPLE_SP_ROWS)
                mixed = _bdot(sp_ref[hd], v16[rs, sl]) + bfull_ref[:, sl]
                mixbuf[rs, osl] = (u[rs, sl] * mixed).astype(jnp.bfloat16)
        else:
            chunks = [slice(c * CHUNK, (c + 1) * CHUNK) for c in range(TM // CHUNK)]
            mixed_all = _bdot(sp_ref[hd], jnp.concatenate([v16[rs, sl] for rs in chunks], axis=1))
            for c, rs in enumerate(chunks):
                mixed = mixed_all[:, c * SG_HEAD_DIM:(c + 1) * SG_HEAD_DIM] + bfull_ref[:, sl]
                mixbuf[rs, osl] = (u[rs, sl] * mixed).astype(jnp.bfloat16)

    x1 = x + _bdot(mixbuf[...], wout_ref[...])
    x1buf[...] = x1
    fill(3)
    h2buf[...] = _rms(x1, n2_ref[...]).astype(jnp.bfloat16)

    for c in range(D_FF // FF_CHUNK):
        cs = slice(c * FF_CHUNK, (c + 1) * FF_CHUNK)
        g = _bdot(h2buf[...], wg_ref[:, cs])
        up = _bdot(h2buf[...], wu_ref[:, cs])
        g3 = g.reshape(S, T, FF_CHUNK)
        if is_sample:
            gbuf[:, CONV_PAD - CONV_BUF:CONV_PAD, :] = sconv_ref[:, :, cs]
        else:
            gbuf[0, 0:CONV_PAD, :] = gcarry[:, cs]
            gcarry[:, cs] = g3[0, T - CONV_PAD:T, :]
        gbuf[:, CONV_PAD:CONV_PAD + T, :] = g3
        conv_out_ref[:, :, cs] = g3[:, T - CONV_BUF:T, :]
        conv = (cb_ref[:, cs]
                + cw_ref[0:1, cs] * gbuf[:, CONV_PAD - 2:CONV_PAD - 2 + T, :]
                + cw_ref[1:2, cs] * gbuf[:, CONV_PAD - 1:CONV_PAD - 1 + T, :]
                + cw_ref[2:3, cs] * g3)
        act = _gelu(conv).reshape(TM, FF_CHUNK) * up
        actbuf[:, cs] = act.astype(jnp.bfloat16)


def _const_spec(shape):
    nd = len(shape)
    return pl.BlockSpec(shape, lambda *_: (0,) * nd)


def _layer_spec(shape, layer):
    nd = len(shape)
    return pl.BlockSpec((None,) + tuple(shape), lambda *_: (layer,) + (0,) * nd)


DOWN_CAST_ROWS = 176


def _run_layer(layer, depth, is_sample, is_last, x, state, invcnt, small, mats, spatial, next_f32, stacked):
    (n1, pscale, vng, n2, cw, cb, pw, fn) = small
    (win, wout, wg, wu, wd) = mats
    sp, bfull = spatial
    TM = TM_SAMPLE if is_sample else TM_PROMPT
    f32 = jnp.float32
    if is_sample:
        n_seq = x.shape[0] // 8
        S, T = TM // 8, 8
        seq_steps = 1
        n_tiles = x.shape[0] // TM
        grid = (n_tiles,)
        cur = prev = lambda s: s
        in_specs = [pl.BlockSpec((TM, D_MODEL), lambda s: (cur(s), 0)),
                    pl.BlockSpec((None, POOL_BUF, S, POOL_WIDTH), lambda s: (layer, 0, cur(s), 0)),
                    pl.BlockSpec((None, S, CONV_BUF, D_FF), lambda s: (layer, cur(s), 0, 0)),
                    _const_spec((TM, POOL_WIDTH))]
        operands = [x, state[0], state[1], invcnt]
        out_shape = [jax.ShapeDtypeStruct((x.shape[0], D_MODEL), f32),
                     jax.ShapeDtypeStruct((depth, n_seq, POOL_BUF, POOL_WIDTH), f32),
                     jax.ShapeDtypeStruct((depth, n_seq, CONV_BUF, D_FF), f32),
                     jax.ShapeDtypeStruct((depth, x.shape[0], SG_WIDTH), f32)]
        out_specs = [pl.BlockSpec((TM, D_MODEL), lambda s: (prev(s), 0)),
                     pl.BlockSpec((None, S, POOL_BUF, POOL_WIDTH), lambda s: (layer, cur(s), 0, 0)),
                     pl.BlockSpec((None, S, CONV_BUF, D_FF), lambda s: (layer, cur(s), 0, 0)),
                     pl.BlockSpec((None, TM, SG_WIDTH), lambda s: (layer, cur(s), 0))]
        n_stacked = 3
    else:
        B, L, _ = x.shape
        S, T = 1, TM
        seq_steps = steps = L // TM
        n_tiles = B * steps
        grid = (n_tiles + 1,)
        cur = lambda s: jnp.minimum(s, n_tiles - 1)
        prev = lambda s: jnp.maximum(s - 1, 0)
        in_specs = [pl.BlockSpec((None, TM, D_MODEL), lambda s: (cur(s) // steps, cur(s) % steps, 0)),
                    pl.BlockSpec((None, TM, POOL_WIDTH), lambda s: (jnp.minimum(cur(s) % steps, 1), 0, 0))]
        operands = [x, invcnt]
        out_shape = [jax.ShapeDtypeStruct((B, L, D_MODEL), f32),
                     jax.ShapeDtypeStruct((depth, B, POOL_BUF, POOL_WIDTH), f32),
                     jax.ShapeDtypeStruct((depth, B, CONV_BUF, D_FF), f32)]
        out_specs = [pl.BlockSpec((None, TM, D_MODEL), lambda s: (prev(s) // steps, prev(s) % steps, 0)),
                     pl.BlockSpec((None, 1, POOL_BUF, POOL_WIDTH), lambda s: (layer, cur(s) // steps, 0, 0)),
                     pl.BlockSpec((None, 1, CONV_BUF, D_FF), lambda s: (layer, cur(s) // steps, 0, 0))]
        n_stacked = 2
    semantics = ("arbitrary",)

    in_specs += [
        _layer_spec((1, D_MODEL), layer),
        _const_spec((D_MODEL, IN_WIDTH)),
        _layer_spec((2, POOL_WIDTH // 2, POOL_WIDTH // 2), layer),
        _layer_spec((1, POOL_WIDTH), layer),
        _layer_spec((1, SG_WIDTH), layer),
        _layer_spec(sp.shape[1:], layer),
        _layer_spec(bfull.shape[1:], layer),
        _const_spec((D_MODEL, D_MODEL)),
        _layer_spec((1, D_MODEL), layer),
        _const_spec((D_MODEL, D_FF)),
        _const_spec((D_MODEL, D_FF)),
        _layer_spec((3, D_FF), layer),
        _layer_spec((1, D_FF), layer),
        _const_spec((D_FF, D_MODEL)),
        _const_spec((1, D_MODEL)),
    ]
    operands += [n1, win, pw, pscale, vng, sp, bfull, wout, n2, wg, wu, cw, cb, wd, fn]

    n_cast = 0
    if next_f32 is not None:
        assert not is_sample
        for w in next_f32:
            rows, share = ((D_MODEL // n_tiles, 1) if w.shape[1] == D_MODEL
                           else (DOWN_CAST_ROWS, n_tiles * DOWN_CAST_ROWS // D_FF))
            cols = w.shape[2]
            in_specs.append(pl.BlockSpec((None, rows, cols),
                                         lambda s, share=share: (layer + 1, cur(s) // share, 0)))
            out_specs.append(pl.BlockSpec((rows, cols), lambda s, share=share: (cur(s) // share, 0)))
            out_shape.append(jax.ShapeDtypeStruct(w.shape[1:], jnp.bfloat16))
            operands.append(w)
            n_cast += 1

    aliases = {}
    n_alias = 0
    if stacked is not None:
        assert len(stacked) == n_stacked
        for k, arr in enumerate(stacked):
            aliases[len(operands)] = 1 + k
            in_specs.append(pl.BlockSpec(memory_space=pl.ANY))
            operands.append(arr)
            n_alias += 1

    R = POOL_PAD + T
    scratch = [pltpu.VMEM((S, R, POOL_WIDTH), jnp.float32)] * 4
    scratch += [pltpu.VMEM((S, CONV_PAD + T, FF_CHUNK), jnp.float32),
                pltpu.VMEM((TM, D_MODEL), jnp.bfloat16),
                pltpu.VMEM((TM, D_MODEL), jnp.bfloat16),
                pltpu.VMEM((TM, D_FF), jnp.bfloat16),
                pltpu.VMEM((TM, D_MODEL), jnp.float32)]
    if not is_sample:
        scratch.append(pltpu.VMEM((CONV_PAD, D_FF), jnp.float32))

    return pl.pallas_call(
        functools.partial(_layer_body, S, T, is_sample, is_last, n_cast, n_alias, seq_steps, n_tiles),
        grid=grid,
        in_specs=in_specs,
        out_specs=out_specs,
        out_shape=out_shape,
        scratch_shapes=scratch,
        input_output_aliases=aliases,
        compiler_params=pltpu.CompilerParams(dimension_semantics=semantics,
                                             vmem_limit_bytes=VMEM_LIMIT_BYTES),
        name=("sample" if is_sample else "prompt") + f"_layer{layer}",
    )(*operands)


def _inv_counts(n_rows):
    win = np.repeat(np.asarray(POOL_WINDOWS, np.float32), POOL_GROUP_DIM)[None, :]
    pos = np.arange(n_rows, dtype=np.float32)[:, None]
    first = 1.0 / np.minimum(pos + 1.0, win)
    later = np.broadcast_to(1.0 / win, (n_rows, POOL_WIDTH))
    return jnp.asarray(np.stack([first, later]).astype(np.float32))


def _spatial_operands(w_spatial, b_spatial, L, n_rows):
    mask = jnp.tril(jnp.ones((L, L), dtype=bool))
    w = jnp.where(mask, w_spatial[:, :, :L, :L], 0.0)
    rows = np.arange(n_rows)
    expand = jnp.asarray((rows[:, None] % L == np.arange(L)[None, :]).astype(np.float32))
    same_block = jnp.asarray(rows[:, None] // L == rows[None, :] // L)
    sp = jnp.einsum('rt,lhts,cs->lhrc', expand, w, expand, precision=lax.Precision.HIGHEST)
    sp = jnp.where(same_block, sp, 0.0).astype(jnp.bfloat16)
    b = jnp.swapaxes(b_spatial[:, :, :L], 1, 2)
    b = jnp.einsum('rt,lth->lrh', expand, b, precision=lax.Precision.HIGHEST)
    return sp, jnp.repeat(b, SG_HEAD_DIM, axis=2)


def kernel(x_prompt, x_sample, state_pool, state_conv, norm1_g, w_in, pool_w, pool_scale, v_norm_g, w_spatial, b_spatial, w_out, norm2_g, w_gate, w_up, conv_w, conv_b, w_down, final_norm_g):
    depth = w_in.shape[0]
    bf = jnp.bfloat16
    eye = jnp.eye(2, dtype=pool_w.dtype)
    pw = jnp.einsum('ab,lkacd->lkacbd', eye, pool_w.reshape(depth, 2, 2, POOL_GROUP_DIM, POOL_GROUP_DIM))
    pw = pw.reshape(depth, 2, POOL_WIDTH // 2, POOL_WIDTH // 2).astype(bf)
    small = (norm1_g[:, None, :], pool_scale[:, None, :], v_norm_g[:, None, :], norm2_g[:, None, :], conv_w,
             conv_b[:, None, :], pw, final_norm_g[None, :])
    big_f32 = (w_in, w_out, w_gate, w_up, w_down)
    spatial_p = _spatial_operands(w_spatial, b_spatial, CHUNK, CHUNK)
    spatial_s = _spatial_operands(w_spatial, b_spatial, x_sample.shape[1], SAMPLE_SP_ROWS)
    invcnt = _inv_counts(TM_PROMPT)

    n_dec, t_dec, _ = x_sample.shape
    xp, xs = x_prompt, x_sample.reshape(n_dec * t_dec, D_MODEL)
    mats = tuple(w[0].astype(bf) for w in big_f32)
    state_pool_t = jnp.transpose(state_pool, (0, 2, 1, 3))
    stacked_p = stacked_s = None
    for i in range(depth):
        last = i == depth - 1
        res = _run_layer(i, depth, False, last, xp, None, invcnt, small, mats, spatial_p,
                         None if last else big_f32, stacked_p)
        xp, stacked_p = res[0], res[1:3]
        res_s = _run_layer(i, depth, True, last, xs, (state_pool_t, state_conv), invcnt[1, :TM_SAMPLE], small,
                           mats, spatial_s, None, stacked_s)
        xs, stacked_s = res_s[0], res_s[1:4]
        mats = tuple(res[3:])
    pool_p, conv_p = stacked_p
    pool_s, conv_s, v_s = stacked_s
    return (xp, xs.reshape(n_dec, t_dec, D_MODEL), pool_p, pool_s, conv_p, conv_s,
            v_s.reshape(depth, n_dec, t_dec, SG_WIDTH))
```

```python
import functools

import numpy as np
import jax
import jax.numpy as jnp
from jax import lax
from jax.experimental import pallas as pl
from jax.experimental.pallas import tpu as pltpu

D_MODEL = 1024
POOL_WIDTH = 512
POOL_WINDOWS = (2, 4, 8, 16)
POOL_GROUP_DIM = 128
POOL_BUF = 15
POOL_PAD = 16
SG_WIDTH = 512
SG_HEADS = 4
SG_HEAD_DIM = 128
CHUNK = 128
IN_WIDTH = POOL_WIDTH + 2 * SG_WIDTH
D_FF = 2816
CONV_BUF = 2
CONV_PAD = 8
EPS = 1e-6

TM_PROMPT = 512
TM_SAMPLE = 256
SAMPLE_SP_ROWS = 256
FF_CHUNK = 256
TAIL_PARTS = 4
TAIL_COLS = D_MODEL // TAIL_PARTS
VMEM_LIMIT_BYTES = 58 * 1024 * 1024

_INV_SQRT2 = 0.7071067811865476


def _gelu(x):
    return 0.5 * x * (1.0 + lax.erf(x * _INV_SQRT2))


def _rms(x, g):
    ms = jnp.mean(x * x, axis=-1, keepdims=True)
    return x * lax.rsqrt(ms + EPS) * g


def _bdot(a, b):
    return jnp.dot(a, b, preferred_element_type=jnp.float32)


def _layer_body(layer, S, T, is_sample, is_last, n_cast, n_alias, seq_steps, n_tiles, *refs):
    refs = list(refs)
    x_ref = refs.pop(0)
    if is_sample:
        spool_ref = refs.pop(0)
        sconv_ref = refs.pop(0)
    (invcnt_ref, n1_ref, win_ref, pw_ref, pscale_ref, vng_ref, sp_ref, bfull_ref, wout_ref, n2_ref,
     wg_ref, wu_ref, cw_ref, cb_ref, wd_ref, fn_ref) = refs[:16]
    refs = refs[16:]
    n1_ref, pscale_ref, vng_ref, n2_ref, cb_ref = (
        r.at[pl.ds(layer, 1)] for r in (n1_ref, pscale_ref, vng_ref, n2_ref, cb_ref))
    cast_in, refs = refs[:n_cast], refs[n_cast:]
    refs = refs[n_alias:]
    y_ref, pool_out_ref, conv_out_ref = refs[:3]
    refs = refs[3:]
    if is_sample:
        v_out_ref = refs.pop(0)
    cast_out, refs = refs[:n_cast], refs[n_cast:]
    pbuf, st8, st4, st2, gbuf, mixbuf, h2buf, actbuf, x1buf = refs[:9]

    for src, dst in zip(cast_in, cast_out):
        dst[...] = src[...].astype(jnp.bfloat16)

    def tail_begin():
        y_ref[...] = x1buf[...]

    def tail_part(k):
        cols = slice(k * TAIL_COLS, (k + 1) * TAIL_COLS)
        y_ref[:, cols] += _bdot(actbuf[...], wd_ref[:, cols])

    def tail_end():
        if is_last:
            y_ref[...] = _rms(y_ref[...], fn_ref[...])

    def tail_all():
        tail_begin()
        for k in range(TAIL_PARTS):
            tail_part(k)
        tail_end()

    front = functools.partial(
        _tile_front, S, T, is_sample, x_ref, invcnt_ref, n1_ref, win_ref, pw_ref, pscale_ref, vng_ref, sp_ref,
        bfull_ref, wout_ref, n2_ref, wg_ref, wu_ref, cw_ref, cb_ref, pool_out_ref, conv_out_ref,
        pbuf, st8, st4, st2, gbuf, mixbuf, h2buf, actbuf, x1buf)

    if is_sample:
        pbuf[:, 0:POOL_PAD - POOL_BUF, :] = jnp.zeros((S, POOL_PAD - POOL_BUF, POOL_WIDTH), jnp.float32)
        for r in range(POOL_BUF):
            pbuf[:, POOL_PAD - POOL_BUF + r, :] = spool_ref[r]
        front(sconv_ref, v_out_ref, None, lambda k: None)
        tail_all()
        return

    s = pl.program_id(0)
    gcarry = refs[9]
    j = lax.rem(s, seq_steps)

    @pl.when(s == 0)
    def _():
        x1buf[...] = jnp.zeros(x1buf.shape, x1buf.dtype)
        actbuf[...] = jnp.zeros(actbuf.shape, actbuf.dtype)

    @pl.when(j == 0)
    def _():
        pbuf[:, 0:POOL_PAD, :] = jnp.zeros((S, POOL_PAD, POOL_WIDTH), jnp.float32)
        gcarry[...] = jnp.zeros((CONV_PAD, D_FF), jnp.float32)

    @pl.when(j > 0)
    def _():
        pbuf[:, 0:POOL_PAD, :] = pbuf[:, T:T + POOL_PAD, :]

    @pl.when(s < n_tiles)
    def _():
        tail_begin()
        front(None, None, gcarry, tail_part)
        tail_end()

    @pl.when(s == n_tiles)
    def _():
        tail_all()


def _tile_front(S, T, is_sample, x_ref, invcnt_ref, n1_ref, win_ref, pw_ref, pscale_ref, vng_ref, sp_ref,
                bfull_ref, wout_ref, n2_ref, wg_ref, wu_ref, cw_ref, cb_ref, pool_out_ref, conv_out_ref,
                pbuf, st8, st4, st2, gbuf, mixbuf, h2buf, actbuf, x1buf, sconv_ref, v_out_ref, gcarry, fill):
    TM = S * T
    R = POOL_PAD + T
    x = x_ref[...]
    fill(0)
    h = _rms(x, n1_ref[...]).astype(jnp.bfloat16)

    p = _bdot(h, win_ref[:, 0:POOL_WIDTH])
    u = _gelu(_bdot(h, win_ref[:, POOL_WIDTH:POOL_WIDTH + SG_WIDTH]))
    v = _rms(_gelu(_bdot(h, win_ref[:, POOL_WIDTH + SG_WIDTH:IN_WIDTH])), vng_ref[...])
    fill(1)
    if is_sample:
        v_out_ref[...] = v
    pbuf[:, POOL_PAD:R, :] = p.reshape(S, T, POOL_WIDTH)
    pool_out_ref[...] = pbuf[:, R - POOL_BUF:R, :]

    zeros8 = jnp.zeros((S, 8, POOL_WIDTH), jnp.float32)
    st8[:, 0:8, :] = zeros8
    st4[:, 0:8, :] = zeros8
    st2[:, 0:8, :] = zeros8
    G = POOL_GROUP_DIM
    st8[:, 8:R, 3 * G:4 * G] = pbuf[:, 8:R, 3 * G:4 * G] + pbuf[:, 0:R - 8, 3 * G:4 * G]
    st4[:, 8:R, 2 * G:3 * G] = pbuf[:, 8:R, 2 * G:3 * G] + pbuf[:, 4:R - 4, 2 * G:3 * G]
    st4[:, 8:R, 3 * G:4 * G] = st8[:, 8:R, 3 * G:4 * G] + st8[:, 4:R - 4, 3 * G:4 * G]
    st2[:, 8:R, 1 * G:2 * G] = pbuf[:, 8:R, 1 * G:2 * G] + pbuf[:, 6:R - 2, 1 * G:2 * G]
    st2[:, 8:R, 2 * G:4 * G] = st4[:, 8:R, 2 * G:4 * G] + st4[:, 6:R - 2, 2 * G:4 * G]
    sum0 = pbuf[:, POOL_PAD:R, 0:G] + pbuf[:, POOL_PAD - 1:R - 1, 0:G]
    sum123 = st2[:, POOL_PAD:R, G:4 * G] + st2[:, POOL_PAD - 1:R - 1, G:4 * G]
    wsum = jnp.concatenate([sum0, sum123], axis=-1).reshape(TM, POOL_WIDTH)
    dpool = wsum * invcnt_ref[...] - p
    d16 = dpool.astype(jnp.bfloat16)
    half = POOL_WIDTH // 2
    for k in range(2):
        ks = slice(k * half, (k + 1) * half)
        a = _bdot(d16[:, ks], pw_ref[k]) * pscale_ref[:, ks]
        mixbuf[:, ks] = a.astype(jnp.bfloat16)

    fill(2)

    v16 = v.astype(jnp.bfloat16)
    for hd in range(SG_HEADS):
        sl = slice(hd * SG_HEAD_DIM, (hd + 1) * SG_HEAD_DIM)
        osl = slice(POOL_WIDTH + hd * SG_HEAD_DIM, POOL_WIDTH + (hd + 1) * SG_HEAD_DIM)
        if is_sample:
            for r in range(TM // SAMPLE_SP_ROWS):
                rs = slice(r * SAMPLE_SP_ROWS, (r + 1) * SAMPLE_SP_ROWS)
                mixed = _bdot(sp_ref[hd], v16[rs, sl]) + bfull_ref[:, sl]
                mixbuf[rs, osl] = (u[rs, sl] * mixed).astype(jnp.bfloat16)
        else:
            chunks = [slice(c * CHUNK, (c + 1) * CHUNK) for c in range(TM // CHUNK)]
            mixed_all = _bdot(sp_ref[hd], jnp.concatenate([v16[rs, sl] for rs in chunks], axis=1))
            for c, rs in enumerate(chunks):
                mixed = mixed_all[:, c * SG_HEAD_DIM:(c + 1) * SG_HEAD_DIM] + bfull_ref[:, sl]
                mixbuf[rs, osl] = (u[rs, sl] * mixed).astype(jnp.bfloat16)

    x1 = x + _bdot(mixbuf[...], wout_ref[...])
    x1buf[...] = x1
    fill(3)
    h2buf[...] = _rms(x1, n2_ref[...]).astype(jnp.bfloat16)

    for c in range(D_FF // FF_CHUNK):
        cs = slice(c * FF_CHUNK, (c + 1) * FF_CHUNK)
        g = _bdot(h2buf[...], wg_ref[:, cs])
        up = _bdot(h2buf[...], wu_ref[:, cs])
        g3 = g.reshape(S, T, FF_CHUNK)
        if is_sample:
            gbuf[:, CONV_PAD - CONV_BUF:CONV_PAD, :] = sconv_ref[:, :, cs]
        else:
            gbuf[0, 0:CONV_PAD, :] = gcarry[:, cs]
            gcarry[:, cs] = g3[0, T - CONV_PAD:T, :]
        gbuf[:, CONV_PAD:CONV_PAD + T, :] = g3
        conv_out_ref[:, :, cs] = g3[:, T - CONV_BUF:T, :]
        conv = (cb_ref[:, cs]
                + cw_ref[0:1, cs] * gbuf[:, CONV_PAD - 2:CONV_PAD - 2 + T, :]
                + cw_ref[1:2, cs] * gbuf[:, CONV_PAD - 1:CONV_PAD - 1 + T, :]
                + cw_ref[2:3, cs] * g3)
        act = _gelu(conv).reshape(TM, FF_CHUNK) * up
        actbuf[:, cs] = act.astype(jnp.bfloat16)


def _const_spec(shape):
    nd = len(shape)
    return pl.BlockSpec(shape, lambda *_: (0,) * nd)


def _layer_spec(shape, layer):
    nd = len(shape)
    return pl.BlockSpec((None,) + tuple(shape), lambda *_: (layer,) + (0,) * nd)


DOWN_CAST_ROWS = 176


def _run_layer(layer, depth, is_sample, is_last, x, state, invcnt, small, mats, spatial, next_f32, stacked):
    (n1, pscale, vng, n2, cw, cb, pw, fn) = small
    (win, wout, wg, wu, wd) = mats
    sp, bfull = spatial
    TM = TM_SAMPLE if is_sample else TM_PROMPT
    f32 = jnp.float32
    if is_sample:
        n_seq = x.shape[0] // 8
        S, T = TM // 8, 8
        seq_steps = 1
        n_tiles = x.shape[0] // TM
        grid = (n_tiles,)
        cur = prev = lambda s: s
        in_specs = [pl.BlockSpec((TM, D_MODEL), lambda s: (cur(s), 0)),
                    pl.BlockSpec((None, POOL_BUF, S, POOL_WIDTH), lambda s: (layer, 0, cur(s), 0)),
                    pl.BlockSpec((None, S, CONV_BUF, D_FF), lambda s: (layer, cur(s), 0, 0)),
                    _const_spec((TM, POOL_WIDTH))]
        operands = [x, state[0], state[1], invcnt]
        out_shape = [jax.ShapeDtypeStruct((x.shape[0], D_MODEL), f32),
                     jax.ShapeDtypeStruct((depth, n_seq, POOL_BUF, POOL_WIDTH), f32),
                     jax.ShapeDtypeStruct((depth, n_seq, CONV_BUF, D_FF), f32),
                     jax.ShapeDtypeStruct((depth, x.shape[0], SG_WIDTH), f32)]
        out_specs = [pl.BlockSpec((TM, D_MODEL), lambda s: (prev(s), 0)),
                     pl.BlockSpec((None, S, POOL_BUF, POOL_WIDTH), lambda s: (layer, cur(s), 0, 0)),
                     pl.BlockSpec((None, S, CONV_BUF, D_FF), lambda s: (layer, cur(s), 0, 0)),
                     pl.BlockSpec((None, TM, SG_WIDTH), lambda s: (layer, cur(s), 0))]
        n_stacked = 3
    else:
        B, L, _ = x.shape
        S, T = 1, TM
        seq_steps = steps = L // TM
        n_tiles = B * steps
        grid = (n_tiles + 1,)
        cur = lambda s: jnp.minimum(s, n_tiles - 1)
        prev = lambda s: jnp.maximum(s - 1, 0)
        in_specs = [pl.BlockSpec((None, TM, D_MODEL), lambda s: (cur(s) // steps, cur(s) % steps, 0)),
                    pl.BlockSpec((None, TM, POOL_WIDTH), lambda s: (jnp.minimum(cur(s) % steps, 1), 0, 0))]
        operands = [x, invcnt]
        out_shape = [jax.ShapeDtypeStruct((B, L, D_MODEL), f32),
                     jax.ShapeDtypeStruct((depth, B, POOL_BUF, POOL_WIDTH), f32),
                     jax.ShapeDtypeStruct((depth, B, CONV_BUF, D_FF), f32)]
        out_specs = [pl.BlockSpec((None, TM, D_MODEL), lambda s: (prev(s) // steps, prev(s) % steps, 0)),
                     pl.BlockSpec((None, 1, POOL_BUF, POOL_WIDTH), lambda s: (layer, cur(s) // steps, 0, 0)),
                     pl.BlockSpec((None, 1, CONV_BUF, D_FF), lambda s: (layer, cur(s) // steps, 0, 0))]
        n_stacked = 2
    semantics = ("arbitrary",)

    in_specs += [
        _const_spec((depth, D_MODEL)),
        _const_spec((D_MODEL, IN_WIDTH)),
        _layer_spec((2, POOL_WIDTH // 2, POOL_WIDTH // 2), layer),
        _const_spec((depth, POOL_WIDTH)),
        _const_spec((depth, SG_WIDTH)),
        _layer_spec(sp.shape[1:], layer),
        _layer_spec(bfull.shape[1:], layer),
        _const_spec((D_MODEL, D_MODEL)),
        _const_spec((depth, D_MODEL)),
        _const_spec((D_MODEL, D_FF)),
        _const_spec((D_MODEL, D_FF)),
        _layer_spec((3, D_FF), layer),
        _const_spec((depth, D_FF)),
        _const_spec((D_FF, D_MODEL)),
        _const_spec((1, D_MODEL)),
    ]
    operands += [n1, win, pw, pscale, vng, sp, bfull, wout, n2, wg, wu, cw, cb, wd, fn]

    n_cast = 0
    if next_f32 is not None:
        assert not is_sample
        for w in next_f32:
            rows, share = ((D_MODEL // n_tiles, 1) if w.shape[1] == D_MODEL
                           else (DOWN_CAST_ROWS, n_tiles * DOWN_CAST_ROWS // D_FF))
            cols = w.shape[2]
            in_specs.append(pl.BlockSpec((None, rows, cols),
                                         lambda s, share=share: (layer + 1, cur(s) // share, 0)))
            out_specs.append(pl.BlockSpec((rows, cols), lambda s, share=share: (cur(s) // share, 0)))
            out_shape.append(jax.ShapeDtypeStruct(w.shape[1:], jnp.bfloat16))
            operands.append(w)
            n_cast += 1

    aliases = {}
    assert len(stacked) == n_stacked
    for k, arr in enumerate(stacked):
        aliases[len(operands)] = 1 + k
        in_specs.append(pl.BlockSpec(memory_space=pl.ANY))
        operands.append(arr)
    n_alias = n_stacked

    R = POOL_PAD + T
    scratch = [pltpu.VMEM((S, R, POOL_WIDTH), jnp.float32)] * 4
    scratch += [pltpu.VMEM((S, CONV_PAD + T, FF_CHUNK), jnp.float32),
                pltpu.VMEM((TM, D_MODEL), jnp.bfloat16),
                pltpu.VMEM((TM, D_MODEL), jnp.bfloat16),
                pltpu.VMEM((TM, D_FF), jnp.bfloat16),
                pltpu.VMEM((TM, D_MODEL), jnp.float32)]
    if not is_sample:
        scratch.append(pltpu.VMEM((CONV_PAD, D_FF), jnp.float32))

    return pl.pallas_call(
        functools.partial(_layer_body, layer, S, T, is_sample, is_last, n_cast, n_alias, seq_steps, n_tiles),
        grid=grid,
        in_specs=in_specs,
        out_specs=out_specs,
        out_shape=out_shape,
        scratch_shapes=scratch,
        input_output_aliases=aliases,
        compiler_params=pltpu.CompilerParams(dimension_semantics=semantics,
                                             vmem_limit_bytes=VMEM_LIMIT_BYTES),
        name=("sample" if is_sample else "prompt") + f"_layer{layer}",
    )(*operands)


def _inv_counts(n_rows):
    win = np.repeat(np.asarray(POOL_WINDOWS, np.float32), POOL_GROUP_DIM)[None, :]
    pos = np.arange(n_rows, dtype=np.float32)[:, None]
    first = 1.0 / np.minimum(pos + 1.0, win)
    later = np.broadcast_to(1.0 / win, (n_rows, POOL_WIDTH))
    return jnp.asarray(np.stack([first, later]).astype(np.float32))


def _spatial_operands(w_spatial, b_spatial, L, n_rows):
    mask = jnp.tril(jnp.ones((L, L), dtype=bool))
    w = jnp.where(mask, w_spatial[:, :, :L, :L], 0.0)
    rows = np.arange(n_rows)
    expand = jnp.asarray((rows[:, None] % L == np.arange(L)[None, :]).astype(np.float32))
    same_block = jnp.asarray(rows[:, None] // L == rows[None, :] // L)
    sp = jnp.einsum('rt,lhts,cs->lhrc', expand, w, expand, precision=lax.Precision.HIGHEST)
    sp = jnp.where(same_block, sp, 0.0).astype(jnp.bfloat16)
    b = jnp.swapaxes(b_spatial[:, :, :L], 1, 2)
    b = jnp.einsum('rt,lth->lrh', expand, b, precision=lax.Precision.HIGHEST)
    return sp, jnp.repeat(b, SG_HEAD_DIM, axis=2)


def kernel(x_prompt, x_sample, state_pool, state_conv, norm1_g, w_in, pool_w, pool_scale, v_norm_g, w_spatial, b_spatial, w_out, norm2_g, w_gate, w_up, conv_w, conv_b, w_down, final_norm_g):
    depth = w_in.shape[0]
    bf = jnp.bfloat16
    eye = jnp.eye(2, dtype=pool_w.dtype)
    pw = jnp.einsum('ab,lkacd->lkacbd', eye, pool_w.reshape(depth, 2, 2, POOL_GROUP_DIM, POOL_GROUP_DIM))
    pw = pw.reshape(depth, 2, POOL_WIDTH // 2, POOL_WIDTH // 2).astype(bf)
    small = (norm1_g, pool_scale, v_norm_g, norm2_g, conv_w, conv_b, pw, final_norm_g[None, :])
    big_f32 = (w_in, w_out, w_gate, w_up, w_down)
    spatial_p = _spatial_operands(w_spatial, b_spatial, CHUNK, CHUNK)
    spatial_s = _spatial_operands(w_spatial, b_spatial, x_sample.shape[1], SAMPLE_SP_ROWS)
    invcnt = _inv_counts(TM_PROMPT)

    n_dec, t_dec, _ = x_sample.shape
    xp, xs = x_prompt, x_sample.reshape(n_dec * t_dec, D_MODEL)
    mats = tuple(w[0].astype(bf) for w in big_f32)
    state_pool_t = jnp.transpose(state_pool, (0, 2, 1, 3))
    n_prompt = x_prompt.shape[0]
    f32 = jnp.float32
    stacked_p = (jnp.zeros((depth, n_prompt, POOL_BUF, POOL_WIDTH), f32),
                 jnp.zeros((depth, n_prompt, CONV_BUF, D_FF), f32))
    stacked_s = (jnp.zeros((depth, n_dec, POOL_BUF, POOL_WIDTH), f32),
                 jnp.zeros((depth, n_dec, CONV_BUF, D_FF), f32),
                 jnp.zeros((depth, n_dec * t_dec, SG_WIDTH), f32))
    for i in range(depth):
        last = i == depth - 1
        res = _run_layer(i, depth, False, last, xp, None, invcnt, small, mats, spatial_p,
                         None if last else big_f32, stacked_p)
        xp, stacked_p = res[0], res[1:3]
        res_s = _run_layer(i, depth, True, last, xs, (state_pool_t, state_conv), invcnt[1, :TM_SAMPLE], small,
                           mats, spatial_s, None, stacked_s)
        xs, stacked_s = res_s[0], res_s[1:4]
        mats = tuple(res[3:])
    pool_p, conv_p = stacked_p
    pool_s, conv_s, v_s = stacked_s
    return (xp, xs.reshape(n_dec, t_dec, D_MODEL), pool_p, pool_s, conv_p, conv_s,
            v_s.reshape(depth, n_dec, t_dec, SG_WIDTH))
```

```python
import functools

import numpy as np
import jax
import jax.numpy as jnp
from jax import lax
from jax.experimental import pallas as pl
from jax.experimental.pallas import tpu as pltpu

D_MODEL = 1024
POOL_WIDTH = 512
POOL_WINDOWS = (2, 4, 8, 16)
POOL_GROUP_DIM = 128
POOL_BUF = 15
POOL_PAD = 16
SG_WIDTH = 512
SG_HEADS = 4
SG_HEAD_DIM = 128
CHUNK = 128
IN_WIDTH = POOL_WIDTH + 2 * SG_WIDTH
D_FF = 2816
CONV_BUF = 2
CONV_PAD = 8
EPS = 1e-6

TM_PROMPT = 512
TM_SAMPLE = 256
SAMPLE_SP_ROWS = 256
FF_CHUNK = 256
TAIL_PARTS = 4
TAIL_COLS = D_MODEL // TAIL_PARTS
VMEM_LIMIT_BYTES = 58 * 1024 * 1024

_INV_SQRT2 = 0.7071067811865476


def _gelu(x):
    return 0.5 * x * (1.0 + lax.erf(x * _INV_SQRT2))


def _rms(x, g):
    ms = jnp.mean(x * x, axis=-1, keepdims=True)
    return x * lax.rsqrt(ms + EPS) * g


def _bdot(a, b):
    return jnp.dot(a, b, preferred_element_type=jnp.float32)


def _layer_body(layer, S, T, is_sample, is_last, n_cast, n_alias, n_zero, seq_steps, n_tiles, *refs):
    refs = list(refs)
    x_ref = refs.pop(0)
    if is_sample:
        spool_ref = refs.pop(0)
        sconv_ref = refs.pop(0)
    (invcnt_ref, n1_ref, win_ref, pw_ref, pscale_ref, vng_ref, sp_ref, bfull_ref, wout_ref, n2_ref,
     wg_ref, wu_ref, cw_ref, cb_ref, wd_ref, fn_ref) = refs[:16]
    refs = refs[16:]
    n1_ref, pscale_ref, vng_ref, n2_ref, cb_ref = (
        r.at[pl.ds(layer, 1)] for r in (n1_ref, pscale_ref, vng_ref, n2_ref, cb_ref))
    cast_in, refs = refs[:n_cast], refs[n_cast:]
    refs = refs[n_alias:]
    y_ref, pool_out_ref, conv_out_ref = refs[:3]
    refs = refs[3:]
    if is_sample:
        v_out_ref = refs.pop(0)
    cast_out, refs = refs[:n_cast], refs[n_cast:]
    zero_out, refs = refs[:n_zero], refs[n_zero:]

    for dst in zero_out:
        dst[...] = jnp.zeros(dst.shape, dst.dtype)
    pbuf, st8, st4, st2, gbuf, mixbuf, h2buf, actbuf, x1buf = refs[:9]

    for src, dst in zip(cast_in, cast_out):
        dst[...] = src[...].astype(jnp.bfloat16)

    def tail_begin():
        y_ref[...] = x1buf[...]

    def tail_part(k):
        cols = slice(k * TAIL_COLS, (k + 1) * TAIL_COLS)
        y_ref[:, cols] += _bdot(actbuf[...], wd_ref[:, cols])

    def tail_end():
        if is_last:
            y_ref[...] = _rms(y_ref[...], fn_ref[...])

    def tail_all():
        tail_begin()
        for k in range(TAIL_PARTS):
            tail_part(k)
        tail_end()

    front = functools.partial(
        _tile_front, S, T, is_sample, x_ref, invcnt_ref, n1_ref, win_ref, pw_ref, pscale_ref, vng_ref, sp_ref,
        bfull_ref, wout_ref, n2_ref, wg_ref, wu_ref, cw_ref, cb_ref, pool_out_ref, conv_out_ref,
        pbuf, st8, st4, st2, gbuf, mixbuf, h2buf, actbuf, x1buf)

    if is_sample:
        pbuf[:, 0:POOL_PAD - POOL_BUF, :] = jnp.zeros((S, POOL_PAD - POOL_BUF, POOL_WIDTH), jnp.float32)
        for r in range(POOL_BUF):
            pbuf[:, POOL_PAD - POOL_BUF + r, :] = spool_ref[r]
        front(sconv_ref, v_out_ref, None, lambda k: None)
        tail_all()
        return

    s = pl.program_id(0)
    gcarry = refs[9]
    j = lax.rem(s, seq_steps)

    @pl.when(s == 0)
    def _():
        x1buf[...] = jnp.zeros(x1buf.shape, x1buf.dtype)
        actbuf[...] = jnp.zeros(actbuf.shape, actbuf.dtype)

    @pl.when(j == 0)
    def _():
        pbuf[:, 0:POOL_PAD, :] = jnp.zeros((S, POOL_PAD, POOL_WIDTH), jnp.float32)
        gcarry[...] = jnp.zeros((CONV_PAD, D_FF), jnp.float32)

    @pl.when(j > 0)
    def _():
        pbuf[:, 0:POOL_PAD, :] = pbuf[:, T:T + POOL_PAD, :]

    @pl.when(s < n_tiles)
    def _():
        tail_begin()
        front(None, None, gcarry, tail_part)
        tail_end()

    @pl.when(s == n_tiles)
    def _():
        tail_all()


def _tile_front(S, T, is_sample, x_ref, invcnt_ref, n1_ref, win_ref, pw_ref, pscale_ref, vng_ref, sp_ref,
                bfull_ref, wout_ref, n2_ref, wg_ref, wu_ref, cw_ref, cb_ref, pool_out_ref, conv_out_ref,
                pbuf, st8, st4, st2, gbuf, mixbuf, h2buf, actbuf, x1buf, sconv_ref, v_out_ref, gcarry, fill):
    TM = S * T
    R = POOL_PAD + T
    x = x_ref[...]
    fill(0)
    h = _rms(x, n1_ref[...]).astype(jnp.bfloat16)

    p = _bdot(h, win_ref[:, 0:POOL_WIDTH])
    u = _gelu(_bdot(h, win_ref[:, POOL_WIDTH:POOL_WIDTH + SG_WIDTH]))
    v = _rms(_gelu(_bdot(h, win_ref[:, POOL_WIDTH + SG_WIDTH:IN_WIDTH])), vng_ref[...])
    fill(1)
    if is_sample:
        v_out_ref[...] = v
    pbuf[:, POOL_PAD:R, :] = p.reshape(S, T, POOL_WIDTH)
    pool_out_ref[...] = pbuf[:, R - POOL_BUF:R, :]

    zeros8 = jnp.zeros((S, 8, POOL_WIDTH), jnp.float32)
    st8[:, 0:8, :] = zeros8
    st4[:, 0:8, :] = zeros8
    st2[:, 0:8, :] = zeros8
    G = POOL_GROUP_DIM
    st8[:, 8:R, 3 * G:4 * G] = pbuf[:, 8:R, 3 * G:4 * G] + pbuf[:, 0:R - 8, 3 * G:4 * G]
    st4[:, 8:R, 2 * G:3 * G] = pbuf[:, 8:R, 2 * G:3 * G] + pbuf[:, 4:R - 4, 2 * G:3 * G]
    st4[:, 8:R, 3 * G:4 * G] = st8[:, 8:R, 3 * G:4 * G] + st8[:, 4:R - 4, 3 * G:4 * G]
    st2[:, 8:R, 1 * G:2 * G] = pbuf[:, 8:R, 1 * G:2 * G] + pbuf[:, 6:R - 2, 1 * G:2 * G]
    st2[:, 8:R, 2 * G:4 * G] = st4[:, 8:R, 2 * G:4 * G] + st4[:, 6:R - 2, 2 * G:4 * G]
    sum0 = pbuf[:, POOL_PAD:R, 0:G] + pbuf[:, POOL_PAD - 1:R - 1, 0:G]
    sum123 = st2[:, POOL_PAD:R, G:4 * G] + st2[:, POOL_PAD - 1:R - 1, G:4 * G]
    wsum = jnp.concatenate([sum0, sum123], axis=-1).reshape(TM, POOL_WIDTH)
    dpool = wsum * invcnt_ref[...] - p
    d16 = dpool.astype(jnp.bfloat16)
    half = POOL_WIDTH // 2
    for k in range(2):
        ks = slice(k * half, (k + 1) * half)
        a = _bdot(d16[:, ks], pw_ref[k]) * pscale_ref[:, ks]
        mixbuf[:, ks] = a.astype(jnp.bfloat16)

    fill(2)

    v16 = v.astype(jnp.bfloat16)
    for hd in range(SG_HEADS):
        sl = slice(hd * SG_HEAD_DIM, (hd + 1) * SG_HEAD_DIM)
        osl = slice(POOL_WIDTH + hd * SG_HEAD_DIM, POOL_WIDTH + (hd + 1) * SG_HEAD_DIM)
        if is_sample:
            for r in range(TM // SAMPLE_SP_ROWS):
                rs = slice(r * SAMPLE_SP_ROWS, (r + 1) * SAMPLE_SP_ROWS)
                mixed = _bdot(sp_ref[hd], v16[rs, sl]) + bfull_ref[:, sl]
                mixbuf[rs, osl] = (u[rs, sl] * mixed).astype(jnp.bfloat16)
        else:
            chunks = [slice(c * CHUNK, (c + 1) * CHUNK) for c in range(TM // CHUNK)]
            mixed_all = _bdot(sp_ref[hd], jnp.concatenate([v16[rs, sl] for rs in chunks], axis=1))
            for c, rs in enumerate(chunks):
                mixed = mixed_all[:, c * SG_HEAD_DIM:(c + 1) * SG_HEAD_DIM] + bfull_ref[:, sl]
                mixbuf[rs, osl] = (u[rs, sl] * mixed).astype(jnp.bfloat16)

    x1 = x + _bdot(mixbuf[...], wout_ref[...])
    x1buf[...] = x1
    fill(3)
    h2buf[...] = _rms(x1, n2_ref[...]).astype(jnp.bfloat16)

    for c in range(D_FF // FF_CHUNK):
        cs = slice(c * FF_CHUNK, (c + 1) * FF_CHUNK)
        g = _bdot(h2buf[...], wg_ref[:, cs])
        up = _bdot(h2buf[...], wu_ref[:, cs])
        g3 = g.reshape(S, T, FF_CHUNK)
        if is_sample:
            gbuf[:, CONV_PAD - CONV_BUF:CONV_PAD, :] = sconv_ref[:, :, cs]
        else:
            gbuf[0, 0:CONV_PAD, :] = gcarry[:, cs]
            gcarry[:, cs] = g3[0, T - CONV_PAD:T, :]
        gbuf[:, CONV_PAD:CONV_PAD + T, :] = g3
        conv_out_ref[:, :, cs] = g3[:, T - CONV_BUF:T, :]
        conv = (cb_ref[:, cs]
                + cw_ref[0:1, cs] * gbuf[:, CONV_PAD - 2:CONV_PAD - 2 + T, :]
                + cw_ref[1:2, cs] * gbuf[:, CONV_PAD - 1:CONV_PAD - 1 + T, :]
                + cw_ref[2:3, cs] * g3)
        act = _gelu(conv).reshape(TM, FF_CHUNK) * up
        actbuf[:, cs] = act.astype(jnp.bfloat16)


def _const_spec(shape):
    nd = len(shape)
    return pl.BlockSpec(shape, lambda *_: (0,) * nd)


def _layer_spec(shape, layer):
    nd = len(shape)
    return pl.BlockSpec((None,) + tuple(shape), lambda *_: (layer,) + (0,) * nd)


DOWN_CAST_ROWS = 176


def _run_layer(layer, depth, is_sample, is_last, x, state, invcnt, small, mats, spatial, next_f32, stacked,
               zero_fill=()):
    (n1, pscale, vng, n2, cw, cb, pw, fn) = small
    (win, wout, wg, wu, wd) = mats
    sp, bfull = spatial
    TM = TM_SAMPLE if is_sample else TM_PROMPT
    f32 = jnp.float32
    if is_sample:
        n_seq = x.shape[0] // 8
        S, T = TM // 8, 8
        seq_steps = 1
        n_tiles = x.shape[0] // TM
        grid = (n_tiles,)
        cur = prev = lambda s: s
        in_specs = [pl.BlockSpec((TM, D_MODEL), lambda s: (cur(s), 0)),
                    pl.BlockSpec((None, POOL_BUF, S, POOL_WIDTH), lambda s: (layer, 0, cur(s), 0)),
                    pl.BlockSpec((None, S, CONV_BUF, D_FF), lambda s: (layer, cur(s), 0, 0)),
                    _const_spec((TM, POOL_WIDTH))]
        operands = [x, state[0], state[1], invcnt]
        out_shape = [jax.ShapeDtypeStruct((x.shape[0], D_MODEL), f32),
                     jax.ShapeDtypeStruct((depth, n_seq, POOL_BUF, POOL_WIDTH), f32),
                     jax.ShapeDtypeStruct((depth, n_seq, CONV_BUF, D_FF), f32),
                     jax.ShapeDtypeStruct((depth, x.shape[0], SG_WIDTH), f32)]
        out_specs = [pl.BlockSpec((TM, D_MODEL), lambda s: (prev(s), 0)),
                     pl.BlockSpec((None, S, POOL_BUF, POOL_WIDTH), lambda s: (layer, cur(s), 0, 0)),
                     pl.BlockSpec((None, S, CONV_BUF, D_FF), lambda s: (layer, cur(s), 0, 0)),
                     pl.BlockSpec((None, TM, SG_WIDTH), lambda s: (layer, cur(s), 0))]
        n_stacked = 3
    else:
        B, L, _ = x.shape
        S, T = 1, TM
        seq_steps = steps = L // TM
        n_tiles = B * steps
        grid = (n_tiles + 1,)
        cur = lambda s: jnp.minimum(s, n_tiles - 1)
        prev = lambda s: jnp.maximum(s - 1, 0)
        in_specs = [pl.BlockSpec((None, TM, D_MODEL), lambda s: (cur(s) // steps, cur(s) % steps, 0)),
                    pl.BlockSpec((None, TM, POOL_WIDTH), lambda s: (jnp.minimum(cur(s) % steps, 1), 0, 0))]
        operands = [x, invcnt]
        out_shape = [jax.ShapeDtypeStruct((B, L, D_MODEL), f32),
                     jax.ShapeDtypeStruct((depth, B, POOL_BUF, POOL_WIDTH), f32),
                     jax.ShapeDtypeStruct((depth, B, CONV_BUF, D_FF), f32)]
        out_specs = [pl.BlockSpec((None, TM, D_MODEL), lambda s: (prev(s) // steps, prev(s) % steps, 0)),
                     pl.BlockSpec((None, 1, POOL_BUF, POOL_WIDTH), lambda s: (layer, cur(s) // steps, 0, 0)),
                     pl.BlockSpec((None, 1, CONV_BUF, D_FF), lambda s: (layer, cur(s) // steps, 0, 0))]
        n_stacked = 2
    semantics = ("arbitrary",)

    in_specs += [
        _const_spec((depth, D_MODEL)),
        _const_spec((D_MODEL, IN_WIDTH)),
        _layer_spec((2, POOL_WIDTH // 2, POOL_WIDTH // 2), layer),
        _const_spec((depth, POOL_WIDTH)),
        _const_spec((depth, SG_WIDTH)),
        _layer_spec(sp.shape[1:], layer),
        _layer_spec(bfull.shape[1:], layer),
        _const_spec((D_MODEL, D_MODEL)),
        _const_spec((depth, D_MODEL)),
        _const_spec((D_MODEL, D_FF)),
        _const_spec((D_MODEL, D_FF)),
        _layer_spec((3, D_FF), layer),
        _const_spec((depth, D_FF)),
        _const_spec((D_FF, D_MODEL)),
        _const_spec((1, D_MODEL)),
    ]
    operands += [n1, win, pw, pscale, vng, sp, bfull, wout, n2, wg, wu, cw, cb, wd, fn]

    n_cast = 0
    if next_f32 is not None:
        assert not is_sample
        for w in next_f32:
            rows, share = ((D_MODEL // n_tiles, 1) if w.shape[1] == D_MODEL
                           else (DOWN_CAST_ROWS, n_tiles * DOWN_CAST_ROWS // D_FF))
            cols = w.shape[2]
            in_specs.append(pl.BlockSpec((None, rows, cols),
                                         lambda s, share=share: (layer + 1, cur(s) // share, 0)))
            out_specs.append(pl.BlockSpec((rows, cols), lambda s, share=share: (cur(s) // share, 0)))
            out_shape.append(jax.ShapeDtypeStruct(w.shape[1:], jnp.bfloat16))
            operands.append(w)
            n_cast += 1

    for shape in zero_fill:
        assert not is_sample and n_tiles % shape[0] == 0
        per_layer = n_tiles // shape[0]
        blk = (1, shape[1] // per_layer) + tuple(shape[2:])
        assert blk[1] * per_layer == shape[1]
        out_specs.append(pl.BlockSpec(
            blk, lambda s, per_layer=per_layer, nd=len(shape): (cur(s) // per_layer, cur(s) % per_layer) + (0,) * (nd - 2)))
        out_shape.append(jax.ShapeDtypeStruct(shape, f32))

    aliases = {}
    assert len(stacked) == n_stacked
    for k, arr in enumerate(stacked):
        aliases[len(operands)] = 1 + k
        in_specs.append(pl.BlockSpec(memory_space=pl.ANY))
        operands.append(arr)
    n_alias = n_stacked

    R = POOL_PAD + T
    scratch = [pltpu.VMEM((S, R, POOL_WIDTH), jnp.float32)] * 4
    scratch += [pltpu.VMEM((S, CONV_PAD + T, FF_CHUNK), jnp.float32),
                pltpu.VMEM((TM, D_MODEL), jnp.bfloat16),
                pltpu.VMEM((TM, D_MODEL), jnp.bfloat16),
                pltpu.VMEM((TM, D_FF), jnp.bfloat16),
                pltpu.VMEM((TM, D_MODEL), jnp.float32)]
    if not is_sample:
        scratch.append(pltpu.VMEM((CONV_PAD, D_FF), jnp.float32))

    return pl.pallas_call(
        functools.partial(_layer_body, layer, S, T, is_sample, is_last, n_cast, n_alias, len(zero_fill), seq_steps,
                          n_tiles),
        grid=grid,
        in_specs=in_specs,
        out_specs=out_specs,
        out_shape=out_shape,
        scratch_shapes=scratch,
        input_output_aliases=aliases,
        compiler_params=pltpu.CompilerParams(dimension_semantics=semantics,
                                             vmem_limit_bytes=VMEM_LIMIT_BYTES),
        name=("sample" if is_sample else "prompt") + f"_layer{layer}",
    )(*operands)


def _inv_counts(n_rows):
    win = np.repeat(np.asarray(POOL_WINDOWS, np.float32), POOL_GROUP_DIM)[None, :]
    pos = np.arange(n_rows, dtype=np.float32)[:, None]
    first = 1.0 / np.minimum(pos + 1.0, win)
    later = np.broadcast_to(1.0 / win, (n_rows, POOL_WIDTH))
    return jnp.asarray(np.stack([first, later]).astype(np.float32))


def _spatial_operands(w_spatial, b_spatial, L, n_rows):
    mask = jnp.tril(jnp.ones((L, L), dtype=bool))
    w = jnp.where(mask, w_spatial[:, :, :L, :L], 0.0)
    rows = np.arange(n_rows)
    expand = jnp.asarray((rows[:, None] % L == np.arange(L)[None, :]).astype(np.float32))
    same_block = jnp.asarray(rows[:, None] // L == rows[None, :] // L)
    sp = jnp.einsum('rt,lhts,cs->lhrc', expand, w, expand, precision=lax.Precision.HIGHEST)
    sp = jnp.where(same_block, sp, 0.0).astype(jnp.bfloat16)
    b = jnp.swapaxes(b_spatial[:, :, :L], 1, 2)
    b = jnp.einsum('rt,lth->lrh', expand, b, precision=lax.Precision.HIGHEST)
    return sp, jnp.repeat(b, SG_HEAD_DIM, axis=2)


def kernel(x_prompt, x_sample, state_pool, state_conv, norm1_g, w_in, pool_w, pool_scale, v_norm_g, w_spatial, b_spatial, w_out, norm2_g, w_gate, w_up, conv_w, conv_b, w_down, final_norm_g):
    depth = w_in.shape[0]
    bf = jnp.bfloat16
    eye = jnp.eye(2, dtype=pool_w.dtype)
    pw = jnp.einsum('ab,lkacd->lkacbd', eye, pool_w.reshape(depth, 2, 2, POOL_GROUP_DIM, POOL_GROUP_DIM))
    pw = pw.reshape(depth, 2, POOL_WIDTH // 2, POOL_WIDTH // 2).astype(bf)
    small = (norm1_g, pool_scale, v_norm_g, norm2_g, conv_w, conv_b, pw, final_norm_g[None, :])
    big_f32 = (w_in, w_out, w_gate, w_up, w_down)
    spatial_p = _spatial_operands(w_spatial, b_spatial, CHUNK, CHUNK)
    spatial_s = _spatial_operands(w_spatial, b_spatial, x_sample.shape[1], SAMPLE_SP_ROWS)
    invcnt = _inv_counts(TM_PROMPT)

    n_dec, t_dec, _ = x_sample.shape
    xp, xs = x_prompt, x_sample.reshape(n_dec * t_dec, D_MODEL)
    mats = tuple(w[0].astype(bf) for w in big_f32)
    state_pool_t = jnp.transpose(state_pool, (0, 2, 1, 3))
    n_prompt = x_prompt.shape[0]
    f32 = jnp.float32
    stacked_p = (jnp.zeros((depth, n_prompt, POOL_BUF, POOL_WIDTH), f32),
                 jnp.zeros((depth, n_prompt, CONV_BUF, D_FF), f32))
    sample_stacked_shapes = ((depth, n_dec, POOL_BUF, POOL_WIDTH), (depth, n_dec, CONV_BUF, D_FF),
                             (depth, n_dec * t_dec, SG_WIDTH))
    stacked_s = None
    for i in range(depth):
        last = i == depth - 1
        res = _run_layer(i, depth, False, last, xp, None, invcnt, small, mats, spatial_p,
                         None if last else big_f32, stacked_p,
                         zero_fill=sample_stacked_shapes if i == 0 else ())
        xp, stacked_p = res[0], res[1:3]
        if i == 0:
            stacked_s = tuple(res[len(res) - len(sample_stacked_shapes):])
        next_mats = tuple(res[3:3 + len(big_f32)])
        res_s = _run_layer(i, depth, True, last, xs, (state_pool_t, state_conv), invcnt[1, :TM_SAMPLE], small,
                           mats, spatial_s, None, stacked_s)
        xs, stacked_s = res_s[0], res_s[1:4]
        mats = next_mats
    pool_p, conv_p = stacked_p
    pool_s, conv_s, v_s = stacked_s
    return (xp, xs.reshape(n_dec, t_dec, D_MODEL), pool_p, pool_s, conv_p, conv_s,
            v_s.reshape(depth, n_dec, t_dec, SG_WIDTH))
```

```python
import functools

import numpy as np
import jax
import jax.numpy as jnp
from jax import lax
from jax.experimental import pallas as pl
from jax.experimental.pallas import tpu as pltpu

D_MODEL = 1024
POOL_WIDTH = 512
POOL_WINDOWS = (2, 4, 8, 16)
POOL_GROUP_DIM = 128
POOL_BUF = 15
POOL_PAD = 16
SG_WIDTH = 512
SG_HEADS = 4
SG_HEAD_DIM = 128
CHUNK = 128
IN_WIDTH = POOL_WIDTH + 2 * SG_WIDTH
D_FF = 2816
CONV_BUF = 2
CONV_PAD = 8
EPS = 1e-6

TM_PROMPT = 512
TM_SAMPLE = 256
SAMPLE_SP_ROWS = 256
FF_CHUNK = 256
TAIL_PARTS = 4
TAIL_COLS = D_MODEL // TAIL_PARTS
VMEM_LIMIT_BYTES = 58 * 1024 * 1024

_INV_SQRT2 = 0.7071067811865476


def _gelu(x):
    return 0.5 * x * (1.0 + lax.erf(x * _INV_SQRT2))


def _rms(x, g):
    ms = jnp.mean(x * x, axis=-1, keepdims=True)
    return x * lax.rsqrt(ms + EPS) * g


def _bdot(a, b):
    return jnp.dot(a, b, preferred_element_type=jnp.float32)


def _layer_body(layer, S, T, is_sample, is_last, n_cast, n_alias, n_zero, seq_steps, n_tiles, *refs):
    refs = list(refs)
    x_ref = refs.pop(0)
    if is_sample:
        spool_ref = refs.pop(0)
        sconv_ref = refs.pop(0)
    (invcnt_ref, n1_ref, win_ref, pw_ref, pscale_ref, vng_ref, sp_ref, bfull_ref, wout_ref, n2_ref,
     wg_ref, wu_ref, cw_ref, cb_ref, wd_ref, fn_ref) = refs[:16]
    refs = refs[16:]
    n1_ref, pscale_ref, vng_ref, n2_ref, cb_ref = (
        r.at[pl.ds(layer, 1)] for r in (n1_ref, pscale_ref, vng_ref, n2_ref, cb_ref))
    cast_in, refs = refs[:n_cast], refs[n_cast:]
    refs = refs[n_alias:]
    y_ref, pool_out_ref, conv_out_ref = refs[:3]
    refs = refs[3:]
    if is_sample:
        v_out_ref = refs.pop(0)
    cast_out, refs = refs[:n_cast], refs[n_cast:]
    zero_out, refs = refs[:n_zero], refs[n_zero:]

    for dst in zero_out:
        dst[...] = jnp.zeros(dst.shape, dst.dtype)
    pbuf, st8, st4, st2, gbuf, mixbuf, h2buf, actbuf, x1buf = refs[:9]

    for src, dst in zip(cast_in, cast_out):
        dst[...] = src[...].astype(jnp.bfloat16)

    def tail_begin():
        y_ref[...] = x1buf[...]

    def tail_part(k):
        cols = slice(k * TAIL_COLS, (k + 1) * TAIL_COLS)
        y_ref[:, cols] += _bdot(actbuf[...], wd_ref[:, cols])

    def tail_end():
        if is_last:
            y_ref[...] = _rms(y_ref[...], fn_ref[...])

    def tail_all():
        tail_begin()
        for k in range(TAIL_PARTS):
            tail_part(k)
        tail_end()

    front = functools.partial(
        _tile_front, S, T, is_sample, x_ref, invcnt_ref, n1_ref, win_ref, pw_ref, pscale_ref, vng_ref, sp_ref,
        bfull_ref, wout_ref, n2_ref, wg_ref, wu_ref, cw_ref, cb_ref, pool_out_ref, conv_out_ref,
        pbuf, st8, st4, st2, gbuf, mixbuf, h2buf, actbuf, x1buf)

    if is_sample:
        pbuf[:, 0:POOL_PAD - POOL_BUF, :] = jnp.zeros((S, POOL_PAD - POOL_BUF, POOL_WIDTH), jnp.float32)
        for r in range(POOL_BUF):
            pbuf[:, POOL_PAD - POOL_BUF + r, :] = spool_ref[r]
        front(sconv_ref, v_out_ref, None, lambda k: None)
        tail_all()
        return

    s = pl.program_id(0)
    gcarry = refs[9]
    j = lax.rem(s, seq_steps)

    @pl.when(s == 0)
    def _():
        x1buf[...] = jnp.zeros(x1buf.shape, x1buf.dtype)
        actbuf[...] = jnp.zeros(actbuf.shape, actbuf.dtype)

    @pl.when(j == 0)
    def _():
        pbuf[:, 0:POOL_PAD, :] = jnp.zeros((S, POOL_PAD, POOL_WIDTH), jnp.float32)
        gcarry[...] = jnp.zeros((CONV_PAD, D_FF), jnp.float32)

    @pl.when(j > 0)
    def _():
        pbuf[:, 0:POOL_PAD, :] = pbuf[:, T:T + POOL_PAD, :]

    def fill(k):
        tail_part(k)
        if k == TAIL_PARTS - 1:
            tail_end()

    @pl.when(s < n_tiles)
    def _():
        tail_begin()
        front(None, None, gcarry, fill)

    @pl.when(s == n_tiles)
    def _():
        tail_all()


def _tile_front(S, T, is_sample, x_ref, invcnt_ref, n1_ref, win_ref, pw_ref, pscale_ref, vng_ref, sp_ref,
                bfull_ref, wout_ref, n2_ref, wg_ref, wu_ref, cw_ref, cb_ref, pool_out_ref, conv_out_ref,
                pbuf, st8, st4, st2, gbuf, mixbuf, h2buf, actbuf, x1buf, sconv_ref, v_out_ref, gcarry, fill):
    TM = S * T
    R = POOL_PAD + T
    x = x_ref[...]
    fill(0)
    h = _rms(x, n1_ref[...]).astype(jnp.bfloat16)

    p = _bdot(h, win_ref[:, 0:POOL_WIDTH])
    u = _gelu(_bdot(h, win_ref[:, POOL_WIDTH:POOL_WIDTH + SG_WIDTH]))
    v = _rms(_gelu(_bdot(h, win_ref[:, POOL_WIDTH + SG_WIDTH:IN_WIDTH])), vng_ref[...])
    fill(1)
    if is_sample:
        v_out_ref[...] = v
    pbuf[:, POOL_PAD:R, :] = p.reshape(S, T, POOL_WIDTH)
    pool_out_ref[...] = pbuf[:, R - POOL_BUF:R, :]

    zeros8 = jnp.zeros((S, 8, POOL_WIDTH), jnp.float32)
    st8[:, 0:8, :] = zeros8
    st4[:, 0:8, :] = zeros8
    st2[:, 0:8, :] = zeros8
    G = POOL_GROUP_DIM
    st8[:, 8:R, 3 * G:4 * G] = pbuf[:, 8:R, 3 * G:4 * G] + pbuf[:, 0:R - 8, 3 * G:4 * G]
    st4[:, 8:R, 2 * G:3 * G] = pbuf[:, 8:R, 2 * G:3 * G] + pbuf[:, 4:R - 4, 2 * G:3 * G]
    st4[:, 8:R, 3 * G:4 * G] = st8[:, 8:R, 3 * G:4 * G] + st8[:, 4:R - 4, 3 * G:4 * G]
    st2[:, 8:R, 1 * G:2 * G] = pbuf[:, 8:R, 1 * G:2 * G] + pbuf[:, 6:R - 2, 1 * G:2 * G]
    st2[:, 8:R, 2 * G:4 * G] = st4[:, 8:R, 2 * G:4 * G] + st4[:, 6:R - 2, 2 * G:4 * G]
    sum0 = pbuf[:, POOL_PAD:R, 0:G] + pbuf[:, POOL_PAD - 1:R - 1, 0:G]
    sum123 = st2[:, POOL_PAD:R, G:4 * G] + st2[:, POOL_PAD - 1:R - 1, G:4 * G]
    wsum = jnp.concatenate([sum0, sum123], axis=-1).reshape(TM, POOL_WIDTH)
    dpool = wsum * invcnt_ref[...] - p
    d16 = dpool.astype(jnp.bfloat16)
    half = POOL_WIDTH // 2
    for k in range(2):
        ks = slice(k * half, (k + 1) * half)
        a = _bdot(d16[:, ks], pw_ref[k]) * pscale_ref[:, ks]
        mixbuf[:, ks] = a.astype(jnp.bfloat16)

    fill(2)

    v16 = v.astype(jnp.bfloat16)
    for hd in range(SG_HEADS):
        sl = slice(hd * SG_HEAD_DIM, (hd + 1) * SG_HEAD_DIM)
        osl = slice(POOL_WIDTH + hd * SG_HEAD_DIM, POOL_WIDTH + (hd + 1) * SG_HEAD_DIM)
        if is_sample:
            for r in range(TM // SAMPLE_SP_ROWS):
                rs = slice(r * SAMPLE_SP_ROWS, (r + 1) * SAMPLE_SP_ROWS)
                mixed = _bdot(sp_ref[hd], v16[rs, sl]) + bfull_ref[:, sl]
                mixbuf[rs, osl] = (u[rs, sl] * mixed).astype(jnp.bfloat16)
        else:
            chunks = [slice(c * CHUNK, (c + 1) * CHUNK) for c in range(TM // CHUNK)]
            mixed_all = _bdot(sp_ref[hd], jnp.concatenate([v16[rs, sl] for rs in chunks], axis=1))
            for c, rs in enumerate(chunks):
                mixed = mixed_all[:, c * SG_HEAD_DIM:(c + 1) * SG_HEAD_DIM] + bfull_ref[:, sl]
                mixbuf[rs, osl] = (u[rs, sl] * mixed).astype(jnp.bfloat16)

    x1 = x + _bdot(mixbuf[...], wout_ref[...])
    x1buf[...] = x1
    fill(3)
    h2buf[...] = _rms(x1, n2_ref[...]).astype(jnp.bfloat16)

    for c in range(D_FF // FF_CHUNK):
        cs = slice(c * FF_CHUNK, (c + 1) * FF_CHUNK)
        g = _bdot(h2buf[...], wg_ref[:, cs])
        up = _bdot(h2buf[...], wu_ref[:, cs])
        g3 = g.reshape(S, T, FF_CHUNK)
        if is_sample:
            gbuf[:, CONV_PAD - CONV_BUF:CONV_PAD, :] = sconv_ref[:, :, cs]
        else:
            gbuf[0, 0:CONV_PAD, :] = gcarry[:, cs]
            gcarry[:, cs] = g3[0, T - CONV_PAD:T, :]
        gbuf[:, CONV_PAD:CONV_PAD + T, :] = g3
        conv_out_ref[:, :, cs] = g3[:, T - CONV_BUF:T, :]
        conv = (cb_ref[:, cs]
                + cw_ref[0:1, cs] * gbuf[:, CONV_PAD - 2:CONV_PAD - 2 + T, :]
                + cw_ref[1:2, cs] * gbuf[:, CONV_PAD - 1:CONV_PAD - 1 + T, :]
                + cw_ref[2:3, cs] * g3)
        act = _gelu(conv).reshape(TM, FF_CHUNK) * up
        actbuf[:, cs] = act.astype(jnp.bfloat16)


def _const_spec(shape):
    nd = len(shape)
    return pl.BlockSpec(shape, lambda *_: (0,) * nd)


def _layer_spec(shape, layer):
    nd = len(shape)
    return pl.BlockSpec((None,) + tuple(shape), lambda *_: (layer,) + (0,) * nd)


DOWN_CAST_ROWS = 176


def _run_layer(layer, depth, is_sample, is_last, x, state, invcnt, small, mats, spatial, next_f32, stacked,
               zero_fill=()):
    (n1, pscale, vng, n2, cw, cb, pw, fn) = small
    (win, wout, wg, wu, wd) = mats
    sp, bfull = spatial
    TM = TM_SAMPLE if is_sample else TM_PROMPT
    f32 = jnp.float32
    if is_sample:
        n_seq = x.shape[0] // 8
        S, T = TM // 8, 8
        seq_steps = 1
        n_tiles = x.shape[0] // TM
        grid = (n_tiles,)
        cur = prev = lambda s: s
        in_specs = [pl.BlockSpec((TM, D_MODEL), lambda s: (cur(s), 0)),
                    pl.BlockSpec((None, POOL_BUF, S, POOL_WIDTH), lambda s: (layer, 0, cur(s), 0)),
                    pl.BlockSpec((None, S, CONV_BUF, D_FF), lambda s: (layer, cur(s), 0, 0)),
                    _const_spec((TM, POOL_WIDTH))]
        operands = [x, state[0], state[1], invcnt]
        out_shape = [jax.ShapeDtypeStruct((x.shape[0], D_MODEL), f32),
                     jax.ShapeDtypeStruct((depth, n_seq, POOL_BUF, POOL_WIDTH), f32),
                     jax.ShapeDtypeStruct((depth, n_seq, CONV_BUF, D_FF), f32),
                     jax.ShapeDtypeStruct((depth, x.shape[0], SG_WIDTH), f32)]
        out_specs = [pl.BlockSpec((TM, D_MODEL), lambda s: (prev(s), 0)),
                     pl.BlockSpec((None, S, POOL_BUF, POOL_WIDTH), lambda s: (layer, cur(s), 0, 0)),
                     pl.BlockSpec((None, S, CONV_BUF, D_FF), lambda s: (layer, cur(s), 0, 0)),
                     pl.BlockSpec((None, TM, SG_WIDTH), lambda s: (layer, cur(s), 0))]
        n_stacked = 3
    else:
        B, L, _ = x.shape
        S, T = 1, TM
        seq_steps = steps = L // TM
        n_tiles = B * steps
        grid = (n_tiles + 1,)
        cur = lambda s: jnp.minimum(s, n_tiles - 1)
        prev = lambda s: jnp.maximum(s - 1, 0)
        in_specs = [pl.BlockSpec((None, TM, D_MODEL), lambda s: (cur(s) // steps, cur(s) % steps, 0)),
                    pl.BlockSpec((None, TM, POOL_WIDTH), lambda s: (jnp.minimum(cur(s) % steps, 1), 0, 0))]
        operands = [x, invcnt]
        out_shape = [jax.ShapeDtypeStruct((B, L, D_MODEL), f32),
                     jax.ShapeDtypeStruct((depth, B, POOL_BUF, POOL_WIDTH), f32),
                     jax.ShapeDtypeStruct((depth, B, CONV_BUF, D_FF), f32)]
        out_specs = [pl.BlockSpec((None, TM, D_MODEL), lambda s: (prev(s) // steps, prev(s) % steps, 0)),
                     pl.BlockSpec((None, 1, POOL_BUF, POOL_WIDTH), lambda s: (layer, cur(s) // steps, 0, 0)),
                     pl.BlockSpec((None, 1, CONV_BUF, D_FF), lambda s: (layer, cur(s) // steps, 0, 0))]
        n_stacked = 2
    semantics = ("arbitrary",)

    in_specs += [
        _const_spec((depth, D_MODEL)),
        _const_spec((D_MODEL, IN_WIDTH)),
        _layer_spec((2, POOL_WIDTH // 2, POOL_WIDTH // 2), layer),
        _const_spec((depth, POOL_WIDTH)),
        _const_spec((depth, SG_WIDTH)),
        _layer_spec(sp.shape[1:], layer),
        _layer_spec(bfull.shape[1:], layer),
        _const_spec((D_MODEL, D_MODEL)),
        _const_spec((depth, D_MODEL)),
        _const_spec((D_MODEL, D_FF)),
        _const_spec((D_MODEL, D_FF)),
        _layer_spec((3, D_FF), layer),
        _const_spec((depth, D_FF)),
        _const_spec((D_FF, D_MODEL)),
        _const_spec((1, D_MODEL)),
    ]
    operands += [n1, win, pw, pscale, vng, sp, bfull, wout, n2, wg, wu, cw, cb, wd, fn]

    n_cast = 0
    if next_f32 is not None:
        assert not is_sample
        for w in next_f32:
            rows, share = ((D_MODEL // n_tiles, 1) if w.shape[1] == D_MODEL
                           else (DOWN_CAST_ROWS, n_tiles * DOWN_CAST_ROWS // D_FF))
            cols = w.shape[2]
            in_specs.append(pl.BlockSpec((None, rows, cols),
                                         lambda s, share=share: (layer + 1, cur(s) // share, 0)))
            out_specs.append(pl.BlockSpec((rows, cols), lambda s, share=share: (cur(s) // share, 0)))
            out_shape.append(jax.ShapeDtypeStruct(w.shape[1:], jnp.bfloat16))
            operands.append(w)
            n_cast += 1

    for shape in zero_fill:
        assert not is_sample and n_tiles % shape[0] == 0
        per_layer = n_tiles // shape[0]
        blk = (1, shape[1] // per_layer) + tuple(shape[2:])
        assert blk[1] * per_layer == shape[1]
        out_specs.append(pl.BlockSpec(
            blk, lambda s, per_layer=per_layer, nd=len(shape): (cur(s) // per_layer, cur(s) % per_layer) + (0,) * (nd - 2)))
        out_shape.append(jax.ShapeDtypeStruct(shape, f32))

    aliases = {}
    assert len(stacked) == n_stacked
    for k, arr in enumerate(stacked):
        aliases[len(operands)] = 1 + k
        in_specs.append(pl.BlockSpec(memory_space=pl.ANY))
        operands.append(arr)
    n_alias = n_stacked

    R = POOL_PAD + T
    scratch = [pltpu.VMEM((S, R, POOL_WIDTH), jnp.float32)] * 4
    scratch += [pltpu.VMEM((S, CONV_PAD + T, FF_CHUNK), jnp.float32),
                pltpu.VMEM((TM, D_MODEL), jnp.bfloat16),
                pltpu.VMEM((TM, D_MODEL), jnp.bfloat16),
                pltpu.VMEM((TM, D_FF), jnp.bfloat16),
                pltpu.VMEM((TM, D_MODEL), jnp.float32)]
    if not is_sample:
        scratch.append(pltpu.VMEM((CONV_PAD, D_FF), jnp.float32))

    return pl.pallas_call(
        functools.partial(_layer_body, layer, S, T, is_sample, is_last, n_cast, n_alias, len(zero_fill), seq_steps,
                          n_tiles),
        grid=grid,
        in_specs=in_specs,
        out_specs=out_specs,
        out_shape=out_shape,
        scratch_shapes=scratch,
        input_output_aliases=aliases,
        compiler_params=pltpu.CompilerParams(dimension_semantics=semantics,
                                             vmem_limit_bytes=VMEM_LIMIT_BYTES),
        name=("sample" if is_sample else "prompt") + f"_layer{layer}",
    )(*operands)


def _inv_counts(n_rows):
    win = np.repeat(np.asarray(POOL_WINDOWS, np.float32), POOL_GROUP_DIM)[None, :]
    pos = np.arange(n_rows, dtype=np.float32)[:, None]
    first = 1.0 / np.minimum(pos + 1.0, win)
    later = np.broadcast_to(1.0 / win, (n_rows, POOL_WIDTH))
    return jnp.asarray(np.stack([first, later]).astype(np.float32))


def _spatial_operands(w_spatial, b_spatial, L, n_rows):
    reps = n_rows // L
    rows = np.arange(n_rows)
    keep = jnp.asarray((rows[:, None] // L == rows[None, :] // L) & (rows[:, None] % L >= rows[None, :] % L))
    sp = jnp.where(keep, jnp.tile(w_spatial[:, :, :L, :L], (1, 1, reps, reps)), 0.0).astype(jnp.bfloat16)
    b = jnp.tile(jnp.swapaxes(b_spatial[:, :, :L], 1, 2), (1, reps, 1))
    return sp, jnp.repeat(b, SG_HEAD_DIM, axis=2)


def kernel(x_prompt, x_sample, state_pool, state_conv, norm1_g, w_in, pool_w, pool_scale, v_norm_g, w_spatial, b_spatial, w_out, norm2_g, w_gate, w_up, conv_w, conv_b, w_down, final_norm_g):
    depth = w_in.shape[0]
    bf = jnp.bfloat16
    eye = jnp.eye(2, dtype=pool_w.dtype)
    pw = jnp.einsum('ab,lkacd->lkacbd', eye, pool_w.reshape(depth, 2, 2, POOL_GROUP_DIM, POOL_GROUP_DIM))
    pw = pw.reshape(depth, 2, POOL_WIDTH // 2, POOL_WIDTH // 2).astype(bf)
    small = (norm1_g, pool_scale, v_norm_g, norm2_g, conv_w, conv_b, pw, final_norm_g[None, :])
    big_f32 = (w_in, w_out, w_gate, w_up, w_down)
    spatial_p = _spatial_operands(w_spatial, b_spatial, CHUNK, CHUNK)
    spatial_s = _spatial_operands(w_spatial, b_spatial, x_sample.shape[1], SAMPLE_SP_ROWS)
    invcnt = _inv_counts(TM_PROMPT)

    n_dec, t_dec, _ = x_sample.shape
    xp, xs = x_prompt, x_sample.reshape(n_dec * t_dec, D_MODEL)
    mats = tuple(w[0].astype(bf) for w in big_f32)
    state_pool_t = jnp.transpose(state_pool, (0, 2, 1, 3))
    n_prompt = x_prompt.shape[0]
    f32 = jnp.float32
    stacked_p = (jnp.zeros((depth, n_prompt, POOL_BUF, POOL_WIDTH), f32),
                 jnp.zeros((depth, n_prompt, CONV_BUF, D_FF), f32))
    sample_stacked_shapes = ((depth, n_dec, POOL_BUF, POOL_WIDTH), (depth, n_dec, CONV_BUF, D_FF),
                             (depth, n_dec * t_dec, SG_WIDTH))
    stacked_s = None
    for i in range(depth):
        last = i == depth - 1
        res = _run_layer(i, depth, False, last, xp, None, invcnt, small, mats, spatial_p,
                         None if last else big_f32, stacked_p,
                         zero_fill=sample_stacked_shapes if i == 0 else ())
        xp, stacked_p = res[0], res[1:3]
        if i == 0:
            stacked_s = tuple(res[len(res) - len(sample_stacked_shapes):])
        next_mats = tuple(res[3:3 + len(big_f32)])
        res_s = _run_layer(i, depth, True, last, xs, (state_pool_t, state_conv), invcnt[1, :TM_SAMPLE], small,
                           mats, spatial_s, None, stacked_s)
        xs, stacked_s = res_s[0], res_s[1:4]
        mats = next_mats
    pool_p, conv_p = stacked_p
    pool_s, conv_s, v_s = stacked_s
    return (xp, xs.reshape(n_dec, t_dec, D_MODEL), pool_p, pool_s, conv_p, conv_s,
            v_s.reshape(depth, n_dec, t_dec, SG_WIDTH))
```

```python
import functools

import numpy as np
import jax
import jax.numpy as jnp
from jax import lax
from jax.experimental import pallas as pl
from jax.experimental.pallas import tpu as pltpu

D_MODEL = 1024
POOL_WIDTH = 512
POOL_WINDOWS = (2, 4, 8, 16)
POOL_GROUP_DIM = 128
POOL_BUF = 15
POOL_PAD = 16
SG_WIDTH = 512
SG_HEADS = 4
SG_HEAD_DIM = 128
CHUNK = 128
IN_WIDTH = POOL_WIDTH + 2 * SG_WIDTH
D_FF = 2816
CONV_BUF = 2
CONV_PAD = 8
EPS = 1e-6

TM_PROMPT = 512
TM_SAMPLE = 256
SAMPLE_SP_ROWS = 256
FF_CHUNK = 256
TAIL_PARTS = 4
TAIL_COLS = D_MODEL // TAIL_PARTS
VMEM_LIMIT_BYTES = 58 * 1024 * 1024

_INV_SQRT2 = 0.7071067811865476


def _gelu(x):
    return 0.5 * x * (1.0 + lax.erf(x * _INV_SQRT2))


def _rms(x, g):
    ms = jnp.mean(x * x, axis=-1, keepdims=True)
    return x * lax.rsqrt(ms + EPS) * g


def _bdot(a, b):
    return jnp.dot(a, b, preferred_element_type=jnp.float32)


def _layer_body(layer, S, T, is_sample, is_last, n_cast, n_alias, n_zero, seq_steps, n_tiles, *refs):
    refs = list(refs)
    x_ref = refs.pop(0)
    if is_sample:
        spool_ref = refs.pop(0)
        sconv_ref = refs.pop(0)
    (invcnt_ref, n1_ref, win_ref, pw_ref, pscale_ref, vng_ref, sp_ref, bfull_ref, wout_ref, n2_ref,
     wg_ref, wu_ref, cw_ref, cb_ref, wd_ref, fn_ref) = refs[:16]
    refs = refs[16:]
    n1_ref, pscale_ref, vng_ref, n2_ref, cb_ref = (
        r.at[pl.ds(layer, 1)] for r in (n1_ref, pscale_ref, vng_ref, n2_ref, cb_ref))
    cast_in, refs = refs[:n_cast], refs[n_cast:]
    refs = refs[n_alias:]
    y_ref, pool_out_ref, conv_out_ref = refs[:3]
    refs = refs[3:]
    if is_sample:
        v_out_ref = refs.pop(0)
    cast_out, refs = refs[:n_cast], refs[n_cast:]
    zero_out, refs = refs[:n_zero], refs[n_zero:]

    for dst in zero_out:
        dst[...] = jnp.zeros(dst.shape, dst.dtype)
    pbuf, st8, st4, st2, gbuf, mixbuf, h2buf, actbuf, x1buf = refs[:9]

    for src, dst in zip(cast_in, cast_out):
        dst[...] = src[...].astype(jnp.bfloat16)

    def tail_begin():
        y_ref[...] = x1buf[...]

    def tail_part(k):
        cols = slice(k * TAIL_COLS, (k + 1) * TAIL_COLS)
        y_ref[:, cols] += _bdot(actbuf[...], wd_ref[:, cols])

    def tail_end():
        if is_last:
            y_ref[...] = _rms(y_ref[...], fn_ref[...])

    def tail_all():
        tail_begin()
        for k in range(TAIL_PARTS):
            tail_part(k)
        tail_end()

    front = functools.partial(
        _tile_front, S, T, is_sample, x_ref, invcnt_ref, n1_ref, win_ref, pw_ref, pscale_ref, vng_ref, sp_ref,
        bfull_ref, wout_ref, n2_ref, wg_ref, wu_ref, cw_ref, cb_ref, pool_out_ref, conv_out_ref,
        pbuf, st8, st4, st2, gbuf, mixbuf, h2buf, actbuf, x1buf)

    if is_sample:
        pbuf[:, 0:POOL_PAD - POOL_BUF, :] = jnp.zeros((S, POOL_PAD - POOL_BUF, POOL_WIDTH), jnp.float32)
        for r in range(POOL_BUF):
            pbuf[:, POOL_PAD - POOL_BUF + r, :] = spool_ref[r]
        front(sconv_ref, v_out_ref, None, lambda k: None)
        tail_all()
        return

    s = pl.program_id(0)
    gcarry = refs[9]
    j = lax.rem(s, seq_steps)

    @pl.when(s == 0)
    def _():
        x1buf[...] = jnp.zeros(x1buf.shape, x1buf.dtype)
        actbuf[...] = jnp.zeros(actbuf.shape, actbuf.dtype)

    @pl.when(j == 0)
    def _():
        pbuf[:, 0:POOL_PAD, :] = jnp.zeros((S, POOL_PAD, POOL_WIDTH), jnp.float32)
        gcarry[...] = jnp.zeros((CONV_PAD, D_FF), jnp.float32)

    @pl.when(j > 0)
    def _():
        pbuf[:, 0:POOL_PAD, :] = pbuf[:, T:T + POOL_PAD, :]

    @pl.when(s < n_tiles)
    def _():
        tail_begin()
        front(None, None, gcarry, tail_part)
        tail_end()

    @pl.when(s == n_tiles)
    def _():
        tail_all()


def _tile_front(S, T, is_sample, x_ref, invcnt_ref, n1_ref, win_ref, pw_ref, pscale_ref, vng_ref, sp_ref,
                bfull_ref, wout_ref, n2_ref, wg_ref, wu_ref, cw_ref, cb_ref, pool_out_ref, conv_out_ref,
                pbuf, st8, st4, st2, gbuf, mixbuf, h2buf, actbuf, x1buf, sconv_ref, v_out_ref, gcarry, fill):
    TM = S * T
    R = POOL_PAD + T
    x = x_ref[...]
    fill(0)
    h = _rms(x, n1_ref[...]).astype(jnp.bfloat16)

    p = _bdot(h, win_ref[:, 0:POOL_WIDTH])
    u = _gelu(_bdot(h, win_ref[:, POOL_WIDTH:POOL_WIDTH + SG_WIDTH]))
    v = _rms(_gelu(_bdot(h, win_ref[:, POOL_WIDTH + SG_WIDTH:IN_WIDTH])), vng_ref[...])
    fill(1)
    if is_sample:
        v_out_ref[...] = v
    pbuf[:, POOL_PAD:R, :] = p.reshape(S, T, POOL_WIDTH)
    pool_out_ref[...] = pbuf[:, R - POOL_BUF:R, :]

    zeros8 = jnp.zeros((S, 8, POOL_WIDTH), jnp.float32)
    st8[:, 0:8, :] = zeros8
    st4[:, 0:8, :] = zeros8
    st2[:, 0:8, :] = zeros8
    G = POOL_GROUP_DIM
    st8[:, 8:R, 3 * G:4 * G] = pbuf[:, 8:R, 3 * G:4 * G] + pbuf[:, 0:R - 8, 3 * G:4 * G]
    st4[:, 8:R, 2 * G:3 * G] = pbuf[:, 8:R, 2 * G:3 * G] + pbuf[:, 4:R - 4, 2 * G:3 * G]
    st4[:, 8:R, 3 * G:4 * G] = st8[:, 8:R, 3 * G:4 * G] + st8[:, 4:R - 4, 3 * G:4 * G]
    st2[:, 8:R, 1 * G:2 * G] = pbuf[:, 8:R, 1 * G:2 * G] + pbuf[:, 6:R - 2, 1 * G:2 * G]
    st2[:, 8:R, 2 * G:4 * G] = st4[:, 8:R, 2 * G:4 * G] + st4[:, 6:R - 2, 2 * G:4 * G]
    sum0 = pbuf[:, POOL_PAD:R, 0:G] + pbuf[:, POOL_PAD - 1:R - 1, 0:G]
    sum123 = st2[:, POOL_PAD:R, G:4 * G] + st2[:, POOL_PAD - 1:R - 1, G:4 * G]
    wsum = jnp.concatenate([sum0, sum123], axis=-1).reshape(TM, POOL_WIDTH)
    dpool = wsum * invcnt_ref[...] - p
    d16 = dpool.astype(jnp.bfloat16)
    half = POOL_WIDTH // 2
    for k in range(2):
        ks = slice(k * half, (k + 1) * half)
        a = _bdot(d16[:, ks], pw_ref[k]) * pscale_ref[:, ks]
        mixbuf[:, ks] = a.astype(jnp.bfloat16)

    fill(2)

    v16 = v.astype(jnp.bfloat16)
    for hd in range(SG_HEADS):
        sl = slice(hd * SG_HEAD_DIM, (hd + 1) * SG_HEAD_DIM)
        osl = slice(POOL_WIDTH + hd * SG_HEAD_DIM, POOL_WIDTH + (hd + 1) * SG_HEAD_DIM)
        if is_sample:
            for r in range(TM // SAMPLE_SP_ROWS):
                rs = slice(r * SAMPLE_SP_ROWS, (r + 1) * SAMPLE_SP_ROWS)
                mixed = _bdot(sp_ref[hd], v16[rs, sl]) + bfull_ref[:, sl]
                mixbuf[rs, osl] = (u[rs, sl] * mixed).astype(jnp.bfloat16)
        else:
            chunks = [slice(c * CHUNK, (c + 1) * CHUNK) for c in range(TM // CHUNK)]
            mixed_all = _bdot(sp_ref[hd], jnp.concatenate([v16[rs, sl] for rs in chunks], axis=1))
            for c, rs in enumerate(chunks):
                mixed = mixed_all[:, c * SG_HEAD_DIM:(c + 1) * SG_HEAD_DIM] + bfull_ref[:, sl]
                mixbuf[rs, osl] = (u[rs, sl] * mixed).astype(jnp.bfloat16)

    x1 = x + _bdot(mixbuf[...], wout_ref[...])
    x1buf[...] = x1
    fill(3)
    h2buf[...] = _rms(x1, n2_ref[...]).astype(jnp.bfloat16)

    for c in range(D_FF // FF_CHUNK):
        cs = slice(c * FF_CHUNK, (c + 1) * FF_CHUNK)
        g = _bdot(h2buf[...], wg_ref[:, cs])
        up = _bdot(h2buf[...], wu_ref[:, cs])
        g3 = g.reshape(S, T, FF_CHUNK)
        if is_sample:
            gbuf[:, CONV_PAD - CONV_BUF:CONV_PAD, :] = sconv_ref[:, :, cs]
        else:
            gbuf[0, 0:CONV_PAD, :] = gcarry[:, cs]
            gcarry[:, cs] = g3[0, T - CONV_PAD:T, :]
        gbuf[:, CONV_PAD:CONV_PAD + T, :] = g3
        conv_out_ref[:, :, cs] = g3[:, T - CONV_BUF:T, :]
        conv = (cb_ref[:, cs]
                + cw_ref[0:1, cs] * gbuf[:, CONV_PAD - 2:CONV_PAD - 2 + T, :]
                + cw_ref[1:2, cs] * gbuf[:, CONV_PAD - 1:CONV_PAD - 1 + T, :]
                + cw_ref[2:3, cs] * g3)
        act = _gelu(conv).reshape(TM, FF_CHUNK) * up
        actbuf[:, cs] = act.astype(jnp.bfloat16)


def _const_spec(shape):
    nd = len(shape)
    return pl.BlockSpec(shape, lambda *_: (0,) * nd)


def _layer_spec(shape, layer):
    nd = len(shape)
    return pl.BlockSpec((None,) + tuple(shape), lambda *_: (layer,) + (0,) * nd)


DOWN_CAST_ROWS = 176


def _run_layer(layer, depth, is_sample, is_last, x, state, invcnt, small, mats, spatial, next_f32, stacked,
               zero_fill=()):
    (n1, pscale, vng, n2, cw, cb, pw, fn) = small
    (win, wout, wg, wu, wd) = mats
    sp, bfull = spatial
    TM = TM_SAMPLE if is_sample else TM_PROMPT
    f32 = jnp.float32
    if is_sample:
        n_seq = state[1].shape[1]
        T = x.shape[0] // n_seq
        S = TM // T
        seq_steps = 1
        n_tiles = x.shape[0] // TM
        grid = (n_tiles,)
        cur = prev = lambda s: s
        in_specs = [pl.BlockSpec((TM, D_MODEL), lambda s: (cur(s), 0)),
                    pl.BlockSpec((None, POOL_BUF, S, POOL_WIDTH), lambda s: (layer, 0, cur(s), 0)),
                    pl.BlockSpec((None, S, CONV_BUF, D_FF), lambda s: (layer, cur(s), 0, 0)),
                    _const_spec((TM, POOL_WIDTH))]
        operands = [x, state[0], state[1], invcnt]
        out_shape = [jax.ShapeDtypeStruct((x.shape[0], D_MODEL), f32),
                     jax.ShapeDtypeStruct((depth, n_seq, POOL_BUF, POOL_WIDTH), f32),
                     jax.ShapeDtypeStruct((depth, n_seq, CONV_BUF, D_FF), f32),
                     jax.ShapeDtypeStruct((depth, x.shape[0], SG_WIDTH), f32)]
        out_specs = [pl.BlockSpec((TM, D_MODEL), lambda s: (prev(s), 0)),
                     pl.BlockSpec((None, S, POOL_BUF, POOL_WIDTH), lambda s: (layer, cur(s), 0, 0)),
                     pl.BlockSpec((None, S, CONV_BUF, D_FF), lambda s: (layer, cur(s), 0, 0)),
                     pl.BlockSpec((None, TM, SG_WIDTH), lambda s: (layer, cur(s), 0))]
        n_stacked = 3
    else:
        B, L, _ = x.shape
        S, T = 1, TM
        seq_steps = steps = L // TM
        n_tiles = B * steps
        grid = (n_tiles + 1,)
        cur = lambda s: jnp.minimum(s, n_tiles - 1)
        prev = lambda s: jnp.maximum(s - 1, 0)
        in_specs = [pl.BlockSpec((None, TM, D_MODEL), lambda s: (cur(s) // steps, cur(s) % steps, 0)),
                    pl.BlockSpec((None, TM, POOL_WIDTH), lambda s: (jnp.minimum(cur(s) % steps, 1), 0, 0))]
        operands = [x, invcnt]
        out_shape = [jax.ShapeDtypeStruct((B, L, D_MODEL), f32),
                     jax.ShapeDtypeStruct((depth, B, POOL_BUF, POOL_WIDTH), f32),
                     jax.ShapeDtypeStruct((depth, B, CONV_BUF, D_FF), f32)]
        out_specs = [pl.BlockSpec((None, TM, D_MODEL), lambda s: (prev(s) // steps, prev(s) % steps, 0)),
                     pl.BlockSpec((None, 1, POOL_BUF, POOL_WIDTH), lambda s: (layer, cur(s) // steps, 0, 0)),
                     pl.BlockSpec((None, 1, CONV_BUF, D_FF), lambda s: (layer, cur(s) // steps, 0, 0))]
        n_stacked = 2
    semantics = ("arbitrary",)

    in_specs += [
        _const_spec((depth, D_MODEL)),
        _const_spec((D_MODEL, IN_WIDTH)),
        _layer_spec((2, POOL_WIDTH // 2, POOL_WIDTH // 2), layer),
        _const_spec((depth, POOL_WIDTH)),
        _const_spec((depth, SG_WIDTH)),
        _layer_spec(sp.shape[1:], layer),
        _layer_spec(bfull.shape[1:], layer),
        _const_spec((D_MODEL, D_MODEL)),
        _const_spec((depth, D_MODEL)),
        _const_spec((D_MODEL, D_FF)),
        _const_spec((D_MODEL, D_FF)),
        _layer_spec((3, D_FF), layer),
        _const_spec((depth, D_FF)),
        _const_spec((D_FF, D_MODEL)),
        _const_spec((1, D_MODEL)),
    ]
    operands += [n1, win, pw, pscale, vng, sp, bfull, wout, n2, wg, wu, cw, cb, wd, fn]

    n_cast = 0
    if next_f32 is not None:
        assert not is_sample
        for w in next_f32:
            rows, share = ((D_MODEL // n_tiles, 1) if w.shape[1] == D_MODEL
                           else (DOWN_CAST_ROWS, n_tiles * DOWN_CAST_ROWS // D_FF))
            cols = w.shape[2]
            in_specs.append(pl.BlockSpec((None, rows, cols),
                                         lambda s, share=share: (layer + 1, cur(s) // share, 0)))
            out_specs.append(pl.BlockSpec((rows, cols), lambda s, share=share: (cur(s) // share, 0)))
            out_shape.append(jax.ShapeDtypeStruct(w.shape[1:], jnp.bfloat16))
            operands.append(w)
            n_cast += 1

    for shape in zero_fill:
        assert not is_sample and n_tiles % shape[0] == 0
        per_layer = n_tiles // shape[0]
        blk = (1, shape[1] // per_layer) + tuple(shape[2:])
        assert blk[1] * per_layer == shape[1]
        out_specs.append(pl.BlockSpec(
            blk, lambda s, per_layer=per_layer, nd=len(shape): (cur(s) // per_layer, cur(s) % per_layer) + (0,) * (nd - 2)))
        out_shape.append(jax.ShapeDtypeStruct(shape, f32))

    aliases = {}
    assert len(stacked) == n_stacked
    for k, arr in enumerate(stacked):
        aliases[len(operands)] = 1 + k
        in_specs.append(pl.BlockSpec(memory_space=pl.ANY))
        operands.append(arr)
    n_alias = n_stacked

    R = POOL_PAD + T
    scratch = [pltpu.VMEM((S, R, POOL_WIDTH), jnp.float32)] * 4
    scratch += [pltpu.VMEM((S, CONV_PAD + T, FF_CHUNK), jnp.float32),
                pltpu.VMEM((TM, D_MODEL), jnp.bfloat16),
                pltpu.VMEM((TM, D_MODEL), jnp.bfloat16),
                pltpu.VMEM((TM, D_FF), jnp.bfloat16),
                pltpu.VMEM((TM, D_MODEL), jnp.float32)]
    if not is_sample:
        scratch.append(pltpu.VMEM((CONV_PAD, D_FF), jnp.float32))

    return pl.pallas_call(
        functools.partial(_layer_body, layer, S, T, is_sample, is_last, n_cast, n_alias, len(zero_fill), seq_steps,
                          n_tiles),
        grid=grid,
        in_specs=in_specs,
        out_specs=out_specs,
        out_shape=out_shape,
        scratch_shapes=scratch,
        input_output_aliases=aliases,
        compiler_params=pltpu.CompilerParams(dimension_semantics=semantics,
                                             vmem_limit_bytes=VMEM_LIMIT_BYTES),
        name=("sample" if is_sample else "prompt") + f"_layer{layer}",
    )(*operands)


def _inv_counts(n_rows):
    win = np.repeat(np.asarray(POOL_WINDOWS, np.float32), POOL_GROUP_DIM)[None, :]
    pos = np.arange(n_rows, dtype=np.float32)[:, None]
    first = 1.0 / np.minimum(pos + 1.0, win)
    later = np.broadcast_to(1.0 / win, (n_rows, POOL_WIDTH))
    return jnp.asarray(np.stack([first, later]).astype(np.float32))


def _spatial_operands(w_spatial, b_spatial, L, n_rows):
    mask = jnp.tril(jnp.ones((L, L), dtype=bool))
    w = jnp.where(mask, w_spatial[:, :, :L, :L], 0.0)
    b = jnp.swapaxes(b_spatial[:, :, :L], 1, 2)
    if n_rows == L:
        return w.astype(jnp.bfloat16), jnp.repeat(b, SG_HEAD_DIM, axis=2)
    rows = np.arange(n_rows)
    expand = jnp.asarray((rows[:, None] % L == np.arange(L)[None, :]).astype(np.float32))
    same_block = jnp.asarray(rows[:, None] // L == rows[None, :] // L)
    sp = jnp.einsum('rt,lhts,cs->lhrc', expand, w, expand, precision=lax.Precision.HIGHEST)
    sp = jnp.where(same_block, sp, 0.0).astype(jnp.bfloat16)
    b = jnp.einsum('rt,lth->lrh', expand, b, precision=lax.Precision.HIGHEST)
    return sp, jnp.repeat(b, SG_HEAD_DIM, axis=2)


def kernel(x_prompt, x_sample, state_pool, state_conv, norm1_g, w_in, pool_w, pool_scale, v_norm_g, w_spatial, b_spatial, w_out, norm2_g, w_gate, w_up, conv_w, conv_b, w_down, final_norm_g):
    depth = w_in.shape[0]
    bf = jnp.bfloat16
    eye = jnp.eye(2, dtype=pool_w.dtype)
    pw = jnp.einsum('ab,lkacd->lkacbd', eye, pool_w.reshape(depth, 2, 2, POOL_GROUP_DIM, POOL_GROUP_DIM))
    pw = pw.reshape(depth, 2, POOL_WIDTH // 2, POOL_WIDTH // 2).astype(bf)
    small = (norm1_g, pool_scale, v_norm_g, norm2_g, conv_w, conv_b, pw, final_norm_g[None, :])
    big_f32 = (w_in, w_out, w_gate, w_up, w_down)
    spatial_p = _spatial_operands(w_spatial, b_spatial, CHUNK, CHUNK)
    spatial_s = _spatial_operands(w_spatial, b_spatial, x_sample.shape[1], SAMPLE_SP_ROWS)
    invcnt = _inv_counts(TM_PROMPT)

    n_dec, t_dec, _ = x_sample.shape
    xp, xs = x_prompt, x_sample.reshape(n_dec * t_dec, D_MODEL)
    mats = tuple(w[0].astype(bf) for w in big_f32)
    state_pool_t = jnp.transpose(state_pool, (0, 2, 1, 3))
    n_prompt = x_prompt.shape[0]
    f32 = jnp.float32
    stacked_p = (jnp.zeros((depth, n_prompt, POOL_BUF, POOL_WIDTH), f32),
                 jnp.zeros((depth, n_prompt, CONV_BUF, D_FF), f32))
    sample_stacked_shapes = ((depth, n_dec, POOL_BUF, POOL_WIDTH), (depth, n_dec, CONV_BUF, D_FF),
                             (depth, n_dec * t_dec, SG_WIDTH))
    stacked_s = None
    for i in range(depth):
        last = i == depth - 1
        res = _run_layer(i, depth, False, last, xp, None, invcnt, small, mats, spatial_p,
                         None if last else big_f32, stacked_p,
                         zero_fill=sample_stacked_shapes if i == 0 else ())
        xp, stacked_p = res[0], res[1:3]
        if i == 0:
            stacked_s = tuple(res[len(res) - len(sample_stacked_shapes):])
        next_mats = tuple(res[3:3 + len(big_f32)])
        res_s = _run_layer(i, depth, True, last, xs, (state_pool_t, state_conv), invcnt[1, :TM_SAMPLE], small,
                           mats, spatial_s, None, stacked_s)
        xs, stacked_s = res_s[0], res_s[1:4]
        mats = next_mats
    pool_p, conv_p = stacked_p
    pool_s, conv_s, v_s = stacked_s
    return (xp, xs.reshape(n_dec, t_dec, D_MODEL), pool_p, pool_s, conv_p, conv_s,
            v_s.reshape(depth, n_dec, t_dec, SG_WIDTH))
```

```python
import functools

import numpy as np
import jax
import jax.numpy as jnp
from jax import lax
from jax.experimental import pallas as pl
from jax.experimental.pallas import tpu as pltpu

D_MODEL = 1024
POOL_WIDTH = 512
POOL_WINDOWS = (2, 4, 8, 16)
POOL_GROUP_DIM = 128
POOL_BUF = 15
POOL_PAD = 16
SG_WIDTH = 512
SG_HEADS = 4
SG_HEAD_DIM = 128
CHUNK = 128
IN_WIDTH = POOL_WIDTH + 2 * SG_WIDTH
D_FF = 2816
CONV_BUF = 2
CONV_PAD = 8
EPS = 1e-6

TM_PROMPT = 512
TM_SAMPLE = 256
SAMPLE_SP_ROWS = 256
FF_CHUNK = 256
TAIL_PARTS = 4
TAIL_COLS = D_MODEL // TAIL_PARTS
VMEM_LIMIT_BYTES = 58 * 1024 * 1024

_INV_SQRT2 = 0.7071067811865476


def _gelu(x):
    return 0.5 * x * (1.0 + lax.erf(x * _INV_SQRT2))


def _rms(x, g):
    ms = jnp.mean(x * x, axis=-1, keepdims=True)
    return x * lax.rsqrt(ms + EPS) * g


def _bdot(a, b):
    return jnp.dot(a, b, preferred_element_type=jnp.float32)


def _layer_body(layer, S, T, is_sample, is_last, n_cast, n_alias, n_zero, seq_steps, n_tiles, *refs):
    refs = list(refs)
    x_ref = refs.pop(0)
    if is_sample:
        spool_ref = refs.pop(0)
        sconv_ref = refs.pop(0)
    (invcnt_ref, n1_ref, win_ref, pw_ref, pscale_ref, vng_ref, sp_ref, bfull_ref, wout_ref, n2_ref,
     wg_ref, wu_ref, cw_ref, cb_ref, wd_ref, fn_ref) = refs[:16]
    refs = refs[16:]
    n1_ref, pscale_ref, vng_ref, n2_ref, cb_ref = (
        r.at[pl.ds(layer, 1)] for r in (n1_ref, pscale_ref, vng_ref, n2_ref, cb_ref))
    cast_in, refs = refs[:n_cast], refs[n_cast:]
    refs = refs[n_alias:]
    y_ref, pool_out_ref, conv_out_ref = refs[:3]
    refs = refs[3:]
    if is_sample:
        v_out_ref = refs.pop(0)
    cast_out, refs = refs[:n_cast], refs[n_cast:]
    zero_out, refs = refs[:n_zero], refs[n_zero:]

    for dst in zero_out:
        dst[...] = jnp.zeros(dst.shape, dst.dtype)
    pbuf, st8, st4, st2, gbuf, mixbuf, h2buf, actbuf, x1buf = refs[:9]

    for src, dst in zip(cast_in, cast_out):
        dst[...] = src[...].astype(jnp.bfloat16)

    def tail_begin():
        y_ref[...] = x1buf[...]

    def tail_part(k):
        cols = slice(k * TAIL_COLS, (k + 1) * TAIL_COLS)
        y_ref[:, cols] += _bdot(actbuf[...], wd_ref[:, cols])

    def tail_end():
        if is_last:
            y_ref[...] = _rms(y_ref[...], fn_ref[...])

    def tail_all():
        tail_begin()
        for k in range(TAIL_PARTS):
            tail_part(k)
        tail_end()

    front = functools.partial(
        _tile_front, S, T, is_sample, x_ref, invcnt_ref, n1_ref, win_ref, pw_ref, pscale_ref, vng_ref, sp_ref,
        bfull_ref, wout_ref, n2_ref, wg_ref, wu_ref, cw_ref, cb_ref, pool_out_ref, conv_out_ref,
        pbuf, st8, st4, st2, gbuf, mixbuf, h2buf, actbuf, x1buf)

    if is_sample:
        pbuf[:, 0:POOL_PAD - POOL_BUF, :] = jnp.zeros((S, POOL_PAD - POOL_BUF, POOL_WIDTH), jnp.float32)
        for r in range(POOL_BUF):
            pbuf[:, POOL_PAD - POOL_BUF + r, :] = spool_ref[r]
        front(sconv_ref, v_out_ref, None, lambda k: None)
        tail_all()
        return

    s = pl.program_id(0)
    gcarry = refs[9]
    j = lax.rem(s, seq_steps)

    @pl.when(s == 0)
    def _():
        x1buf[...] = jnp.zeros(x1buf.shape, x1buf.dtype)
        actbuf[...] = jnp.zeros(actbuf.shape, actbuf.dtype)

    @pl.when(j == 0)
    def _():
        pbuf[:, 0:POOL_PAD, :] = jnp.zeros((S, POOL_PAD, POOL_WIDTH), jnp.float32)
        gcarry[...] = jnp.zeros((CONV_PAD, D_FF), jnp.float32)

    @pl.when(j > 0)
    def _():
        pbuf[:, 0:POOL_PAD, :] = pbuf[:, T:T + POOL_PAD, :]

    @pl.when(s < n_tiles)
    def _():
        tail_begin()
        front(None, None, gcarry, tail_part)
        tail_end()

    @pl.when(s == n_tiles)
    def _():
        tail_all()


def _tile_front(S, T, is_sample, x_ref, invcnt_ref, n1_ref, win_ref, pw_ref, pscale_ref, vng_ref, sp_ref,
                bfull_ref, wout_ref, n2_ref, wg_ref, wu_ref, cw_ref, cb_ref, pool_out_ref, conv_out_ref,
                pbuf, st8, st4, st2, gbuf, mixbuf, h2buf, actbuf, x1buf, sconv_ref, v_out_ref, gcarry, fill):
    TM = S * T
    R = POOL_PAD + T
    x = x_ref[...]
    fill(0)
    h = _rms(x, n1_ref[...]).astype(jnp.bfloat16)

    p = _bdot(h, win_ref[:, 0:POOL_WIDTH])
    u = _gelu(_bdot(h, win_ref[:, POOL_WIDTH:POOL_WIDTH + SG_WIDTH]))
    v = _rms(_gelu(_bdot(h, win_ref[:, POOL_WIDTH + SG_WIDTH:IN_WIDTH])), vng_ref[...])
    fill(1)
    if is_sample:
        v_out_ref[...] = v
    pbuf[:, POOL_PAD:R, :] = p.reshape(S, T, POOL_WIDTH)
    pool_out_ref[...] = pbuf[:, R - POOL_BUF:R, :]

    zeros8 = jnp.zeros((S, 8, POOL_WIDTH), jnp.float32)
    st8[:, 0:8, :] = zeros8
    st4[:, 0:8, :] = zeros8
    st2[:, 0:8, :] = zeros8
    G = POOL_GROUP_DIM
    st8[:, 8:R, 3 * G:4 * G] = pbuf[:, 8:R, 3 * G:4 * G] + pbuf[:, 0:R - 8, 3 * G:4 * G]
    st4[:, 8:R, 2 * G:3 * G] = pbuf[:, 8:R, 2 * G:3 * G] + pbuf[:, 4:R - 4, 2 * G:3 * G]
    st4[:, 8:R, 3 * G:4 * G] = st8[:, 8:R, 3 * G:4 * G] + st8[:, 4:R - 4, 3 * G:4 * G]
    st2[:, 8:R, 1 * G:2 * G] = pbuf[:, 8:R, 1 * G:2 * G] + pbuf[:, 6:R - 2, 1 * G:2 * G]
    st2[:, 8:R, 2 * G:4 * G] = st4[:, 8:R, 2 * G:4 * G] + st4[:, 6:R - 2, 2 * G:4 * G]
    sum0 = pbuf[:, POOL_PAD:R, 0:G] + pbuf[:, POOL_PAD - 1:R - 1, 0:G]
    sum123 = st2[:, POOL_PAD:R, G:4 * G] + st2[:, POOL_PAD - 1:R - 1, G:4 * G]
    wsum = jnp.concatenate([sum0, sum123], axis=-1).reshape(TM, POOL_WIDTH)
    dpool = wsum * invcnt_ref[...] - p
    d16 = dpool.astype(jnp.bfloat16)
    half = POOL_WIDTH // 2
    for k in range(2):
        ks = slice(k * half, (k + 1) * half)
        a = _bdot(d16[:, ks], pw_ref[k]) * pscale_ref[:, ks]
        mixbuf[:, ks] = a.astype(jnp.bfloat16)

    fill(2)

    v16 = v.astype(jnp.bfloat16)
    for hd in range(SG_HEADS):
        sl = slice(hd * SG_HEAD_DIM, (hd + 1) * SG_HEAD_DIM)
        osl = slice(POOL_WIDTH + hd * SG_HEAD_DIM, POOL_WIDTH + (hd + 1) * SG_HEAD_DIM)
        if is_sample:
            for r in range(TM // SAMPLE_SP_ROWS):
                rs = slice(r * SAMPLE_SP_ROWS, (r + 1) * SAMPLE_SP_ROWS)
                mixed = _bdot(sp_ref[hd], v16[rs, sl]) + bfull_ref[:, sl]
                mixbuf[rs, osl] = (u[rs, sl] * mixed).astype(jnp.bfloat16)
        else:
            chunks = [slice(c * CHUNK, (c + 1) * CHUNK) for c in range(TM // CHUNK)]
            mixed_all = _bdot(sp_ref[hd], jnp.concatenate([v16[rs, sl] for rs in chunks], axis=1))
            for c, rs in enumerate(chunks):
                mixed = mixed_all[:, c * SG_HEAD_DIM:(c + 1) * SG_HEAD_DIM] + bfull_ref[:, sl]
                mixbuf[rs, osl] = (u[rs, sl] * mixed).astype(jnp.bfloat16)

    x1 = x + _bdot(mixbuf[...], wout_ref[...])
    x1buf[...] = x1
    fill(3)
    h2buf[...] = _rms(x1, n2_ref[...]).astype(jnp.bfloat16)

    for c in range(D_FF // FF_CHUNK):
        cs = slice(c * FF_CHUNK, (c + 1) * FF_CHUNK)
        g = _bdot(h2buf[...], wg_ref[:, cs])
        up = _bdot(h2buf[...], wu_ref[:, cs])
        g3 = g.reshape(S, T, FF_CHUNK)
        if is_sample:
            gbuf[:, CONV_PAD - CONV_BUF:CONV_PAD, :] = sconv_ref[:, :, cs]
        else:
            gbuf[0, 0:CONV_PAD, :] = gcarry[:, cs]
            gcarry[:, cs] = g3[0, T - CONV_PAD:T, :]
        gbuf[:, CONV_PAD:CONV_PAD + T, :] = g3
        conv_out_ref[:, :, cs] = g3[:, T - CONV_BUF:T, :]
        conv = (cb_ref[:, cs]
                + cw_ref[0:1, cs] * gbuf[:, CONV_PAD - 2:CONV_PAD - 2 + T, :]
                + cw_ref[1:2, cs] * gbuf[:, CONV_PAD - 1:CONV_PAD - 1 + T, :]
                + cw_ref[2:3, cs] * g3)
        act = _gelu(conv).reshape(TM, FF_CHUNK) * up
        actbuf[:, cs] = act.astype(jnp.bfloat16)


def _const_spec(shape):
    nd = len(shape)
    return pl.BlockSpec(shape, lambda *_: (0,) * nd)


def _layer_spec(shape, layer):
    nd = len(shape)
    return pl.BlockSpec((None,) + tuple(shape), lambda *_: (layer,) + (0,) * nd)


DOWN_CAST_ROWS = 176


def _run_layer(layer, depth, is_sample, is_last, x, state, invcnt, small, mats, spatial, next_f32, stacked,
               zero_fill=()):
    (n1, pscale, vng, n2, cw, cb, pw, fn) = small
    (win, wout, wg, wu, wd) = mats
    sp, bfull = spatial
    TM = TM_SAMPLE if is_sample else TM_PROMPT
    f32 = jnp.float32
    if is_sample:
        n_seq = state[1].shape[1]
        T = x.shape[0] // n_seq
        S = TM // T
        seq_steps = 1
        n_tiles = x.shape[0] // TM
        grid = (n_tiles,)
        cur = prev = lambda s: s
        in_specs = [pl.BlockSpec((TM, D_MODEL), lambda s: (cur(s), 0)),
                    pl.BlockSpec((None, POOL_BUF, S, POOL_WIDTH), lambda s: (layer, 0, cur(s), 0)),
                    pl.BlockSpec((None, S, CONV_BUF, D_FF), lambda s: (layer, cur(s), 0, 0)),
                    _const_spec((TM, POOL_WIDTH))]
        operands = [x, state[0], state[1], invcnt]
        out_shape = [jax.ShapeDtypeStruct((x.shape[0], D_MODEL), f32),
                     jax.ShapeDtypeStruct((depth, n_seq, POOL_BUF, POOL_WIDTH), f32),
                     jax.ShapeDtypeStruct((depth, n_seq, CONV_BUF, D_FF), f32),
                     jax.ShapeDtypeStruct((depth, x.shape[0], SG_WIDTH), f32)]
        out_specs = [pl.BlockSpec((TM, D_MODEL), lambda s: (prev(s), 0)),
                     pl.BlockSpec((None, S, POOL_BUF, POOL_WIDTH), lambda s: (layer, cur(s), 0, 0)),
                     pl.BlockSpec((None, S, CONV_BUF, D_FF), lambda s: (layer, cur(s), 0, 0)),
                     pl.BlockSpec((None, TM, SG_WIDTH), lambda s: (layer, cur(s), 0))]
        n_stacked = 3
    else:
        B, L, _ = x.shape
        S, T = 1, TM
        seq_steps = steps = L // TM
        n_tiles = B * steps
        grid = (n_tiles + 1,)
        cur = lambda s: jnp.minimum(s, n_tiles - 1)
        prev = lambda s: jnp.maximum(s - 1, 0)
        in_specs = [pl.BlockSpec((None, TM, D_MODEL), lambda s: (cur(s) // steps, cur(s) % steps, 0)),
                    pl.BlockSpec((None, TM, POOL_WIDTH), lambda s: (jnp.minimum(cur(s) % steps, 1), 0, 0))]
        operands = [x, invcnt]
        out_shape = [jax.ShapeDtypeStruct((B, L, D_MODEL), f32),
                     jax.ShapeDtypeStruct((depth, B, POOL_BUF, POOL_WIDTH), f32),
                     jax.ShapeDtypeStruct((depth, B, CONV_BUF, D_FF), f32)]
        out_specs = [pl.BlockSpec((None, TM, D_MODEL), lambda s: (prev(s) // steps, prev(s) % steps, 0)),
                     pl.BlockSpec((None, 1, POOL_BUF, POOL_WIDTH), lambda s: (layer, cur(s) // steps, 0, 0)),
                     pl.BlockSpec((None, 1, CONV_BUF, D_FF), lambda s: (layer, cur(s) // steps, 0, 0))]
        n_stacked = 2
    semantics = ("arbitrary",)

    in_specs += [
        _const_spec((depth, D_MODEL)),
        _const_spec((D_MODEL, IN_WIDTH)),
        _layer_spec((2, POOL_WIDTH // 2, POOL_WIDTH // 2), layer),
        _const_spec((depth, POOL_WIDTH)),
        _const_spec((depth, SG_WIDTH)),
        _layer_spec(sp.shape[1:], layer),
        _layer_spec(bfull.shape[1:], layer),
        _const_spec((D_MODEL, D_MODEL)),
        _const_spec((depth, D_MODEL)),
        _const_spec((D_MODEL, D_FF)),
        _const_spec((D_MODEL, D_FF)),
        _layer_spec((3, D_FF), layer),
        _const_spec((depth, D_FF)),
        _const_spec((D_FF, D_MODEL)),
        _const_spec((1, D_MODEL)),
    ]
    operands += [n1, win, pw, pscale, vng, sp, bfull, wout, n2, wg, wu, cw, cb, wd, fn]

    n_cast = 0
    if next_f32 is not None:
        assert not is_sample
        for w in next_f32:
            rows, share = ((D_MODEL // n_tiles, 1) if w.shape[1] == D_MODEL
                           else (DOWN_CAST_ROWS, n_tiles * DOWN_CAST_ROWS // D_FF))
            cols = w.shape[2]
            in_specs.append(pl.BlockSpec((None, rows, cols),
                                         lambda s, share=share: (layer + 1, cur(s) // share, 0)))
            out_specs.append(pl.BlockSpec((rows, cols), lambda s, share=share: (cur(s) // share, 0)))
            out_shape.append(jax.ShapeDtypeStruct(w.shape[1:], jnp.bfloat16))
            operands.append(w)
            n_cast += 1

    for shape in zero_fill:
        assert not is_sample and n_tiles % shape[0] == 0
        per_layer = n_tiles // shape[0]
        blk = (1, shape[1] // per_layer) + tuple(shape[2:])
        assert blk[1] * per_layer == shape[1]
        out_specs.append(pl.BlockSpec(
            blk, lambda s, per_layer=per_layer, nd=len(shape): (cur(s) // per_layer, cur(s) % per_layer) + (0,) * (nd - 2)))
        out_shape.append(jax.ShapeDtypeStruct(shape, f32))

    aliases = {}
    assert len(stacked) == n_stacked
    for k, arr in enumerate(stacked):
        aliases[len(operands)] = 1 + k
        in_specs.append(pl.BlockSpec(memory_space=pl.ANY))
        operands.append(arr)
    n_alias = n_stacked

    R = POOL_PAD + T
    scratch = [pltpu.VMEM((S, R, POOL_WIDTH), jnp.float32)] * 4
    scratch += [pltpu.VMEM((S, CONV_PAD + T, FF_CHUNK), jnp.float32),
                pltpu.VMEM((TM, D_MODEL), jnp.bfloat16),
                pltpu.VMEM((TM, D_MODEL), jnp.bfloat16),
                pltpu.VMEM((TM, D_FF), jnp.bfloat16),
                pltpu.VMEM((TM, D_MODEL), jnp.float32)]
    if not is_sample:
        scratch.append(pltpu.VMEM((CONV_PAD, D_FF), jnp.float32))

    return pl.pallas_call(
        functools.partial(_layer_body, layer, S, T, is_sample, is_last, n_cast, n_alias, len(zero_fill), seq_steps,
                          n_tiles),
        grid=grid,
        in_specs=in_specs,
        out_specs=out_specs,
        out_shape=out_shape,
        scratch_shapes=scratch,
        input_output_aliases=aliases,
        compiler_params=pltpu.CompilerParams(dimension_semantics=semantics,
                                             vmem_limit_bytes=VMEM_LIMIT_BYTES),
        name=("sample" if is_sample else "prompt") + f"_layer{layer}",
    )(*operands)


def _inv_counts(n_rows):
    win = np.repeat(np.asarray(POOL_WINDOWS, np.float32), POOL_GROUP_DIM)[None, :]
    pos = np.arange(n_rows, dtype=np.float32)[:, None]
    first = 1.0 / np.minimum(pos + 1.0, win)
    later = np.broadcast_to(1.0 / win, (n_rows, POOL_WIDTH))
    return jnp.asarray(np.stack([first, later]).astype(np.float32))


def _prep_body(t_dec, pool_w_ref, wsp_ref, bt_ref, pw_ref, spp_ref, sps_ref, bfp_ref, bfs_ref):
    bf = jnp.bfloat16
    G = POOL_GROUP_DIM
    for k in range(2):
        for a in range(2):
            for b in range(2):
                blk = pool_w_ref[2 * k + a].astype(bf) if a == b else jnp.zeros((G, G), bf)
                pw_ref[k, a * G:(a + 1) * G, b * G:(b + 1) * G] = blk

    n = SAMPLE_SP_ROWS
    pos_mask = t_dec - 1
    shift = t_dec.bit_length() - 1
    row = lax.broadcasted_iota(jnp.int32, (CHUNK, CHUNK), 0)
    col = lax.broadcasted_iota(jnp.int32, (CHUNK, CHUNK), 1)
    causal = row >= col
    expand = (lax.broadcasted_iota(jnp.int32, (n, CHUNK), 1)
              == (lax.broadcasted_iota(jnp.int32, (n, CHUNK), 0) & pos_mask)).astype(bf)
    expand_t = (lax.broadcasted_iota(jnp.int32, (CHUNK, n), 0)
                == (lax.broadcasted_iota(jnp.int32, (CHUNK, n), 1) & pos_mask)).astype(bf)
    r2 = lax.broadcasted_iota(jnp.int32, (n, n), 0)
    c2 = lax.broadcasted_iota(jnp.int32, (n, n), 1)
    keep = jnp.logical_and((r2 >> shift) == (c2 >> shift), (r2 & pos_mask) >= (c2 & pos_mask))
    for h in range(SG_HEADS):
        w = wsp_ref[h]
        spp_ref[h] = jnp.where(causal, w, 0.0).astype(bf)
        rows = _bdot(expand, w.astype(bf))
        tiled = _bdot(rows.astype(bf), expand_t)
        sps_ref[h] = jnp.where(keep, tiled, 0.0).astype(bf)
        bfp_ref[:, h * SG_HEAD_DIM:(h + 1) * SG_HEAD_DIM] = jnp.broadcast_to(
            bt_ref[:, h:h + 1], (CHUNK, SG_HEAD_DIM))
    first = bfp_ref[0:t_dec, :]
    bfs_ref[...] = jnp.broadcast_to(first[None], (n // t_dec, t_dec, SG_WIDTH)).reshape(n, SG_WIDTH)


def _prep_operands(pool_w, w_spatial, b_spatial, t_dec):
    depth = pool_w.shape[0]
    assert t_dec & (t_dec - 1) == 0 and t_dec % 8 == 0 and SAMPLE_SP_ROWS % t_dec == 0
    bf, f32 = jnp.bfloat16, jnp.float32
    n = SAMPLE_SP_ROWS
    per_layer = lambda *shape: pl.BlockSpec((None,) + shape, lambda l: (l,) + (0,) * len(shape))
    return pl.pallas_call(
        functools.partial(_prep_body, t_dec),
        grid=(depth,),
        in_specs=[per_layer(len(POOL_WINDOWS), POOL_GROUP_DIM, POOL_GROUP_DIM),
                  per_layer(SG_HEADS, CHUNK, CHUNK),
                  per_layer(CHUNK, SG_HEADS)],
        out_specs=[per_layer(2, POOL_WIDTH // 2, POOL_WIDTH // 2), per_layer(SG_HEADS, CHUNK, CHUNK),
                   per_layer(SG_HEADS, n, n), per_layer(CHUNK, SG_WIDTH), per_layer(n, SG_WIDTH)],
        out_shape=[jax.ShapeDtypeStruct((depth, 2, POOL_WIDTH // 2, POOL_WIDTH // 2), bf),
                   jax.ShapeDtypeStruct((depth, SG_HEADS, CHUNK, CHUNK), bf),
                   jax.ShapeDtypeStruct((depth, SG_HEADS, n, n), bf),
                   jax.ShapeDtypeStruct((depth, CHUNK, SG_WIDTH), f32),
                   jax.ShapeDtypeStruct((depth, n, SG_WIDTH), f32)],
        compiler_params=pltpu.CompilerParams(dimension_semantics=("arbitrary",)),
        name="prep_operands",
    )(pool_w, w_spatial, jnp.swapaxes(b_spatial, 1, 2))


def kernel(x_prompt, x_sample, state_pool, state_conv, norm1_g, w_in, pool_w, pool_scale, v_norm_g, w_spatial, b_spatial, w_out, norm2_g, w_gate, w_up, conv_w, conv_b, w_down, final_norm_g):
    depth = w_in.shape[0]
    bf = jnp.bfloat16
    pw, sp_p, sp_s, bias_p, bias_s = _prep_operands(pool_w, w_spatial, b_spatial, x_sample.shape[1])
    small = (norm1_g, pool_scale, v_norm_g, norm2_g, conv_w, conv_b, pw, final_norm_g[None, :])
    big_f32 = (w_in, w_out, w_gate, w_up, w_down)
    spatial_p = (sp_p, bias_p)
    spatial_s = (sp_s, bias_s)
    invcnt = _inv_counts(TM_PROMPT)

    n_dec, t_dec, _ = x_sample.shape
    xp, xs = x_prompt, x_sample.reshape(n_dec * t_dec, D_MODEL)
    mats = tuple(w[0].astype(bf) for w in big_f32)
    state_pool_t = jnp.transpose(state_pool, (0, 2, 1, 3))
    n_prompt = x_prompt.shape[0]
    f32 = jnp.float32
    stacked_p = (jnp.zeros((depth, n_prompt, POOL_BUF, POOL_WIDTH), f32),
                 jnp.zeros((depth, n_prompt, CONV_BUF, D_FF), f32))
    sample_stacked_shapes = ((depth, n_dec, POOL_BUF, POOL_WIDTH), (depth, n_dec, CONV_BUF, D_FF),
                             (depth, n_dec * t_dec, SG_WIDTH))
    stacked_s = None
    for i in range(depth):
        last = i == depth - 1
        res = _run_layer(i, depth, False, last, xp, None, invcnt, small, mats, spatial_p,
                         None if last else big_f32, stacked_p,
                         zero_fill=sample_stacked_shapes if i == 0 else ())
        xp, stacked_p = res[0], res[1:3]
        if i == 0:
            stacked_s = tuple(res[len(res) - len(sample_stacked_shapes):])
        next_mats = tuple(res[3:3 + len(big_f32)])
        res_s = _run_layer(i, depth, True, last, xs, (state_pool_t, state_conv), invcnt[1, :TM_SAMPLE], small,
                           mats, spatial_s, None, stacked_s)
        xs, stacked_s = res_s[0], res_s[1:4]
        mats = next_mats
    pool_p, conv_p = stacked_p
    pool_s, conv_s, v_s = stacked_s
    return (xp, xs.reshape(n_dec, t_dec, D_MODEL), pool_p, pool_s, conv_p, conv_s,
            v_s.reshape(depth, n_dec, t_dec, SG_WIDTH))
```

```python
import functools

import numpy as np
import jax
import jax.numpy as jnp
from jax import lax
from jax.experimental import pallas as pl
from jax.experimental.pallas import tpu as pltpu

D_MODEL = 1024
POOL_WIDTH = 512
POOL_WINDOWS = (2, 4, 8, 16)
POOL_GROUP_DIM = 128
POOL_BUF = 15
POOL_PAD = 16
SG_WIDTH = 512
SG_HEADS = 4
SG_HEAD_DIM = 128
CHUNK = 128
IN_WIDTH = POOL_WIDTH + 2 * SG_WIDTH
D_FF = 2816
CONV_BUF = 2
CONV_PAD = 8
EPS = 1e-6

TM_PROMPT = 512
TM_SAMPLE = 256
SAMPLE_SP_ROWS = 256
FF_CHUNK = 256
TAIL_PARTS = 4
TAIL_COLS = D_MODEL // TAIL_PARTS
VMEM_LIMIT_BYTES = 58 * 1024 * 1024

_INV_SQRT2 = 0.7071067811865476


def _gelu(x):
    return 0.5 * x * (1.0 + lax.erf(x * _INV_SQRT2))


def _rms(x, g):
    ms = jnp.mean(x * x, axis=-1, keepdims=True)
    return x * lax.rsqrt(ms + EPS) * g


def _bdot(a, b):
    return jnp.dot(a, b, preferred_element_type=jnp.float32)


def _layer_body(layer, S, T, is_sample, is_last, n_cast, n_alias, n_zero, seq_steps, n_tiles, *refs):
    refs = list(refs)
    x_ref = refs.pop(0)
    if is_sample:
        spool_ref = refs.pop(0)
        sconv_ref = refs.pop(0)
    (invcnt_ref, n1_ref, win_ref, pw_ref, pscale_ref, vng_ref, sp_ref, bfull_ref, wout_ref, n2_ref,
     wg_ref, wu_ref, cw_ref, cb_ref, wd_ref, fn_ref) = refs[:16]
    refs = refs[16:]
    n1_ref, pscale_ref, vng_ref, n2_ref, cb_ref = (
        r.at[pl.ds(layer, 1)] for r in (n1_ref, pscale_ref, vng_ref, n2_ref, cb_ref))
    cast_in, refs = refs[:n_cast], refs[n_cast:]
    refs = refs[n_alias:]
    y_ref, pool_out_ref, conv_out_ref = refs[:3]
    refs = refs[3:]
    if is_sample:
        v_out_ref = refs.pop(0)
    cast_out, refs = refs[:n_cast], refs[n_cast:]
    zero_out, refs = refs[:n_zero], refs[n_zero:]

    for dst in zero_out:
        dst[...] = jnp.zeros(dst.shape, dst.dtype)
    pbuf, st8, st4, st2, gbuf, mixbuf, h2buf, actbuf, x1buf = refs[:9]

    if is_sample:
        wg_v, wu_v, wd_v, wsem = refs[9:13]
        ffn_weight_copies = [pltpu.make_async_copy(src, dst, wsem.at[i]) for i, (src, dst) in enumerate(
            ((wg_ref, wg_v), (wu_ref, wu_v), (wd_ref, wd_v)))]
        wg_ref, wu_ref, wd_ref = wg_v, wu_v, wd_v
        first_step = pl.program_id(0) == 0

        @pl.when(first_step)
        def _():
            for cp in ffn_weight_copies:
                cp.start()

    for src, dst in zip(cast_in, cast_out):
        dst[...] = src[...].astype(jnp.bfloat16)

    def tail_begin():
        y_ref[...] = x1buf[...]

    def tail_part(k):
        cols = slice(k * TAIL_COLS, (k + 1) * TAIL_COLS)
        y_ref[:, cols] += _bdot(actbuf[...], wd_ref[:, cols])

    def tail_end():
        if is_last:
            y_ref[...] = _rms(y_ref[...], fn_ref[...])

    def tail_all():
        tail_begin()
        for k in range(TAIL_PARTS):
            tail_part(k)
        tail_end()

    front = functools.partial(
        _tile_front, S, T, is_sample, x_ref, invcnt_ref, n1_ref, win_ref, pw_ref, pscale_ref, vng_ref, sp_ref,
        bfull_ref, wout_ref, n2_ref, wg_ref, wu_ref, cw_ref, cb_ref, pool_out_ref, conv_out_ref,
        pbuf, st8, st4, st2, gbuf, mixbuf, h2buf, actbuf, x1buf)

    if is_sample:
        pbuf[:, 0:POOL_PAD - POOL_BUF, :] = jnp.zeros((S, POOL_PAD - POOL_BUF, POOL_WIDTH), jnp.float32)
        for r in range(POOL_BUF):
            pbuf[:, POOL_PAD - POOL_BUF + r, :] = spool_ref[r]
        def wait_ffn_weights(k):
            if k == TAIL_PARTS - 1:
                ffn_weight_copies[0].wait()
                ffn_weight_copies[1].wait()

        @pl.when(first_step)
        def _():
            front(sconv_ref, v_out_ref, None, wait_ffn_weights)
            ffn_weight_copies[2].wait()
            tail_all()

        @pl.when(jnp.logical_not(first_step))
        def _():
            front(sconv_ref, v_out_ref, None, lambda k: None)
            tail_all()

        return

    s = pl.program_id(0)
    gcarry = refs[9]
    j = lax.rem(s, seq_steps)

    @pl.when(s == 0)
    def _():
        x1buf[...] = jnp.zeros(x1buf.shape, x1buf.dtype)
        actbuf[...] = jnp.zeros(actbuf.shape, actbuf.dtype)

    @pl.when(j == 0)
    def _():
        pbuf[:, 0:POOL_PAD, :] = jnp.zeros((S, POOL_PAD, POOL_WIDTH), jnp.float32)
        gcarry[...] = jnp.zeros((CONV_PAD, D_FF), jnp.float32)

    @pl.when(j > 0)
    def _():
        pbuf[:, 0:POOL_PAD, :] = pbuf[:, T:T + POOL_PAD, :]

    @pl.when(s < n_tiles)
    def _():
        tail_begin()
        front(None, None, gcarry, tail_part)
        tail_end()

    @pl.when(s == n_tiles)
    def _():
        tail_all()


def _tile_front(S, T, is_sample, x_ref, invcnt_ref, n1_ref, win_ref, pw_ref, pscale_ref, vng_ref, sp_ref,
                bfull_ref, wout_ref, n2_ref, wg_ref, wu_ref, cw_ref, cb_ref, pool_out_ref, conv_out_ref,
                pbuf, st8, st4, st2, gbuf, mixbuf, h2buf, actbuf, x1buf, sconv_ref, v_out_ref, gcarry, fill):
    TM = S * T
    R = POOL_PAD + T
    x = x_ref[...]
    fill(0)
    h = _rms(x, n1_ref[...]).astype(jnp.bfloat16)

    p = _bdot(h, win_ref[:, 0:POOL_WIDTH])
    u = _gelu(_bdot(h, win_ref[:, POOL_WIDTH:POOL_WIDTH + SG_WIDTH]))
    v = _rms(_gelu(_bdot(h, win_ref[:, POOL_WIDTH + SG_WIDTH:IN_WIDTH])), vng_ref[...])
    fill(1)
    if is_sample:
        v_out_ref[...] = v
    pbuf[:, POOL_PAD:R, :] = p.reshape(S, T, POOL_WIDTH)
    pool_out_ref[...] = pbuf[:, R - POOL_BUF:R, :]

    zeros8 = jnp.zeros((S, 8, POOL_WIDTH), jnp.float32)
    st8[:, 0:8, :] = zeros8
    st4[:, 0:8, :] = zeros8
    st2[:, 0:8, :] = zeros8
    G = POOL_GROUP_DIM
    st8[:, 8:R, 3 * G:4 * G] = pbuf[:, 8:R, 3 * G:4 * G] + pbuf[:, 0:R - 8, 3 * G:4 * G]
    st4[:, 8:R, 2 * G:3 * G] = pbuf[:, 8:R, 2 * G:3 * G] + pbuf[:, 4:R - 4, 2 * G:3 * G]
    st4[:, 8:R, 3 * G:4 * G] = st8[:, 8:R, 3 * G:4 * G] + st8[:, 4:R - 4, 3 * G:4 * G]
    st2[:, 8:R, 1 * G:2 * G] = pbuf[:, 8:R, 1 * G:2 * G] + pbuf[:, 6:R - 2, 1 * G:2 * G]
    st2[:, 8:R, 2 * G:4 * G] = st4[:, 8:R, 2 * G:4 * G] + st4[:, 6:R - 2, 2 * G:4 * G]
    sum0 = pbuf[:, POOL_PAD:R, 0:G] + pbuf[:, POOL_PAD - 1:R - 1, 0:G]
    sum123 = st2[:, POOL_PAD:R, G:4 * G] + st2[:, POOL_PAD - 1:R - 1, G:4 * G]
    wsum = jnp.concatenate([sum0, sum123], axis=-1).reshape(TM, POOL_WIDTH)
    dpool = wsum * invcnt_ref[...] - p
    d16 = dpool.astype(jnp.bfloat16)
    half = POOL_WIDTH // 2
    for k in range(2):
        ks = slice(k * half, (k + 1) * half)
        a = _bdot(d16[:, ks], pw_ref[k]) * pscale_ref[:, ks]
        mixbuf[:, ks] = a.astype(jnp.bfloat16)

    fill(2)

    v16 = v.astype(jnp.bfloat16)
    for hd in range(SG_HEADS):
        sl = slice(hd * SG_HEAD_DIM, (hd + 1) * SG_HEAD_DIM)
        osl = slice(POOL_WIDTH + hd * SG_HEAD_DIM, POOL_WIDTH + (hd + 1) * SG_HEAD_DIM)
        if is_sample:
            for r in range(TM // SAMPLE_SP_ROWS):
                rs = slice(r * SAMPLE_SP_ROWS, (r + 1) * SAMPLE_SP_ROWS)
                mixed = _bdot(sp_ref[hd], v16[rs, sl]) + bfull_ref[:, sl]
                mixbuf[rs, osl] = (u[rs, sl] * mixed).astype(jnp.bfloat16)
        else:
            chunks = [slice(c * CHUNK, (c + 1) * CHUNK) for c in range(TM // CHUNK)]
            mixed_all = _bdot(sp_ref[hd], jnp.concatenate([v16[rs, sl] for rs in chunks], axis=1))
            for c, rs in enumerate(chunks):
                mixed = mixed_all[:, c * SG_HEAD_DIM:(c + 1) * SG_HEAD_DIM] + bfull_ref[:, sl]
                mixbuf[rs, osl] = (u[rs, sl] * mixed).astype(jnp.bfloat16)

    x1 = x + _bdot(mixbuf[...], wout_ref[...])
    x1buf[...] = x1
    fill(3)
    h2buf[...] = _rms(x1, n2_ref[...]).astype(jnp.bfloat16)

    for c in range(D_FF // FF_CHUNK):
        cs = slice(c * FF_CHUNK, (c + 1) * FF_CHUNK)
        g = _bdot(h2buf[...], wg_ref[:, cs])
        up = _bdot(h2buf[...], wu_ref[:, cs])
        g3 = g.reshape(S, T, FF_CHUNK)
        if is_sample:
            gbuf[:, CONV_PAD - CONV_BUF:CONV_PAD, :] = sconv_ref[:, :, cs]
        else:
            gbuf[0, 0:CONV_PAD, :] = gcarry[:, cs]
            gcarry[:, cs] = g3[0, T - CONV_PAD:T, :]
        gbuf[:, CONV_PAD:CONV_PAD + T, :] = g3
        conv_out_ref[:, :, cs] = g3[:, T - CONV_BUF:T, :]
        conv = (cb_ref[:, cs]
                + cw_ref[0:1, cs] * gbuf[:, CONV_PAD - 2:CONV_PAD - 2 + T, :]
                + cw_ref[1:2, cs] * gbuf[:, CONV_PAD - 1:CONV_PAD - 1 + T, :]
                + cw_ref[2:3, cs] * g3)
        act = _gelu(conv).reshape(TM, FF_CHUNK) * up
        actbuf[:, cs] = act.astype(jnp.bfloat16)


def _const_spec(shape):
    nd = len(shape)
    return pl.BlockSpec(shape, lambda *_: (0,) * nd)


def _layer_spec(shape, layer):
    nd = len(shape)
    return pl.BlockSpec((None,) + tuple(shape), lambda *_: (layer,) + (0,) * nd)


DOWN_CAST_ROWS = 176


def _run_layer(layer, depth, is_sample, is_last, x, state, invcnt, small, mats, spatial, next_f32, stacked,
               zero_fill=()):
    (n1, pscale, vng, n2, cw, cb, pw, fn) = small
    (win, wout, wg, wu, wd) = mats
    sp, bfull = spatial
    TM = TM_SAMPLE if is_sample else TM_PROMPT
    f32 = jnp.float32
    if is_sample:
        n_seq = state[1].shape[1]
        T = x.shape[0] // n_seq
        S = TM // T
        seq_steps = 1
        n_tiles = x.shape[0] // TM
        grid = (n_tiles,)
        cur = prev = lambda s: s
        in_specs = [pl.BlockSpec((TM, D_MODEL), lambda s: (cur(s), 0)),
                    pl.BlockSpec((None, POOL_BUF, S, POOL_WIDTH), lambda s: (layer, 0, cur(s), 0)),
                    pl.BlockSpec((None, S, CONV_BUF, D_FF), lambda s: (layer, cur(s), 0, 0)),
                    _const_spec((TM, POOL_WIDTH))]
        operands = [x, state[0], state[1], invcnt]
        out_shape = [jax.ShapeDtypeStruct((x.shape[0], D_MODEL), f32),
                     jax.ShapeDtypeStruct((depth, n_seq, POOL_BUF, POOL_WIDTH), f32),
                     jax.ShapeDtypeStruct((depth, n_seq, CONV_BUF, D_FF), f32),
                     jax.ShapeDtypeStruct((depth, x.shape[0], SG_WIDTH), f32)]
        out_specs = [pl.BlockSpec((TM, D_MODEL), lambda s: (prev(s), 0)),
                     pl.BlockSpec((None, S, POOL_BUF, POOL_WIDTH), lambda s: (layer, cur(s), 0, 0)),
                     pl.BlockSpec((None, S, CONV_BUF, D_FF), lambda s: (layer, cur(s), 0, 0)),
                     pl.BlockSpec((None, TM, SG_WIDTH), lambda s: (layer, cur(s), 0))]
        n_stacked = 3
    else:
        B, L, _ = x.shape
        S, T = 1, TM
        seq_steps = steps = L // TM
        n_tiles = B * steps
        grid = (n_tiles + 1,)
        cur = lambda s: jnp.minimum(s, n_tiles - 1)
        prev = lambda s: jnp.maximum(s - 1, 0)
        in_specs = [pl.BlockSpec((None, TM, D_MODEL), lambda s: (cur(s) // steps, cur(s) % steps, 0)),
                    pl.BlockSpec((None, TM, POOL_WIDTH), lambda s: (jnp.minimum(cur(s) % steps, 1), 0, 0))]
        operands = [x, invcnt]
        out_shape = [jax.ShapeDtypeStruct((B, L, D_MODEL), f32),
                     jax.ShapeDtypeStruct((depth, B, POOL_BUF, POOL_WIDTH), f32),
                     jax.ShapeDtypeStruct((depth, B, CONV_BUF, D_FF), f32)]
        out_specs = [pl.BlockSpec((None, TM, D_MODEL), lambda s: (prev(s) // steps, prev(s) % steps, 0)),
                     pl.BlockSpec((None, 1, POOL_BUF, POOL_WIDTH), lambda s: (layer, cur(s) // steps, 0, 0)),
                     pl.BlockSpec((None, 1, CONV_BUF, D_FF), lambda s: (layer, cur(s) // steps, 0, 0))]
        n_stacked = 2
    semantics = ("arbitrary",)

    def ffn_weight_spec(shape):
        return pl.BlockSpec(memory_space=pl.ANY) if is_sample else _const_spec(shape)

    in_specs += [
        _const_spec((depth, D_MODEL)),
        _const_spec((D_MODEL, IN_WIDTH)),
        _layer_spec((2, POOL_WIDTH // 2, POOL_WIDTH // 2), layer),
        _const_spec((depth, POOL_WIDTH)),
        _const_spec((depth, SG_WIDTH)),
        _layer_spec(sp.shape[1:], layer),
        _layer_spec(bfull.shape[1:], layer),
        _const_spec((D_MODEL, D_MODEL)),
        _const_spec((depth, D_MODEL)),
        ffn_weight_spec((D_MODEL, D_FF)),
        ffn_weight_spec((D_MODEL, D_FF)),
        _layer_spec((3, D_FF), layer),
        _const_spec((depth, D_FF)),
        ffn_weight_spec((D_FF, D_MODEL)),
        _const_spec((1, D_MODEL)),
    ]
    operands += [n1, win, pw, pscale, vng, sp, bfull, wout, n2, wg, wu, cw, cb, wd, fn]

    n_cast = 0
    if next_f32 is not None:
        assert not is_sample
        for w in next_f32:
            rows, share = ((D_MODEL // n_tiles, 1) if w.shape[1] == D_MODEL
                           else (DOWN_CAST_ROWS, n_tiles * DOWN_CAST_ROWS // D_FF))
            cols = w.shape[2]
            in_specs.append(pl.BlockSpec((None, rows, cols),
                                         lambda s, share=share: (layer + 1, cur(s) // share, 0)))
            out_specs.append(pl.BlockSpec((rows, cols), lambda s, share=share: (cur(s) // share, 0)))
            out_shape.append(jax.ShapeDtypeStruct(w.shape[1:], jnp.bfloat16))
            operands.append(w)
            n_cast += 1

    for shape in zero_fill:
        assert not is_sample and n_tiles % shape[0] == 0
        per_layer = n_tiles // shape[0]
        blk = (1, shape[1] // per_layer) + tuple(shape[2:])
        assert blk[1] * per_layer == shape[1]
        out_specs.append(pl.BlockSpec(
            blk, lambda s, per_layer=per_layer, nd=len(shape): (cur(s) // per_layer, cur(s) % per_layer) + (0,) * (nd - 2)))
        out_shape.append(jax.ShapeDtypeStruct(shape, f32))

    aliases = {}
    assert len(stacked) == n_stacked
    for k, arr in enumerate(stacked):
        aliases[len(operands)] = 1 + k
        in_specs.append(pl.BlockSpec(memory_space=pl.ANY))
        operands.append(arr)
    n_alias = n_stacked

    R = POOL_PAD + T
    scratch = [pltpu.VMEM((S, R, POOL_WIDTH), jnp.float32)] * 4
    scratch += [pltpu.VMEM((S, CONV_PAD + T, FF_CHUNK), jnp.float32),
                pltpu.VMEM((TM, D_MODEL), jnp.bfloat16),
                pltpu.VMEM((TM, D_MODEL), jnp.bfloat16),
                pltpu.VMEM((TM, D_FF), jnp.bfloat16),
                pltpu.VMEM((TM, D_MODEL), jnp.float32)]
    if is_sample:
        scratch += [pltpu.VMEM((D_MODEL, D_FF), jnp.bfloat16),
                    pltpu.VMEM((D_MODEL, D_FF), jnp.bfloat16),
                    pltpu.VMEM((D_FF, D_MODEL), jnp.bfloat16),
                    pltpu.SemaphoreType.DMA((3,))]
    else:
        scratch.append(pltpu.VMEM((CONV_PAD, D_FF), jnp.float32))

    return pl.pallas_call(
        functools.partial(_layer_body, layer, S, T, is_sample, is_last, n_cast, n_alias, len(zero_fill), seq_steps,
                          n_tiles),
        grid=grid,
        in_specs=in_specs,
        out_specs=out_specs,
        out_shape=out_shape,
        scratch_shapes=scratch,
        input_output_aliases=aliases,
        compiler_params=pltpu.CompilerParams(dimension_semantics=semantics,
                                             vmem_limit_bytes=VMEM_LIMIT_BYTES),
        name=("sample" if is_sample else "prompt") + f"_layer{layer}",
    )(*operands)


def _inv_counts(n_rows):
    win = np.repeat(np.asarray(POOL_WINDOWS, np.float32), POOL_GROUP_DIM)[None, :]
    pos = np.arange(n_rows, dtype=np.float32)[:, None]
    first = 1.0 / np.minimum(pos + 1.0, win)
    later = np.broadcast_to(1.0 / win, (n_rows, POOL_WIDTH))
    return jnp.asarray(np.stack([first, later]).astype(np.float32))


def _prep_body(t_dec, pool_w_ref, wsp_ref, bt_ref, pw_ref, spp_ref, sps_ref, bfp_ref, bfs_ref):
    bf = jnp.bfloat16
    G = POOL_GROUP_DIM
    for k in range(2):
        for a in range(2):
            for b in range(2):
                blk = pool_w_ref[2 * k + a].astype(bf) if a == b else jnp.zeros((G, G), bf)
                pw_ref[k, a * G:(a + 1) * G, b * G:(b + 1) * G] = blk

    n = SAMPLE_SP_ROWS
    pos_mask = t_dec - 1
    shift = t_dec.bit_length() - 1
    row = lax.broadcasted_iota(jnp.int32, (CHUNK, CHUNK), 0)
    col = lax.broadcasted_iota(jnp.int32, (CHUNK, CHUNK), 1)
    causal = row >= col
    expand = (lax.broadcasted_iota(jnp.int32, (n, CHUNK), 1)
              == (lax.broadcasted_iota(jnp.int32, (n, CHUNK), 0) & pos_mask)).astype(bf)
    expand_t = (lax.broadcasted_iota(jnp.int32, (CHUNK, n), 0)
                == (lax.broadcasted_iota(jnp.int32, (CHUNK, n), 1) & pos_mask)).astype(bf)
    r2 = lax.broadcasted_iota(jnp.int32, (n, n), 0)
    c2 = lax.broadcasted_iota(jnp.int32, (n, n), 1)
    keep = jnp.logical_and((r2 >> shift) == (c2 >> shift), (r2 & pos_mask) >= (c2 & pos_mask))
    for h in range(SG_HEADS):
        w = wsp_ref[h]
        spp_ref[h] = jnp.where(causal, w, 0.0).astype(bf)
        rows = _bdot(expand, w.astype(bf))
        tiled = _bdot(rows.astype(bf), expand_t)
        sps_ref[h] = jnp.where(keep, tiled, 0.0).astype(bf)
        bfp_ref[:, h * SG_HEAD_DIM:(h + 1) * SG_HEAD_DIM] = jnp.broadcast_to(
            bt_ref[:, h:h + 1], (CHUNK, SG_HEAD_DIM))
    first = bfp_ref[0:t_dec, :]
    bfs_ref[...] = jnp.broadcast_to(first[None], (n // t_dec, t_dec, SG_WIDTH)).reshape(n, SG_WIDTH)


def _prep_operands(pool_w, w_spatial, b_spatial, t_dec):
    depth = pool_w.shape[0]
    assert t_dec & (t_dec - 1) == 0 and t_dec % 8 == 0 and SAMPLE_SP_ROWS % t_dec == 0
    bf, f32 = jnp.bfloat16, jnp.float32
    n = SAMPLE_SP_ROWS
    per_layer = lambda *shape: pl.BlockSpec((None,) + shape, lambda l: (l,) + (0,) * len(shape))
    return pl.pallas_call(
        functools.partial(_prep_body, t_dec),
        grid=(depth,),
        in_specs=[per_layer(len(POOL_WINDOWS), POOL_GROUP_DIM, POOL_GROUP_DIM),
                  per_layer(SG_HEADS, CHUNK, CHUNK),
                  per_layer(CHUNK, SG_HEADS)],
        out_specs=[per_layer(2, POOL_WIDTH // 2, POOL_WIDTH // 2), per_layer(SG_HEADS, CHUNK, CHUNK),
                   per_layer(SG_HEADS, n, n), per_layer(CHUNK, SG_WIDTH), per_layer(n, SG_WIDTH)],
        out_shape=[jax.ShapeDtypeStruct((depth, 2, POOL_WIDTH // 2, POOL_WIDTH // 2), bf),
                   jax.ShapeDtypeStruct((depth, SG_HEADS, CHUNK, CHUNK), bf),
                   jax.ShapeDtypeStruct((depth, SG_HEADS, n, n), bf),
                   jax.ShapeDtypeStruct((depth, CHUNK, SG_WIDTH), f32),
                   jax.ShapeDtypeStruct((depth, n, SG_WIDTH), f32)],
        compiler_params=pltpu.CompilerParams(dimension_semantics=("arbitrary",)),
        name="prep_operands",
    )(pool_w, w_spatial, jnp.swapaxes(b_spatial, 1, 2))


def kernel(x_prompt, x_sample, state_pool, state_conv, norm1_g, w_in, pool_w, pool_scale, v_norm_g, w_spatial, b_spatial, w_out, norm2_g, w_gate, w_up, conv_w, conv_b, w_down, final_norm_g):
    depth = w_in.shape[0]
    bf = jnp.bfloat16
    pw, sp_p, sp_s, bias_p, bias_s = _prep_operands(pool_w, w_spatial, b_spatial, x_sample.shape[1])
    small = (norm1_g, pool_scale, v_norm_g, norm2_g, conv_w, conv_b, pw, final_norm_g[None, :])
    big_f32 = (w_in, w_out, w_gate, w_up, w_down)
    spatial_p = (sp_p, bias_p)
    spatial_s = (sp_s, bias_s)
    invcnt = _inv_counts(TM_PROMPT)

    n_dec, t_dec, _ = x_sample.shape
    xp, xs = x_prompt, x_sample.reshape(n_dec * t_dec, D_MODEL)
    mats = tuple(w[0].astype(bf) for w in big_f32)
    state_pool_t = jnp.transpose(state_pool, (0, 2, 1, 3))
    n_prompt = x_prompt.shape[0]
    f32 = jnp.float32
    stacked_p = (jnp.zeros((depth, n_prompt, POOL_BUF, POOL_WIDTH), f32),
                 jnp.zeros((depth, n_prompt, CONV_BUF, D_FF), f32))
    sample_stacked_shapes = ((depth, n_dec, POOL_BUF, POOL_WIDTH), (depth, n_dec, CONV_BUF, D_FF),
                             (depth, n_dec * t_dec, SG_WIDTH))
    stacked_s = None
    for i in range(depth):
        last = i == depth - 1
        res = _run_layer(i, depth, False, last, xp, None, invcnt, small, mats, spatial_p,
                         None if last else big_f32, stacked_p,
                         zero_fill=sample_stacked_shapes if i == 0 else ())
        xp, stacked_p = res[0], res[1:3]
        if i == 0:
            stacked_s = tuple(res[len(res) - len(sample_stacked_shapes):])
        next_mats = tuple(res[3:3 + len(big_f32)])
        res_s = _run_layer(i, depth, True, last, xs, (state_pool_t, state_conv), invcnt[1, :TM_SAMPLE], small,
                           mats, spatial_s, None, stacked_s)
        xs, stacked_s = res_s[0], res_s[1:4]
        mats = next_mats
    pool_p, conv_p = stacked_p
    pool_s, conv_s, v_s = stacked_s
    return (xp, xs.reshape(n_dec, t_dec, D_MODEL), pool_p, pool_s, conv_p, conv_s,
            v_s.reshape(depth, n_dec, t_dec, SG_WIDTH))
```

```python
import functools

import numpy as np
import jax
import jax.numpy as jnp
from jax import lax
from jax.experimental import pallas as pl
from jax.experimental.pallas import tpu as pltpu

D_MODEL = 1024
POOL_WIDTH = 512
POOL_WINDOWS = (2, 4, 8, 16)
POOL_GROUP_DIM = 128
POOL_BUF = 15
POOL_PAD = 16
SG_WIDTH = 512
SG_HEADS = 4
SG_HEAD_DIM = 128
CHUNK = 128
IN_WIDTH = POOL_WIDTH + 2 * SG_WIDTH
D_FF = 2816
CONV_BUF = 2
CONV_PAD = 8
EPS = 1e-6

TM_PROMPT = 512
TM_SAMPLE = 256
SAMPLE_SP_ROWS = 256
FF_CHUNK = 256
TAIL_PARTS = 4
TAIL_COLS = D_MODEL // TAIL_PARTS
VMEM_LIMIT_BYTES = 58 * 1024 * 1024

_INV_SQRT2 = 0.7071067811865476


def _gelu(x):
    return 0.5 * x * (1.0 + lax.erf(x * _INV_SQRT2))


def _rms(x, g):
    ms = jnp.mean(x * x, axis=-1, keepdims=True)
    return x * lax.rsqrt(ms + EPS) * g


def _bdot(a, b):
    return jnp.dot(a, b, preferred_element_type=jnp.float32)


def _layer_body(layer, S, T, is_sample, is_last, n_cast, n_alias, n_zero, seq_steps, n_tiles, *refs):
    refs = list(refs)
    x_ref = refs.pop(0)
    if is_sample:
        spool_ref = refs.pop(0)
        sconv_ref = refs.pop(0)
    (invcnt_ref, n1_ref, win_ref, pw_ref, pscale_ref, vng_ref, sp_ref, bfull_ref, wout_ref, n2_ref,
     wg_ref, wu_ref, cw_ref, cb_ref, wd_ref, fn_ref) = refs[:16]
    refs = refs[16:]
    n1_ref, pscale_ref, vng_ref, n2_ref, cb_ref = (
        r.at[pl.ds(layer, 1)] for r in (n1_ref, pscale_ref, vng_ref, n2_ref, cb_ref))
    cast_in, refs = refs[:n_cast], refs[n_cast:]
    refs = refs[n_alias:]
    y_ref, pool_out_ref, conv_out_ref = refs[:3]
    refs = refs[3:]
    if is_sample:
        v_out_ref = refs.pop(0)
    cast_out, refs = refs[:n_cast], refs[n_cast:]
    zero_out, refs = refs[:n_zero], refs[n_zero:]

    for dst in zero_out:
        dst[...] = jnp.zeros(dst.shape, dst.dtype)
    pbuf, st8, st4, st2, gbuf, mixbuf, h2buf, actbuf, x1buf = refs[:9]

    for src, dst in zip(cast_in, cast_out):
        dst[...] = src[...].astype(jnp.bfloat16)

    def tail_begin():
        y_ref[...] = x1buf[...]

    def tail_part(k):
        cols = slice(k * TAIL_COLS, (k + 1) * TAIL_COLS)
        y_ref[:, cols] += _bdot(actbuf[...], wd_ref[:, cols])

    def tail_end():
        if is_last:
            y_ref[...] = _rms(y_ref[...], fn_ref[...])

    def tail_all():
        tail_begin()
        for k in range(TAIL_PARTS):
            tail_part(k)
        tail_end()

    front = functools.partial(
        _tile_front, S, T, is_sample, x_ref, invcnt_ref, n1_ref, win_ref, pw_ref, pscale_ref, vng_ref, sp_ref,
        bfull_ref, wout_ref, n2_ref, wg_ref, wu_ref, cw_ref, cb_ref, pool_out_ref, conv_out_ref,
        pbuf, st8, st4, st2, gbuf, mixbuf, h2buf, actbuf, x1buf)

    if is_sample:
        pbuf[:, 0:POOL_PAD - POOL_BUF, :] = jnp.zeros((S, POOL_PAD - POOL_BUF, POOL_WIDTH), jnp.float32)
        for r in range(POOL_BUF):
            pbuf[:, POOL_PAD - POOL_BUF + r, :] = spool_ref[r]
        front(sconv_ref, v_out_ref, None, lambda k: None)
        tail_all()
        return

    s = pl.program_id(0)
    gcarry = refs[9]
    j = lax.rem(s, seq_steps)

    @pl.when(s == 0)
    def _():
        x1buf[...] = jnp.zeros(x1buf.shape, x1buf.dtype)
        actbuf[...] = jnp.zeros(actbuf.shape, actbuf.dtype)

    @pl.when(j == 0)
    def _():
        pbuf[:, 0:POOL_PAD, :] = jnp.zeros((S, POOL_PAD, POOL_WIDTH), jnp.float32)
        gcarry[...] = jnp.zeros((CONV_PAD, D_FF), jnp.float32)

    @pl.when(j > 0)
    def _():
        pbuf[:, 0:POOL_PAD, :] = pbuf[:, T:T + POOL_PAD, :]

    @pl.when(s < n_tiles)
    def _():
        tail_begin()
        front(None, None, gcarry, tail_part)
        tail_end()

    @pl.when(s == n_tiles)
    def _():
        tail_all()


def _tile_front(S, T, is_sample, x_ref, invcnt_ref, n1_ref, win_ref, pw_ref, pscale_ref, vng_ref, sp_ref,
                bfull_ref, wout_ref, n2_ref, wg_ref, wu_ref, cw_ref, cb_ref, pool_out_ref, conv_out_ref,
                pbuf, st8, st4, st2, gbuf, mixbuf, h2buf, actbuf, x1buf, sconv_ref, v_out_ref, gcarry, fill):
    TM = S * T
    R = POOL_PAD + T
    x = x_ref[...]
    fill(0)
    h = _rms(x, n1_ref[...]).astype(jnp.bfloat16)

    p = _bdot(h, win_ref[:, 0:POOL_WIDTH])
    u = _gelu(_bdot(h, win_ref[:, POOL_WIDTH:POOL_WIDTH + SG_WIDTH]))
    v = _rms(_gelu(_bdot(h, win_ref[:, POOL_WIDTH + SG_WIDTH:IN_WIDTH])), vng_ref[...])
    fill(1)
    if is_sample:
        v_out_ref[...] = v
    pbuf[:, POOL_PAD:R, :] = p.reshape(S, T, POOL_WIDTH)
    pool_out_ref[...] = pbuf[:, R - POOL_BUF:R, :]

    zeros8 = jnp.zeros((S, 8, POOL_WIDTH), jnp.float32)
    st8[:, 0:8, :] = zeros8
    st4[:, 0:8, :] = zeros8
    st2[:, 0:8, :] = zeros8
    G = POOL_GROUP_DIM
    st8[:, 8:R, 3 * G:4 * G] = pbuf[:, 8:R, 3 * G:4 * G] + pbuf[:, 0:R - 8, 3 * G:4 * G]
    st4[:, 8:R, 2 * G:3 * G] = pbuf[:, 8:R, 2 * G:3 * G] + pbuf[:, 4:R - 4, 2 * G:3 * G]
    st4[:, 8:R, 3 * G:4 * G] = st8[:, 8:R, 3 * G:4 * G] + st8[:, 4:R - 4, 3 * G:4 * G]
    st2[:, 8:R, 1 * G:2 * G] = pbuf[:, 8:R, 1 * G:2 * G] + pbuf[:, 6:R - 2, 1 * G:2 * G]
    st2[:, 8:R, 2 * G:4 * G] = st4[:, 8:R, 2 * G:4 * G] + st4[:, 6:R - 2, 2 * G:4 * G]
    sum0 = pbuf[:, POOL_PAD:R, 0:G] + pbuf[:, POOL_PAD - 1:R - 1, 0:G]
    sum123 = st2[:, POOL_PAD:R, G:4 * G] + st2[:, POOL_PAD - 1:R - 1, G:4 * G]
    wsum = jnp.concatenate([sum0, sum123], axis=-1).reshape(TM, POOL_WIDTH)
    dpool = wsum * invcnt_ref[...] - p
    d16 = dpool.astype(jnp.bfloat16)
    half = POOL_WIDTH // 2
    for k in range(2):
        ks = slice(k * half, (k + 1) * half)
        a = _bdot(d16[:, ks], pw_ref[k]) * pscale_ref[:, ks]
        mixbuf[:, ks] = a.astype(jnp.bfloat16)

    fill(2)

    v16 = v.astype(jnp.bfloat16)
    for hd in range(SG_HEADS):
        sl = slice(hd * SG_HEAD_DIM, (hd + 1) * SG_HEAD_DIM)
        osl = slice(POOL_WIDTH + hd * SG_HEAD_DIM, POOL_WIDTH + (hd + 1) * SG_HEAD_DIM)
        if is_sample:
            for r in range(TM // SAMPLE_SP_ROWS):
                rs = slice(r * SAMPLE_SP_ROWS, (r + 1) * SAMPLE_SP_ROWS)
                mixed = _bdot(sp_ref[hd], v16[rs, sl]) + bfull_ref[:, sl]
                mixbuf[rs, osl] = (u[rs, sl] * mixed).astype(jnp.bfloat16)
        else:
            chunks = [slice(c * CHUNK, (c + 1) * CHUNK) for c in range(TM // CHUNK)]
            mixed_all = _bdot(sp_ref[hd], jnp.concatenate([v16[rs, sl] for rs in chunks], axis=1))
            for c, rs in enumerate(chunks):
                mixed = mixed_all[:, c * SG_HEAD_DIM:(c + 1) * SG_HEAD_DIM] + bfull_ref[:, sl]
                mixbuf[rs, osl] = (u[rs, sl] * mixed).astype(jnp.bfloat16)

    x1 = x + _bdot(mixbuf[...], wout_ref[...])
    x1buf[...] = x1
    fill(3)
    h2buf[...] = _rms(x1, n2_ref[...]).astype(jnp.bfloat16)

    for c in range(D_FF // FF_CHUNK):
        cs = slice(c * FF_CHUNK, (c + 1) * FF_CHUNK)
        g = _bdot(h2buf[...], wg_ref[:, cs])
        up = _bdot(h2buf[...], wu_ref[:, cs])
        g3 = g.reshape(S, T, FF_CHUNK)
        if is_sample:
            gbuf[:, CONV_PAD - CONV_BUF:CONV_PAD, :] = sconv_ref[:, :, cs]
        else:
            gbuf[0, 0:CONV_PAD, :] = gcarry[:, cs]
            gcarry[:, cs] = g3[0, T - CONV_PAD:T, :]
        gbuf[:, CONV_PAD:CONV_PAD + T, :] = g3
        conv_out_ref[:, :, cs] = g3[:, T - CONV_BUF:T, :]
        conv = (cb_ref[:, cs]
                + cw_ref[0:1, cs] * gbuf[:, CONV_PAD - 2:CONV_PAD - 2 + T, :]
                + cw_ref[1:2, cs] * gbuf[:, CONV_PAD - 1:CONV_PAD - 1 + T, :]
                + cw_ref[2:3, cs] * g3)
        act = _gelu(conv).reshape(TM, FF_CHUNK) * up
        actbuf[:, cs] = act.astype(jnp.bfloat16)


def _const_spec(shape):
    nd = len(shape)
    return pl.BlockSpec(shape, lambda *_: (0,) * nd)


def _layer_spec(shape, layer):
    nd = len(shape)
    return pl.BlockSpec((None,) + tuple(shape), lambda *_: (layer,) + (0,) * nd)


DOWN_CAST_ROWS = 176


def _run_layer(layer, depth, is_sample, is_last, x, state, invcnt, small, mats, spatial, next_f32, stacked,
               zero_fill=()):
    (n1, pscale, vng, n2, cw, cb, pw, fn) = small
    (win, wout, wg, wu, wd) = mats
    sp, bfull = spatial
    TM = TM_SAMPLE if is_sample else TM_PROMPT
    f32 = jnp.float32
    if is_sample:
        n_seq = state[1].shape[1]
        T = x.shape[0] // n_seq
        S = TM // T
        seq_steps = 1
        n_tiles = x.shape[0] // TM
        grid = (n_tiles,)
        cur = prev = lambda s: s
        in_specs = [pl.BlockSpec((TM, D_MODEL), lambda s: (cur(s), 0)),
                    pl.BlockSpec((None, POOL_BUF, S, POOL_WIDTH), lambda s: (layer, 0, cur(s), 0)),
                    pl.BlockSpec((None, S, CONV_BUF, D_FF), lambda s: (layer, cur(s), 0, 0)),
                    _const_spec((TM, POOL_WIDTH))]
        operands = [x, state[0], state[1], invcnt]
        out_shape = [jax.ShapeDtypeStruct((x.shape[0], D_MODEL), f32),
                     jax.ShapeDtypeStruct((depth, n_seq, POOL_BUF, POOL_WIDTH), f32),
                     jax.ShapeDtypeStruct((depth, n_seq, CONV_BUF, D_FF), f32),
                     jax.ShapeDtypeStruct((depth, x.shape[0], SG_WIDTH), f32)]
        out_specs = [pl.BlockSpec((TM, D_MODEL), lambda s: (prev(s), 0)),
                     pl.BlockSpec((None, S, POOL_BUF, POOL_WIDTH), lambda s: (layer, cur(s), 0, 0)),
                     pl.BlockSpec((None, S, CONV_BUF, D_FF), lambda s: (layer, cur(s), 0, 0)),
                     pl.BlockSpec((None, TM, SG_WIDTH), lambda s: (layer, cur(s), 0))]
        n_stacked = 3
    else:
        B, L, _ = x.shape
        S, T = 1, TM
        seq_steps = steps = L // TM
        n_tiles = B * steps
        grid = (n_tiles + 1,)
        cur = lambda s: jnp.minimum(s, n_tiles - 1)
        prev = lambda s: jnp.maximum(s - 1, 0)
        in_specs = [pl.BlockSpec((None, TM, D_MODEL), lambda s: (cur(s) // steps, cur(s) % steps, 0)),
                    pl.BlockSpec((None, TM, POOL_WIDTH), lambda s: (jnp.minimum(cur(s) % steps, 1), 0, 0))]
        operands = [x, invcnt]
        out_shape = [jax.ShapeDtypeStruct((B, L, D_MODEL), f32),
                     jax.ShapeDtypeStruct((depth, B, POOL_BUF, POOL_WIDTH), f32),
                     jax.ShapeDtypeStruct((depth, B, CONV_BUF, D_FF), f32)]
        out_specs = [pl.BlockSpec((None, TM, D_MODEL), lambda s: (prev(s) // steps, prev(s) % steps, 0)),
                     pl.BlockSpec((None, 1, POOL_BUF, POOL_WIDTH), lambda s: (layer, cur(s) // steps, 0, 0)),
                     pl.BlockSpec((None, 1, CONV_BUF, D_FF), lambda s: (layer, cur(s) // steps, 0, 0))]
        n_stacked = 2
    semantics = ("arbitrary",)

    in_specs += [
        _const_spec((depth, D_MODEL)),
        _const_spec((D_MODEL, IN_WIDTH)),
        _layer_spec((2, POOL_WIDTH // 2, POOL_WIDTH // 2), layer),
        _const_spec((depth, POOL_WIDTH)),
        _const_spec((depth, SG_WIDTH)),
        _layer_spec(sp.shape[1:], layer),
        _layer_spec(bfull.shape[1:], layer),
        _const_spec((D_MODEL, D_MODEL)),
        _const_spec((depth, D_MODEL)),
        _const_spec((D_MODEL, D_FF)),
        _const_spec((D_MODEL, D_FF)),
        _layer_spec((3, D_FF), layer),
        _const_spec((depth, D_FF)),
        _const_spec((D_FF, D_MODEL)),
        _const_spec((1, D_MODEL)),
    ]
    operands += [n1, win, pw, pscale, vng, sp, bfull, wout, n2, wg, wu, cw, cb, wd, fn]

    n_cast = 0
    if next_f32 is not None:
        assert not is_sample
        for w in next_f32:
            rows, share = ((D_MODEL // n_tiles, 1) if w.shape[1] == D_MODEL
                           else (DOWN_CAST_ROWS, n_tiles * DOWN_CAST_ROWS // D_FF))
            cols = w.shape[2]
            in_specs.append(pl.BlockSpec((None, rows, cols),
                                         lambda s, share=share: (layer + 1, cur(s) // share, 0)))
            out_specs.append(pl.BlockSpec((rows, cols), lambda s, share=share: (cur(s) // share, 0)))
            out_shape.append(jax.ShapeDtypeStruct(w.shape[1:], jnp.bfloat16))
            operands.append(w)
            n_cast += 1

    for shape in zero_fill:
        assert not is_sample and n_tiles % shape[0] == 0 and len(shape) == 4
        per_layer = n_tiles // shape[0]
        blk = (1, shape[1], shape[2] // per_layer, shape[3])
        assert blk[2] * per_layer == shape[2]
        out_specs.append(pl.BlockSpec(
            blk, lambda s, per_layer=per_layer: (cur(s) // per_layer, 0, cur(s) % per_layer, 0)))
        out_shape.append(jax.ShapeDtypeStruct(shape, f32))

    aliases = {}
    assert len(stacked) == n_stacked
    for k, arr in enumerate(stacked):
        aliases[len(operands)] = 1 + k
        in_specs.append(pl.BlockSpec(memory_space=pl.ANY))
        operands.append(arr)
    n_alias = n_stacked

    R = POOL_PAD + T
    scratch = [pltpu.VMEM((S, R, POOL_WIDTH), jnp.float32)] * 4
    scratch += [pltpu.VMEM((S, CONV_PAD + T, FF_CHUNK), jnp.float32),
                pltpu.VMEM((TM, D_MODEL), jnp.bfloat16),
                pltpu.VMEM((TM, D_MODEL), jnp.bfloat16),
                pltpu.VMEM((TM, D_FF), jnp.bfloat16),
                pltpu.VMEM((TM, D_MODEL), jnp.float32)]
    if not is_sample:
        scratch.append(pltpu.VMEM((CONV_PAD, D_FF), jnp.float32))

    return pl.pallas_call(
        functools.partial(_layer_body, layer, S, T, is_sample, is_last, n_cast, n_alias, len(zero_fill), seq_steps,
                          n_tiles),
        grid=grid,
        in_specs=in_specs,
        out_specs=out_specs,
        out_shape=out_shape,
        scratch_shapes=scratch,
        input_output_aliases=aliases,
        compiler_params=pltpu.CompilerParams(dimension_semantics=semantics,
                                             vmem_limit_bytes=VMEM_LIMIT_BYTES),
        name=("sample" if is_sample else "prompt") + f"_layer{layer}",
    )(*operands)


SAMPLE_SEQS = 64


def _sample_body(layer, T, is_last, x_ref, spool_ref, sconv_ref, wsp_ref, bsp_ref, n1_ref, win_ref, pw_ref,
                 pscale_ref, vng_ref, wout_ref, n2_ref, wg_ref, wu_ref, cw_ref, cb_ref, wd_ref, fn_ref,
                 pool_in_all, conv_in_all, v_in_all, y_ref, pool_out_ref, conv_out_ref, v_out_ref,
                 gbuf, mixbuf, h2buf, actbuf):
    del pool_in_all, conv_in_all, v_in_all
    S = SAMPLE_SEQS
    TM = T * S
    n1_ref, pscale_ref, vng_ref, n2_ref, cb_ref = (
        r.at[pl.ds(layer, 1)] for r in (n1_ref, pscale_ref, vng_ref, n2_ref, cb_ref))
    slab = lambda t: slice(t * S, (t + 1) * S)

    x = x_ref[...].reshape(TM, D_MODEL)
    h = _rms(x, n1_ref[...]).astype(jnp.bfloat16)
    p = _bdot(h, win_ref[:, 0:POOL_WIDTH])
    u = _gelu(_bdot(h, win_ref[:, POOL_WIDTH:POOL_WIDTH + SG_WIDTH]))
    v = _rms(_gelu(_bdot(h, win_ref[:, POOL_WIDTH + SG_WIDTH:IN_WIDTH])), vng_ref[...])
    v_out_ref[...] = v.reshape(T, S, SG_WIDTH)

    pool_out_ref[0:POOL_BUF - T] = spool_ref[T:POOL_BUF]
    pool_out_ref[POOL_BUF - T:POOL_BUF] = p.reshape(T, S, POOL_WIDTH)
    G = POOL_GROUP_DIM
    for gi, w in enumerate(POOL_WINDOWS):
        gs = slice(gi * G, (gi + 1) * G)

        def row(j):
            return spool_ref[j, :, gs] if j < POOL_BUF else p[slab(j - POOL_BUF), gs]

        for t in range(T):
            acc = row(POOL_BUF + t)
            for k in range(1, w):
                acc = acc + row(POOL_BUF + t - k)
            d = acc * (1.0 / w) - p[slab(t), gs]
            h2buf[slab(t), gs] = d.astype(jnp.bfloat16)
    half = POOL_WIDTH // 2
    for k in range(2):
        ks = slice(k * half, (k + 1) * half)
        a = _bdot(h2buf[:, ks], pw_ref[k]) * pscale_ref[:, ks]
        mixbuf[:, ks] = a.astype(jnp.bfloat16)

    for hd in range(SG_HEADS):
        hs = slice(hd * SG_HEAD_DIM, (hd + 1) * SG_HEAD_DIM)
        for t in range(T):
            mixed = jnp.full((S, SG_HEAD_DIM), bsp_ref[layer, hd * T + t], jnp.float32)
            for t2 in range(t + 1):
                mixed = mixed + wsp_ref[layer, (hd * T + t) * T + t2] * v[slab(t2), hs]
            mixbuf[slab(t), POOL_WIDTH + hd * SG_HEAD_DIM:POOL_WIDTH + (hd + 1) * SG_HEAD_DIM] = (
                (u[slab(t), hs] * mixed).astype(jnp.bfloat16))

    x1 = x + _bdot(mixbuf[...], wout_ref[...])
    h2buf[...] = _rms(x1, n2_ref[...]).astype(jnp.bfloat16)

    hist = CONV_BUF * S
    for c in range(D_FF // FF_CHUNK):
        cs = slice(c * FF_CHUNK, (c + 1) * FF_CHUNK)
        g = _bdot(h2buf[...], wg_ref[:, cs])
        up = _bdot(h2buf[...], wu_ref[:, cs])
        gbuf[0:hist, :] = sconv_ref[:, :, cs].reshape(hist, FF_CHUNK)
        gbuf[hist:hist + TM, :] = g
        conv_out_ref[:, :, cs] = g[TM - hist:TM, :].reshape(CONV_BUF, S, FF_CHUNK)
        conv = (cb_ref[:, cs]
                + cw_ref[0:1, cs] * gbuf[0:TM, :]
                + cw_ref[1:2, cs] * gbuf[S:S + TM, :]
                + cw_ref[2:3, cs] * g)
        actbuf[:, cs] = (_gelu(conv) * up).astype(jnp.bfloat16)

    out = x1 + _bdot(actbuf[...], wd_ref[...])
    if is_last:
        out = _rms(out, fn_ref[...])
    y_ref[...] = out.reshape(T, S, D_MODEL)


def _run_sample_layer(layer, depth, is_last, x, state_pool_t, state_conv_t, wsp, bsp, small, mats, stacked):
    (n1, pscale, vng, n2, cw, cb, pw, fn) = small
    (win, wout, wg, wu, wd) = mats
    T, n_seq, _ = x.shape
    S = SAMPLE_SEQS
    TM = T * S
    f32 = jnp.float32
    smem = pl.BlockSpec(memory_space=pltpu.SMEM)
    in_specs = [
        pl.BlockSpec((T, S, D_MODEL), lambda i: (0, i, 0)),
        pl.BlockSpec((None, POOL_BUF, S, POOL_WIDTH), lambda i: (layer, 0, i, 0)),
        pl.BlockSpec((None, CONV_BUF, S, D_FF), lambda i: (layer, 0, i, 0)),
        smem, smem,
        _const_spec((depth, D_MODEL)),
        _const_spec((D_MODEL, IN_WIDTH)),
        _layer_spec((2, POOL_WIDTH // 2, POOL_WIDTH // 2), layer),
        _const_spec((depth, POOL_WIDTH)),
        _const_spec((depth, SG_WIDTH)),
        _const_spec((D_MODEL, D_MODEL)),
        _const_spec((depth, D_MODEL)),
        _const_spec((D_MODEL, D_FF)),
        _const_spec((D_MODEL, D_FF)),
        _layer_spec((3, D_FF), layer),
        _const_spec((depth, D_FF)),
        _const_spec((D_FF, D_MODEL)),
        _const_spec((1, D_MODEL)),
    ]
    operands = [x, state_pool_t, state_conv_t, wsp, bsp, n1, win, pw, pscale, vng, wout, n2, wg, wu, cw, cb, wd, fn]
    aliases = {}
    for k, arr in enumerate(stacked):
        aliases[len(operands)] = 1 + k
        in_specs.append(pl.BlockSpec(memory_space=pl.ANY))
        operands.append(arr)
    out_shape = [jax.ShapeDtypeStruct((T, n_seq, D_MODEL), f32),
                 jax.ShapeDtypeStruct((depth, POOL_BUF, n_seq, POOL_WIDTH), f32),
                 jax.ShapeDtypeStruct((depth, CONV_BUF, n_seq, D_FF), f32),
                 jax.ShapeDtypeStruct((depth, T, n_seq, SG_WIDTH), f32)]
    out_specs = [pl.BlockSpec((T, S, D_MODEL), lambda i: (0, i, 0)),
                 pl.BlockSpec((None, POOL_BUF, S, POOL_WIDTH), lambda i: (layer, 0, i, 0)),
                 pl.BlockSpec((None, CONV_BUF, S, D_FF), lambda i: (layer, 0, i, 0)),
                 pl.BlockSpec((None, T, S, SG_WIDTH), lambda i: (layer, 0, i, 0))]
    scratch = [pltpu.VMEM(((CONV_BUF + T) * S, FF_CHUNK), f32),
               pltpu.VMEM((TM, D_MODEL), jnp.bfloat16),
               pltpu.VMEM((TM, D_MODEL), jnp.bfloat16),
               pltpu.VMEM((TM, D_FF), jnp.bfloat16)]
    return pl.pallas_call(
        functools.partial(_sample_body, layer, T, is_last),
        grid=(n_seq // S,),
        in_specs=in_specs,
        out_specs=out_specs,
        out_shape=out_shape,
        scratch_shapes=scratch,
        input_output_aliases=aliases,
        compiler_params=pltpu.CompilerParams(dimension_semantics=("arbitrary",),
                                             vmem_limit_bytes=VMEM_LIMIT_BYTES),
        name=f"sample_layer{layer}",
    )(*operands)


def _inv_counts(n_rows):
    win = np.repeat(np.asarray(POOL_WINDOWS, np.float32), POOL_GROUP_DIM)[None, :]
    pos = np.arange(n_rows, dtype=np.float32)[:, None]
    first = 1.0 / np.minimum(pos + 1.0, win)
    later = np.broadcast_to(1.0 / win, (n_rows, POOL_WIDTH))
    return jnp.asarray(np.stack([first, later]).astype(np.float32))


def _prep_body(t_dec, pool_w_ref, wsp_ref, bt_ref, pw_ref, spp_ref, sps_ref, bfp_ref, bfs_ref):
    bf = jnp.bfloat16
    G = POOL_GROUP_DIM
    for k in range(2):
        for a in range(2):
            for b in range(2):
                blk = pool_w_ref[2 * k + a].astype(bf) if a == b else jnp.zeros((G, G), bf)
                pw_ref[k, a * G:(a + 1) * G, b * G:(b + 1) * G] = blk

    n = SAMPLE_SP_ROWS
    pos_mask = t_dec - 1
    shift = t_dec.bit_length() - 1
    row = lax.broadcasted_iota(jnp.int32, (CHUNK, CHUNK), 0)
    col = lax.broadcasted_iota(jnp.int32, (CHUNK, CHUNK), 1)
    causal = row >= col
    expand = (lax.broadcasted_iota(jnp.int32, (n, CHUNK), 1)
              == (lax.broadcasted_iota(jnp.int32, (n, CHUNK), 0) & pos_mask)).astype(bf)
    expand_t = (lax.broadcasted_iota(jnp.int32, (CHUNK, n), 0)
                == (lax.broadcasted_iota(jnp.int32, (CHUNK, n), 1) & pos_mask)).astype(bf)
    r2 = lax.broadcasted_iota(jnp.int32, (n, n), 0)
    c2 = lax.broadcasted_iota(jnp.int32, (n, n), 1)
    keep = jnp.logical_and((r2 >> shift) == (c2 >> shift), (r2 & pos_mask) >= (c2 & pos_mask))
    for h in range(SG_HEADS):
        w = wsp_ref[h]
        spp_ref[h] = jnp.where(causal, w, 0.0).astype(bf)
        rows = _bdot(expand, w.astype(bf))
        tiled = _bdot(rows.astype(bf), expand_t)
        sps_ref[h] = jnp.where(keep, tiled, 0.0).astype(bf)
        bfp_ref[:, h * SG_HEAD_DIM:(h + 1) * SG_HEAD_DIM] = jnp.broadcast_to(
            bt_ref[:, h:h + 1], (CHUNK, SG_HEAD_DIM))
    first = bfp_ref[0:t_dec, :]
    bfs_ref[...] = jnp.broadcast_to(first[None], (n // t_dec, t_dec, SG_WIDTH)).reshape(n, SG_WIDTH)


def _prep_operands(pool_w, w_spatial, b_spatial, t_dec):
    depth = pool_w.shape[0]
    assert t_dec & (t_dec - 1) == 0 and t_dec % 8 == 0 and SAMPLE_SP_ROWS % t_dec == 0
    bf, f32 = jnp.bfloat16, jnp.float32
    n = SAMPLE_SP_ROWS
    per_layer = lambda *shape: pl.BlockSpec((None,) + shape, lambda l: (l,) + (0,) * len(shape))
    return pl.pallas_call(
        functools.partial(_prep_body, t_dec),
        grid=(depth,),
        in_specs=[per_layer(len(POOL_WINDOWS), POOL_GROUP_DIM, POOL_GROUP_DIM),
                  per_layer(SG_HEADS, CHUNK, CHUNK),
                  per_layer(CHUNK, SG_HEADS)],
        out_specs=[per_layer(2, POOL_WIDTH // 2, POOL_WIDTH // 2), per_layer(SG_HEADS, CHUNK, CHUNK),
                   per_layer(SG_HEADS, n, n), per_layer(CHUNK, SG_WIDTH), per_layer(n, SG_WIDTH)],
        out_shape=[jax.ShapeDtypeStruct((depth, 2, POOL_WIDTH // 2, POOL_WIDTH // 2), bf),
                   jax.ShapeDtypeStruct((depth, SG_HEADS, CHUNK, CHUNK), bf),
                   jax.ShapeDtypeStruct((depth, SG_HEADS, n, n), bf),
                   jax.ShapeDtypeStruct((depth, CHUNK, SG_WIDTH), f32),
                   jax.ShapeDtypeStruct((depth, n, SG_WIDTH), f32)],
        compiler_params=pltpu.CompilerParams(dimension_semantics=("arbitrary",)),
        name="prep_operands",
    )(pool_w, w_spatial, jnp.swapaxes(b_spatial, 1, 2))


def kernel(x_prompt, x_sample, state_pool, state_conv, norm1_g, w_in, pool_w, pool_scale, v_norm_g, w_spatial, b_spatial, w_out, norm2_g, w_gate, w_up, conv_w, conv_b, w_down, final_norm_g):
    depth = w_in.shape[0]
    bf = jnp.bfloat16
    pw, sp_p, sp_s, bias_p, bias_s = _prep_operands(pool_w, w_spatial, b_spatial, x_sample.shape[1])
    small = (norm1_g, pool_scale, v_norm_g, norm2_g, conv_w, conv_b, pw, final_norm_g[None, :])
    big_f32 = (w_in, w_out, w_gate, w_up, w_down)
    spatial_p = (sp_p, bias_p)
    spatial_s = (sp_s, bias_s)
    invcnt = _inv_counts(TM_PROMPT)

    n_dec, t_dec, _ = x_sample.shape
    mats = tuple(w[0].astype(bf) for w in big_f32)
    xp, xs = x_prompt, jnp.transpose(x_sample, (1, 0, 2))
    state_pool_t = jnp.transpose(state_pool, (0, 2, 1, 3))
    state_conv_t = jnp.transpose(state_conv, (0, 2, 1, 3))
    wsp = w_spatial[:, :, :t_dec, :t_dec].reshape(depth, SG_HEADS * t_dec * t_dec)
    bsp = b_spatial[:, :, :t_dec].reshape(depth, SG_HEADS * t_dec)
    n_prompt = x_prompt.shape[0]
    f32 = jnp.float32
    stacked_p = (jnp.zeros((depth, n_prompt, POOL_BUF, POOL_WIDTH), f32),
                 jnp.zeros((depth, n_prompt, CONV_BUF, D_FF), f32))
    sample_stacked_shapes = ((depth, POOL_BUF, n_dec, POOL_WIDTH), (depth, CONV_BUF, n_dec, D_FF),
                             (depth, t_dec, n_dec, SG_WIDTH))
    stacked_s = None
    for i in range(depth):
        last = i == depth - 1
        res = _run_layer(i, depth, False, last, xp, None, invcnt, small, mats, spatial_p,
                         None if last else big_f32, stacked_p,
                         zero_fill=sample_stacked_shapes if i == 0 else ())
        xp, stacked_p = res[0], res[1:3]
        if i == 0:
            stacked_s = tuple(res[len(res) - len(sample_stacked_shapes):])
        next_mats = tuple(res[3:3 + len(big_f32)])
        res_s = _run_sample_layer(i, depth, last, xs, state_pool_t, state_conv_t, wsp, bsp, small, mats, stacked_s)
        xs, stacked_s = res_s[0], res_s[1:4]
        mats = next_mats
    pool_p, conv_p = stacked_p
    pool_s, conv_s, v_s = (jnp.transpose(a, (0, 2, 1, 3)) for a in stacked_s)
    return (xp, jnp.transpose(xs, (1, 0, 2)), pool_p, pool_s, conv_p, conv_s, v_s)
```

```python
import functools

import numpy as np
import jax
import jax.numpy as jnp
from jax import lax
from jax.experimental import pallas as pl
from jax.experimental.pallas import tpu as pltpu

D_MODEL = 1024
POOL_WIDTH = 512
POOL_WINDOWS = (2, 4, 8, 16)
POOL_GROUP_DIM = 128
POOL_BUF = 15
POOL_PAD = 16
SG_WIDTH = 512
SG_HEADS = 4
SG_HEAD_DIM = 128
CHUNK = 128
IN_WIDTH = POOL_WIDTH + 2 * SG_WIDTH
D_FF = 2816
CONV_BUF = 2
CONV_PAD = 8
EPS = 1e-6

TM_PROMPT = 512
SAMPLE_SEQS = 64
FF_CHUNK = 256
TAIL_PARTS = 4
TAIL_COLS = D_MODEL // TAIL_PARTS
DOWN_CAST_ROWS = 176
VMEM_LIMIT_BYTES = 58 * 1024 * 1024

_INV_SQRT2 = 0.7071067811865476


def _gelu(x):
    return 0.5 * x * (1.0 + lax.erf(x * _INV_SQRT2))


def _rms(x, g):
    ms = jnp.mean(x * x, axis=-1, keepdims=True)
    return x * lax.rsqrt(ms + EPS) * g


def _bdot(a, b):
    return jnp.dot(a, b, preferred_element_type=jnp.float32)


def _const_spec(shape):
    nd = len(shape)
    return pl.BlockSpec(shape, lambda *_: (0,) * nd)


def _layer_spec(shape, layer):
    nd = len(shape)
    return pl.BlockSpec((None,) + tuple(shape), lambda *_: (layer,) + (0,) * nd)


def _prompt_body(layer, T, is_last, n_cast, n_alias, n_zero, seq_steps, n_tiles, *refs):
    S = 1
    refs = list(refs)
    x_ref = refs.pop(0)
    (invcnt_ref, n1_ref, win_ref, pw_ref, pscale_ref, vng_ref, sp_ref, bfull_ref, wout_ref, n2_ref,
     wg_ref, wu_ref, cw_ref, cb_ref, wd_ref, fn_ref) = refs[:16]
    refs = refs[16:]
    n1_ref, pscale_ref, vng_ref, n2_ref, cb_ref = (
        r.at[pl.ds(layer, 1)] for r in (n1_ref, pscale_ref, vng_ref, n2_ref, cb_ref))
    cast_in, refs = refs[:n_cast], refs[n_cast:]
    refs = refs[n_alias:]
    y_ref, pool_out_ref, conv_out_ref = refs[:3]
    refs = refs[3:]
    cast_out, refs = refs[:n_cast], refs[n_cast:]
    zero_out, refs = refs[:n_zero], refs[n_zero:]

    for dst in zero_out:
        dst[...] = jnp.zeros(dst.shape, dst.dtype)
    pbuf, st8, st4, st2, gbuf, mixbuf, h2buf, actbuf, x1buf = refs[:9]

    for src, dst in zip(cast_in, cast_out):
        dst[...] = src[...].astype(jnp.bfloat16)

    def tail_begin():
        y_ref[...] = x1buf[...]

    def tail_part(k):
        cols = slice(k * TAIL_COLS, (k + 1) * TAIL_COLS)
        y_ref[:, cols] += _bdot(actbuf[...], wd_ref[:, cols])

    def tail_end():
        if is_last:
            y_ref[...] = _rms(y_ref[...], fn_ref[...])

    def tail_all():
        tail_begin()
        for k in range(TAIL_PARTS):
            tail_part(k)
        tail_end()

    front = functools.partial(
        _prompt_front, S, T, x_ref, invcnt_ref, n1_ref, win_ref, pw_ref, pscale_ref, vng_ref, sp_ref,
        bfull_ref, wout_ref, n2_ref, wg_ref, wu_ref, cw_ref, cb_ref, pool_out_ref, conv_out_ref,
        pbuf, st8, st4, st2, gbuf, mixbuf, h2buf, actbuf, x1buf)

    s = pl.program_id(0)
    gcarry = refs[9]
    j = lax.rem(s, seq_steps)

    @pl.when(s == 0)
    def _():
        x1buf[...] = jnp.zeros(x1buf.shape, x1buf.dtype)
        actbuf[...] = jnp.zeros(actbuf.shape, actbuf.dtype)

    @pl.when(j == 0)
    def _():
        pbuf[:, 0:POOL_PAD, :] = jnp.zeros((S, POOL_PAD, POOL_WIDTH), jnp.float32)
        gcarry[...] = jnp.zeros((CONV_PAD, D_FF), jnp.float32)

    @pl.when(j > 0)
    def _():
        pbuf[:, 0:POOL_PAD, :] = pbuf[:, T:T + POOL_PAD, :]

    @pl.when(s < n_tiles)
    def _():
        tail_begin()
        front(gcarry, tail_part)
        tail_end()

    @pl.when(s == n_tiles)
    def _():
        tail_all()


def _prompt_front(S, T, x_ref, invcnt_ref, n1_ref, win_ref, pw_ref, pscale_ref, vng_ref, sp_ref,
                  bfull_ref, wout_ref, n2_ref, wg_ref, wu_ref, cw_ref, cb_ref, pool_out_ref, conv_out_ref,
                  pbuf, st8, st4, st2, gbuf, mixbuf, h2buf, actbuf, x1buf, gcarry, fill):
    TM = S * T
    R = POOL_PAD + T
    x = x_ref[...]
    fill(0)
    h = _rms(x, n1_ref[...]).astype(jnp.bfloat16)

    p = _bdot(h, win_ref[:, 0:POOL_WIDTH])
    u = _gelu(_bdot(h, win_ref[:, POOL_WIDTH:POOL_WIDTH + SG_WIDTH]))
    v = _rms(_gelu(_bdot(h, win_ref[:, POOL_WIDTH + SG_WIDTH:IN_WIDTH])), vng_ref[...])
    fill(1)
    pbuf[:, POOL_PAD:R, :] = p.reshape(S, T, POOL_WIDTH)
    pool_out_ref[...] = pbuf[:, R - POOL_BUF:R, :]

    zeros8 = jnp.zeros((S, 8, POOL_WIDTH), jnp.float32)
    st8[:, 0:8, :] = zeros8
    st4[:, 0:8, :] = zeros8
    st2[:, 0:8, :] = zeros8
    G = POOL_GROUP_DIM
    st8[:, 8:R, 3 * G:4 * G] = pbuf[:, 8:R, 3 * G:4 * G] + pbuf[:, 0:R - 8, 3 * G:4 * G]
    st4[:, 8:R, 2 * G:3 * G] = pbuf[:, 8:R, 2 * G:3 * G] + pbuf[:, 4:R - 4, 2 * G:3 * G]
    st4[:, 8:R, 3 * G:4 * G] = st8[:, 8:R, 3 * G:4 * G] + st8[:, 4:R - 4, 3 * G:4 * G]
    st2[:, 8:R, 1 * G:2 * G] = pbuf[:, 8:R, 1 * G:2 * G] + pbuf[:, 6:R - 2, 1 * G:2 * G]
    st2[:, 8:R, 2 * G:4 * G] = st4[:, 8:R, 2 * G:4 * G] + st4[:, 6:R - 2, 2 * G:4 * G]
    sum0 = pbuf[:, POOL_PAD:R, 0:G] + pbuf[:, POOL_PAD - 1:R - 1, 0:G]
    sum123 = st2[:, POOL_PAD:R, G:4 * G] + st2[:, POOL_PAD - 1:R - 1, G:4 * G]
    wsum = jnp.concatenate([sum0, sum123], axis=-1).reshape(TM, POOL_WIDTH)
    dpool = wsum * invcnt_ref[...] - p
    d16 = dpool.astype(jnp.bfloat16)
    half = POOL_WIDTH // 2
    for k in range(2):
        ks = slice(k * half, (k + 1) * half)
        a = _bdot(d16[:, ks], pw_ref[k]) * pscale_ref[:, ks]
        mixbuf[:, ks] = a.astype(jnp.bfloat16)

    fill(2)

    v16 = v.astype(jnp.bfloat16)
    for hd in range(SG_HEADS):
        sl = slice(hd * SG_HEAD_DIM, (hd + 1) * SG_HEAD_DIM)
        osl = slice(POOL_WIDTH + hd * SG_HEAD_DIM, POOL_WIDTH + (hd + 1) * SG_HEAD_DIM)
        chunks = [slice(c * CHUNK, (c + 1) * CHUNK) for c in range(TM // CHUNK)]
        mixed_all = _bdot(sp_ref[hd], jnp.concatenate([v16[rs, sl] for rs in chunks], axis=1))
        for c, rs in enumerate(chunks):
            mixed = mixed_all[:, c * SG_HEAD_DIM:(c + 1) * SG_HEAD_DIM] + bfull_ref[:, sl]
            mixbuf[rs, osl] = (u[rs, sl] * mixed).astype(jnp.bfloat16)

    x1 = x + _bdot(mixbuf[...], wout_ref[...])
    x1buf[...] = x1
    fill(3)
    h2buf[...] = _rms(x1, n2_ref[...]).astype(jnp.bfloat16)

    for c in range(D_FF // FF_CHUNK):
        cs = slice(c * FF_CHUNK, (c + 1) * FF_CHUNK)
        g = _bdot(h2buf[...], wg_ref[:, cs])
        up = _bdot(h2buf[...], wu_ref[:, cs])
        g3 = g.reshape(S, T, FF_CHUNK)
        gbuf[0, 0:CONV_PAD, :] = gcarry[:, cs]
        gcarry[:, cs] = g3[0, T - CONV_PAD:T, :]
        gbuf[:, CONV_PAD:CONV_PAD + T, :] = g3
        conv_out_ref[:, :, cs] = g3[:, T - CONV_BUF:T, :]
        conv = (cb_ref[:, cs]
                + cw_ref[0:1, cs] * gbuf[:, CONV_PAD - 2:CONV_PAD - 2 + T, :]
                + cw_ref[1:2, cs] * gbuf[:, CONV_PAD - 1:CONV_PAD - 1 + T, :]
                + cw_ref[2:3, cs] * g3)
        act = _gelu(conv).reshape(TM, FF_CHUNK) * up
        actbuf[:, cs] = act.astype(jnp.bfloat16)


def _run_prompt_layer(layer, depth, is_last, x, invcnt, small, mats, spatial, next_f32, stacked, zero_fill=()):
    (n1, pscale, vng, n2, cw, cb, pw, fn) = small
    (win, wout, wg, wu, wd) = mats
    sp, bfull = spatial
    TM = TM_PROMPT
    f32 = jnp.float32
    B, L, _ = x.shape
    S, T = 1, TM
    seq_steps = steps = L // TM
    n_tiles = B * steps
    grid = (n_tiles + 1,)
    cur = lambda s: jnp.minimum(s, n_tiles - 1)
    prev = lambda s: jnp.maximum(s - 1, 0)
    in_specs = [pl.BlockSpec((None, TM, D_MODEL), lambda s: (cur(s) // steps, cur(s) % steps, 0)),
                pl.BlockSpec((None, TM, POOL_WIDTH), lambda s: (jnp.minimum(cur(s) % steps, 1), 0, 0))]
    operands = [x, invcnt]
    out_shape = [jax.ShapeDtypeStruct((B, L, D_MODEL), f32),
                 jax.ShapeDtypeStruct((depth, B, POOL_BUF, POOL_WIDTH), f32),
                 jax.ShapeDtypeStruct((depth, B, CONV_BUF, D_FF), f32)]
    out_specs = [pl.BlockSpec((None, TM, D_MODEL), lambda s: (prev(s) // steps, prev(s) % steps, 0)),
                 pl.BlockSpec((None, 1, POOL_BUF, POOL_WIDTH), lambda s: (layer, cur(s) // steps, 0, 0)),
                 pl.BlockSpec((None, 1, CONV_BUF, D_FF), lambda s: (layer, cur(s) // steps, 0, 0))]

    in_specs += [
        _const_spec((depth, D_MODEL)),
        _const_spec((D_MODEL, IN_WIDTH)),
        _layer_spec((2, POOL_WIDTH // 2, POOL_WIDTH // 2), layer),
        _const_spec((depth, POOL_WIDTH)),
        _const_spec((depth, SG_WIDTH)),
        _layer_spec(sp.shape[1:], layer),
        _layer_spec(bfull.shape[1:], layer),
        _const_spec((D_MODEL, D_MODEL)),
        _const_spec((depth, D_MODEL)),
        _const_spec((D_MODEL, D_FF)),
        _const_spec((D_MODEL, D_FF)),
        _layer_spec((3, D_FF), layer),
        _const_spec((depth, D_FF)),
        _const_spec((D_FF, D_MODEL)),
        _const_spec((1, D_MODEL)),
    ]
    operands += [n1, win, pw, pscale, vng, sp, bfull, wout, n2, wg, wu, cw, cb, wd, fn]

    n_cast = 0
    if next_f32 is not None:
        for w in next_f32:
            rows, share = ((D_MODEL // n_tiles, 1) if w.shape[1] == D_MODEL
                           else (DOWN_CAST_ROWS, n_tiles * DOWN_CAST_ROWS // D_FF))
            cols = w.shape[2]
            in_specs.append(pl.BlockSpec((None, rows, cols),
                                         lambda s, share=share: (layer + 1, cur(s) // share, 0)))
            out_specs.append(pl.BlockSpec((rows, cols), lambda s, share=share: (cur(s) // share, 0)))
            out_shape.append(jax.ShapeDtypeStruct(w.shape[1:], jnp.bfloat16))
            operands.append(w)
            n_cast += 1

    for shape in zero_fill:
        assert n_tiles % shape[0] == 0 and len(shape) == 4
        per_layer = n_tiles // shape[0]
        blk = (1, shape[1], shape[2] // per_layer, shape[3])
        assert blk[2] * per_layer == shape[2]
        out_specs.append(pl.BlockSpec(
            blk, lambda s, per_layer=per_layer: (cur(s) // per_layer, 0, cur(s) % per_layer, 0)))
        out_shape.append(jax.ShapeDtypeStruct(shape, f32))

    aliases = {}
    for k, arr in enumerate(stacked):
        aliases[len(operands)] = 1 + k
        in_specs.append(pl.BlockSpec(memory_space=pl.ANY))
        operands.append(arr)
    n_alias = len(stacked)

    R = POOL_PAD + T
    scratch = [pltpu.VMEM((S, R, POOL_WIDTH), jnp.float32)] * 4
    scratch += [pltpu.VMEM((S, CONV_PAD + T, FF_CHUNK), jnp.float32),
                pltpu.VMEM((TM, D_MODEL), jnp.bfloat16),
                pltpu.VMEM((TM, D_MODEL), jnp.bfloat16),
                pltpu.VMEM((TM, D_FF), jnp.bfloat16),
                pltpu.VMEM((TM, D_MODEL), jnp.float32),
                pltpu.VMEM((CONV_PAD, D_FF), jnp.float32)]

    return pl.pallas_call(
        functools.partial(_prompt_body, layer, T, is_last, n_cast, n_alias, len(zero_fill), seq_steps, n_tiles),
        grid=grid,
        in_specs=in_specs,
        out_specs=out_specs,
        out_shape=out_shape,
        scratch_shapes=scratch,
        input_output_aliases=aliases,
        compiler_params=pltpu.CompilerParams(dimension_semantics=("arbitrary",),
                                             vmem_limit_bytes=VMEM_LIMIT_BYTES),
        name=f"prompt_layer{layer}",
    )(*operands)


def _sample_body(layer, T, seq_major_in, seq_major_out, is_last, x_ref, spool_ref, sconv_ref, wsp_ref, bsp_ref,
                 n1_ref, win_ref, pw_ref, pscale_ref, vng_ref, wout_ref, n2_ref, wg_ref, wu_ref, cw_ref, cb_ref,
                 wd_ref, fn_ref, pool_in_all, conv_in_all, v_in_all, y_ref, pool_out_ref, conv_out_ref, v_out_ref,
                 gbuf, mixbuf, h2buf, actbuf):
    del pool_in_all, conv_in_all, v_in_all
    S = SAMPLE_SEQS
    TM = T * S
    n1_ref, pscale_ref, vng_ref, n2_ref, cb_ref = (
        r.at[pl.ds(layer, 1)] for r in (n1_ref, pscale_ref, vng_ref, n2_ref, cb_ref))
    slab = lambda t: slice(t * S, (t + 1) * S)

    if seq_major_in:
        x = jnp.concatenate([x_ref[:, t, :] for t in range(T)], axis=0)
    else:
        x = x_ref[...].reshape(TM, D_MODEL)
    h = _rms(x, n1_ref[...]).astype(jnp.bfloat16)
    p = _bdot(h, win_ref[:, 0:POOL_WIDTH])
    u = _gelu(_bdot(h, win_ref[:, POOL_WIDTH:POOL_WIDTH + SG_WIDTH]))
    v = _rms(_gelu(_bdot(h, win_ref[:, POOL_WIDTH + SG_WIDTH:IN_WIDTH])), vng_ref[...])
    v_out_ref[...] = v.reshape(T, S, SG_WIDTH)

    pool_out_ref[0:POOL_BUF - T] = spool_ref[T:POOL_BUF]
    pool_out_ref[POOL_BUF - T:POOL_BUF] = p.reshape(T, S, POOL_WIDTH)
    G = POOL_GROUP_DIM
    for gi, w in enumerate(POOL_WINDOWS):
        gs = slice(gi * G, (gi + 1) * G)

        def row(j):
            return spool_ref[j, :, gs] if j < POOL_BUF else p[slab(j - POOL_BUF), gs]

        for t in range(T):
            acc = row(POOL_BUF + t)
            for k in range(1, w):
                acc = acc + row(POOL_BUF + t - k)
            d = acc * (1.0 / w) - p[slab(t), gs]
            h2buf[slab(t), gs] = d.astype(jnp.bfloat16)
    half = POOL_WIDTH // 2
    for k in range(2):
        ks = slice(k * half, (k + 1) * half)
        a = _bdot(h2buf[:, ks], pw_ref[k]) * pscale_ref[:, ks]
        mixbuf[:, ks] = a.astype(jnp.bfloat16)

    for hd in range(SG_HEADS):
        hs = slice(hd * SG_HEAD_DIM, (hd + 1) * SG_HEAD_DIM)
        for t in range(T):
            mixed = jnp.full((S, SG_HEAD_DIM), bsp_ref[layer, hd * T + t], jnp.float32)
            for t2 in range(t + 1):
                mixed = mixed + wsp_ref[layer, (hd * T + t) * T + t2] * v[slab(t2), hs]
            mixbuf[slab(t), POOL_WIDTH + hd * SG_HEAD_DIM:POOL_WIDTH + (hd + 1) * SG_HEAD_DIM] = (
                (u[slab(t), hs] * mixed).astype(jnp.bfloat16))

    x1 = x + _bdot(mixbuf[...], wout_ref[...])
    h2buf[...] = _rms(x1, n2_ref[...]).astype(jnp.bfloat16)

    hist = CONV_BUF * S
    for c in range(D_FF // FF_CHUNK):
        cs = slice(c * FF_CHUNK, (c + 1) * FF_CHUNK)
        g = _bdot(h2buf[...], wg_ref[:, cs])
        up = _bdot(h2buf[...], wu_ref[:, cs])
        gbuf[0:hist, :] = sconv_ref[:, :, cs].reshape(hist, FF_CHUNK)
        gbuf[hist:hist + TM, :] = g
        conv_out_ref[:, :, cs] = g[TM - hist:TM, :].reshape(CONV_BUF, S, FF_CHUNK)
        conv = (cb_ref[:, cs]
                + cw_ref[0:1, cs] * gbuf[0:TM, :]
                + cw_ref[1:2, cs] * gbuf[S:S + TM, :]
                + cw_ref[2:3, cs] * g)
        actbuf[:, cs] = (_gelu(conv) * up).astype(jnp.bfloat16)

    out = x1 + _bdot(actbuf[...], wd_ref[...])
    if is_last:
        out = _rms(out, fn_ref[...])
    if seq_major_out:
        for t in range(T):
            y_ref[:, t, :] = out[slab(t)]
    else:
        y_ref[...] = out.reshape(T, S, D_MODEL)


def _run_sample_layer(layer, depth, is_last, x, seq_major_in, seq_major_out, state_pool_t, state_conv_t, wsp, bsp,
                      small, mats, stacked):
    (n1, pscale, vng, n2, cw, cb, pw, fn) = small
    (win, wout, wg, wu, wd) = mats
    n_seq, T = (x.shape[0], x.shape[1]) if seq_major_in else (x.shape[1], x.shape[0])
    S = SAMPLE_SEQS
    TM = T * S
    f32 = jnp.float32
    smem = pl.BlockSpec(memory_space=pltpu.SMEM)
    seq_major_spec = pl.BlockSpec((S, T, D_MODEL), lambda i: (i, 0, 0))
    time_major_spec = pl.BlockSpec((T, S, D_MODEL), lambda i: (0, i, 0))
    in_specs = [
        seq_major_spec if seq_major_in else time_major_spec,
        pl.BlockSpec((None, POOL_BUF, S, POOL_WIDTH), lambda i: (layer, 0, i, 0)),
        pl.BlockSpec((None, CONV_BUF, S, D_FF), lambda i: (layer, 0, i, 0)),
        smem, smem,
        _const_spec((depth, D_MODEL)),
        _const_spec((D_MODEL, IN_WIDTH)),
        _layer_spec((2, POOL_WIDTH // 2, POOL_WIDTH // 2), layer),
        _const_spec((depth, POOL_WIDTH)),
        _const_spec((depth, SG_WIDTH)),
        _const_spec((D_MODEL, D_MODEL)),
        _const_spec((depth, D_MODEL)),
        _const_spec((D_MODEL, D_FF)),
        _const_spec((D_MODEL, D_FF)),
        _layer_spec((3, D_FF), layer),
        _const_spec((depth, D_FF)),
        _const_spec((D_FF, D_MODEL)),
        _const_spec((1, D_MODEL)),
    ]
    operands = [x, state_pool_t, state_conv_t, wsp, bsp, n1, win, pw, pscale, vng, wout, n2, wg, wu, cw, cb, wd, fn]
    aliases = {}
    for k, arr in enumerate(stacked):
        aliases[len(operands)] = 1 + k
        in_specs.append(pl.BlockSpec(memory_space=pl.ANY))
        operands.append(arr)
    out_shape = [jax.ShapeDtypeStruct((n_seq, T, D_MODEL) if seq_major_out else (T, n_seq, D_MODEL), f32),
                 jax.ShapeDtypeStruct((depth, POOL_BUF, n_seq, POOL_WIDTH), f32),
                 jax.ShapeDtypeStruct((depth, CONV_BUF, n_seq, D_FF), f32),
                 jax.ShapeDtypeStruct((depth, T, n_seq, SG_WIDTH), f32)]
    out_specs = [seq_major_spec if seq_major_out else time_major_spec,
                 pl.BlockSpec((None, POOL_BUF, S, POOL_WIDTH), lambda i: (layer, 0, i, 0)),
                 pl.BlockSpec((None, CONV_BUF, S, D_FF), lambda i: (layer, 0, i, 0)),
                 pl.BlockSpec((None, T, S, SG_WIDTH), lambda i: (layer, 0, i, 0))]
    scratch = [pltpu.VMEM(((CONV_BUF + T) * S, FF_CHUNK), f32),
               pltpu.VMEM((TM, D_MODEL), jnp.bfloat16),
               pltpu.VMEM((TM, D_MODEL), jnp.bfloat16),
               pltpu.VMEM((TM, D_FF), jnp.bfloat16)]
    return pl.pallas_call(
        functools.partial(_sample_body, layer, T, seq_major_in, seq_major_out, is_last),
        grid=(n_seq // S,),
        in_specs=in_specs,
        out_specs=out_specs,
        out_shape=out_shape,
        scratch_shapes=scratch,
        input_output_aliases=aliases,
        compiler_params=pltpu.CompilerParams(dimension_semantics=("arbitrary",),
                                             vmem_limit_bytes=VMEM_LIMIT_BYTES),
        name=f"sample_layer{layer}",
    )(*operands)


def _inv_counts(n_rows):
    win = np.repeat(np.asarray(POOL_WINDOWS, np.float32), POOL_GROUP_DIM)[None, :]
    pos = np.arange(n_rows, dtype=np.float32)[:, None]
    first = 1.0 / np.minimum(pos + 1.0, win)
    later = np.broadcast_to(1.0 / win, (n_rows, POOL_WIDTH))
    return jnp.asarray(np.stack([first, later]).astype(np.float32))


def _prep_body(pool_w_ref, wsp_ref, bt_ref, pw_ref, spp_ref, bfp_ref):
    bf = jnp.bfloat16
    G = POOL_GROUP_DIM
    for k in range(2):
        for a in range(2):
            for b in range(2):
                blk = pool_w_ref[2 * k + a].astype(bf) if a == b else jnp.zeros((G, G), bf)
                pw_ref[k, a * G:(a + 1) * G, b * G:(b + 1) * G] = blk
    row = lax.broadcasted_iota(jnp.int32, (CHUNK, CHUNK), 0)
    col = lax.broadcasted_iota(jnp.int32, (CHUNK, CHUNK), 1)
    causal = row >= col
    for h in range(SG_HEADS):
        spp_ref[h] = jnp.where(causal, wsp_ref[h], 0.0).astype(bf)
        bfp_ref[:, h * SG_HEAD_DIM:(h + 1) * SG_HEAD_DIM] = jnp.broadcast_to(
            bt_ref[:, h:h + 1], (CHUNK, SG_HEAD_DIM))


def _prep_operands(pool_w, w_spatial, b_spatial):
    depth = pool_w.shape[0]
    bf, f32 = jnp.bfloat16, jnp.float32
    per_layer = lambda *shape: pl.BlockSpec((None,) + shape, lambda l: (l,) + (0,) * len(shape))
    return pl.pallas_call(
        _prep_body,
        grid=(depth,),
        in_specs=[per_layer(len(POOL_WINDOWS), POOL_GROUP_DIM, POOL_GROUP_DIM),
                  per_layer(SG_HEADS, CHUNK, CHUNK),
                  per_layer(CHUNK, SG_HEADS)],
        out_specs=[per_layer(2, POOL_WIDTH // 2, POOL_WIDTH // 2), per_layer(SG_HEADS, CHUNK, CHUNK),
                   per_layer(CHUNK, SG_WIDTH)],
        out_shape=[jax.ShapeDtypeStruct((depth, 2, POOL_WIDTH // 2, POOL_WIDTH // 2), bf),
                   jax.ShapeDtypeStruct((depth, SG_HEADS, CHUNK, CHUNK), bf),
                   jax.ShapeDtypeStruct((depth, CHUNK, SG_WIDTH), f32)],
        compiler_params=pltpu.CompilerParams(dimension_semantics=("arbitrary",)),
        name="prep_operands",
    )(pool_w, w_spatial, jnp.swapaxes(b_spatial, 1, 2))


def kernel(x_prompt, x_sample, state_pool, state_conv, norm1_g, w_in, pool_w, pool_scale, v_norm_g, w_spatial, b_spatial, w_out, norm2_g, w_gate, w_up, conv_w, conv_b, w_down, final_norm_g):
    depth = w_in.shape[0]
    bf = jnp.bfloat16
    pw, sp_p, bias_p = _prep_operands(pool_w, w_spatial, b_spatial)
    small = (norm1_g, pool_scale, v_norm_g, norm2_g, conv_w, conv_b, pw, final_norm_g[None, :])
    big_f32 = (w_in, w_out, w_gate, w_up, w_down)
    invcnt = _inv_counts(TM_PROMPT)

    n_dec, t_dec, _ = x_sample.shape
    mats = tuple(w[0].astype(bf) for w in big_f32)
    state_pool_t = jnp.transpose(state_pool, (0, 2, 1, 3))
    state_conv_t = jnp.transpose(state_conv, (0, 2, 1, 3))
    wsp = w_spatial[:, :, :t_dec, :t_dec].reshape(depth, SG_HEADS * t_dec * t_dec)
    bsp = b_spatial[:, :, :t_dec].reshape(depth, SG_HEADS * t_dec)
    n_prompt = x_prompt.shape[0]
    f32 = jnp.float32
    stacked_p = (jnp.zeros((depth, n_prompt, POOL_BUF, POOL_WIDTH), f32),
                 jnp.zeros((depth, n_prompt, CONV_BUF, D_FF), f32))
    sample_stacked_shapes = ((depth, POOL_BUF, n_dec, POOL_WIDTH), (depth, CONV_BUF, n_dec, D_FF),
                             (depth, t_dec, n_dec, SG_WIDTH))
    stacked_s = None
    xp, xs = x_prompt, x_sample
    for i in range(depth):
        last = i == depth - 1
        res = _run_prompt_layer(i, depth, last, xp, invcnt, small, mats, (sp_p, bias_p),
                                None if last else big_f32, stacked_p,
                                zero_fill=sample_stacked_shapes if i == 0 else ())
        xp, stacked_p = res[0], res[1:3]
        if i == 0:
            stacked_s = tuple(res[len(res) - len(sample_stacked_shapes):])
        next_mats = tuple(res[3:3 + len(big_f32)])
        res_s = _run_sample_layer(i, depth, last, xs, i == 0, last, state_pool_t, state_conv_t, wsp, bsp,
                                  small, mats, stacked_s)
        xs, stacked_s = res_s[0], res_s[1:4]
        mats = next_mats
    pool_p, conv_p = stacked_p
    pool_s, conv_s, v_s = (jnp.transpose(a, (0, 2, 1, 3)) for a in stacked_s)
    return (xp, xs, pool_p, pool_s, conv_p, conv_s, v_s)
```

```python
import functools

import numpy as np
import jax
import jax.numpy as jnp
from jax import lax
from jax.experimental import pallas as pl
from jax.experimental.pallas import tpu as pltpu

D_MODEL = 1024
POOL_WIDTH = 512
POOL_WINDOWS = (2, 4, 8, 16)
POOL_GROUP_DIM = 128
POOL_BUF = 15
POOL_PAD = 16
SG_WIDTH = 512
SG_HEADS = 4
SG_HEAD_DIM = 128
CHUNK = 128
IN_WIDTH = POOL_WIDTH + 2 * SG_WIDTH
D_FF = 2816
CONV_BUF = 2
CONV_PAD = 8
EPS = 1e-6

TM_PROMPT = 512
SAMPLE_SEQS = 64
FF_CHUNK = 256
TAIL_PARTS = 4
TAIL_COLS = D_MODEL // TAIL_PARTS
DOWN_CAST_ROWS = 176
VMEM_LIMIT_BYTES = 58 * 1024 * 1024

_INV_SQRT2 = 0.7071067811865476


def _gelu(x):
    return 0.5 * x * (1.0 + lax.erf(x * _INV_SQRT2))


def _rms(x, g):
    ms = jnp.mean(x * x, axis=-1, keepdims=True)
    return x * lax.rsqrt(ms + EPS) * g


def _bdot(a, b):
    return jnp.dot(a, b, preferred_element_type=jnp.float32)


def _const_spec(shape):
    nd = len(shape)
    return pl.BlockSpec(shape, lambda *_: (0,) * nd)


def _layer_spec(shape, layer):
    nd = len(shape)
    return pl.BlockSpec((None,) + tuple(shape), lambda *_: (layer,) + (0,) * nd)


def _prompt_body(layer, T, is_last, n_cast, n_alias, n_zero, seq_steps, n_tiles, *refs):
    S = 1
    refs = list(refs)
    x_ref = refs.pop(0)
    (invcnt_ref, n1_ref, win_ref, pw_ref, pscale_ref, vng_ref, sp_ref, bfull_ref, wout_ref, n2_ref,
     wg_ref, wu_ref, cw_ref, cb_ref, wd_ref, fn_ref) = refs[:16]
    refs = refs[16:]
    n1_ref, pscale_ref, vng_ref, n2_ref, cb_ref = (
        r.at[pl.ds(layer, 1)] for r in (n1_ref, pscale_ref, vng_ref, n2_ref, cb_ref))
    cast_in, refs = refs[:n_cast], refs[n_cast:]
    refs = refs[n_alias:]
    y_ref, pool_out_ref, conv_out_ref = refs[:3]
    refs = refs[3:]
    cast_out, refs = refs[:n_cast], refs[n_cast:]
    zero_out, refs = refs[:n_zero], refs[n_zero:]

    for dst in zero_out:
        dst[...] = jnp.zeros(dst.shape, dst.dtype)
    pbuf, st8, st4, st2, gbuf, mixbuf, h2buf, actbuf, x1buf = refs[:9]

    for src, dst in zip(cast_in, cast_out):
        dst[...] = src[...].astype(jnp.bfloat16)

    def tail_begin():
        y_ref[...] = x1buf[...]

    def tail_part(k):
        cols = slice(k * TAIL_COLS, (k + 1) * TAIL_COLS)
        y_ref[:, cols] += _bdot(actbuf[...], wd_ref[:, cols])

    def tail_end():
        if is_last:
            y_ref[...] = _rms(y_ref[...], fn_ref[...])

    def tail_all():
        tail_begin()
        for k in range(TAIL_PARTS):
            tail_part(k)
        tail_end()

    front = functools.partial(
        _prompt_front, S, T, x_ref, invcnt_ref, n1_ref, win_ref, pw_ref, pscale_ref, vng_ref, sp_ref,
        bfull_ref, wout_ref, n2_ref, wg_ref, wu_ref, cw_ref, cb_ref, pool_out_ref, conv_out_ref,
        pbuf, st8, st4, st2, gbuf, mixbuf, h2buf, actbuf, x1buf)

    s = pl.program_id(0)
    gcarry = refs[9]
    j = lax.rem(s, seq_steps)

    @pl.when(s == 0)
    def _():
        x1buf[...] = jnp.zeros(x1buf.shape, x1buf.dtype)
        actbuf[...] = jnp.zeros(actbuf.shape, actbuf.dtype)

    @pl.when(j == 0)
    def _():
        pbuf[:, 0:POOL_PAD, :] = jnp.zeros((S, POOL_PAD, POOL_WIDTH), jnp.float32)
        gcarry[...] = jnp.zeros((CONV_PAD, D_FF), jnp.float32)

    @pl.when(j > 0)
    def _():
        pbuf[:, 0:POOL_PAD, :] = pbuf[:, T:T + POOL_PAD, :]

    @pl.when(s < n_tiles)
    def _():
        tail_begin()
        front(gcarry, tail_part)
        tail_end()

    @pl.when(s == n_tiles)
    def _():
        tail_all()


def _prompt_front(S, T, x_ref, invcnt_ref, n1_ref, win_ref, pw_ref, pscale_ref, vng_ref, sp_ref,
                  bfull_ref, wout_ref, n2_ref, wg_ref, wu_ref, cw_ref, cb_ref, pool_out_ref, conv_out_ref,
                  pbuf, st8, st4, st2, gbuf, mixbuf, h2buf, actbuf, x1buf, gcarry, fill):
    TM = S * T
    R = POOL_PAD + T
    x = x_ref[...]
    fill(0)
    h = _rms(x, n1_ref[...]).astype(jnp.bfloat16)

    p = _bdot(h, win_ref[:, 0:POOL_WIDTH])
    u = _gelu(_bdot(h, win_ref[:, POOL_WIDTH:POOL_WIDTH + SG_WIDTH]))
    v = _rms(_gelu(_bdot(h, win_ref[:, POOL_WIDTH + SG_WIDTH:IN_WIDTH])), vng_ref[...])
    fill(1)
    pbuf[:, POOL_PAD:R, :] = p.reshape(S, T, POOL_WIDTH)
    pool_out_ref[...] = pbuf[:, R - POOL_BUF:R, :]

    zeros8 = jnp.zeros((S, 8, POOL_WIDTH), jnp.float32)
    st8[:, 0:8, :] = zeros8
    st4[:, 0:8, :] = zeros8
    st2[:, 0:8, :] = zeros8
    G = POOL_GROUP_DIM
    st8[:, 8:R, 3 * G:4 * G] = pbuf[:, 8:R, 3 * G:4 * G] + pbuf[:, 0:R - 8, 3 * G:4 * G]
    st4[:, 8:R, 2 * G:3 * G] = pbuf[:, 8:R, 2 * G:3 * G] + pbuf[:, 4:R - 4, 2 * G:3 * G]
    st4[:, 8:R, 3 * G:4 * G] = st8[:, 8:R, 3 * G:4 * G] + st8[:, 4:R - 4, 3 * G:4 * G]
    st2[:, 8:R, 1 * G:2 * G] = pbuf[:, 8:R, 1 * G:2 * G] + pbuf[:, 6:R - 2, 1 * G:2 * G]
    st2[:, 8:R, 2 * G:4 * G] = st4[:, 8:R, 2 * G:4 * G] + st4[:, 6:R - 2, 2 * G:4 * G]
    sum0 = pbuf[:, POOL_PAD:R, 0:G] + pbuf[:, POOL_PAD - 1:R - 1, 0:G]
    sum123 = st2[:, POOL_PAD:R, G:4 * G] + st2[:, POOL_PAD - 1:R - 1, G:4 * G]
    wsum = jnp.concatenate([sum0, sum123], axis=-1).reshape(TM, POOL_WIDTH)
    dpool = wsum * invcnt_ref[...] - p
    d16 = dpool.astype(jnp.bfloat16)
    half = POOL_WIDTH // 2
    for k in range(2):
        ks = slice(k * half, (k + 1) * half)
        a = _bdot(d16[:, ks], pw_ref[k]) * pscale_ref[:, ks]
        mixbuf[:, ks] = a.astype(jnp.bfloat16)

    fill(2)

    v16 = v.astype(jnp.bfloat16)
    for hd in range(SG_HEADS):
        sl = slice(hd * SG_HEAD_DIM, (hd + 1) * SG_HEAD_DIM)
        osl = slice(POOL_WIDTH + hd * SG_HEAD_DIM, POOL_WIDTH + (hd + 1) * SG_HEAD_DIM)
        chunks = [slice(c * CHUNK, (c + 1) * CHUNK) for c in range(TM // CHUNK)]
        mixed_all = _bdot(sp_ref[hd], jnp.concatenate([v16[rs, sl] for rs in chunks], axis=1))
        for c, rs in enumerate(chunks):
            mixed = mixed_all[:, c * SG_HEAD_DIM:(c + 1) * SG_HEAD_DIM] + bfull_ref[:, sl]
            mixbuf[rs, osl] = (u[rs, sl] * mixed).astype(jnp.bfloat16)

    x1 = x + _bdot(mixbuf[...], wout_ref[...])
    x1buf[...] = x1
    fill(3)
    h2buf[...] = _rms(x1, n2_ref[...]).astype(jnp.bfloat16)

    for c in range(D_FF // FF_CHUNK):
        cs = slice(c * FF_CHUNK, (c + 1) * FF_CHUNK)
        g = _bdot(h2buf[...], wg_ref[:, cs])
        up = _bdot(h2buf[...], wu_ref[:, cs])
        g3 = g.reshape(S, T, FF_CHUNK)
        gbuf[0, 0:CONV_PAD, :] = gcarry[:, cs]
        gcarry[:, cs] = g3[0, T - CONV_PAD:T, :]
        gbuf[:, CONV_PAD:CONV_PAD + T, :] = g3
        conv_out_ref[:, :, cs] = g3[:, T - CONV_BUF:T, :]
        conv = (cb_ref[:, cs]
                + cw_ref[0:1, cs] * gbuf[:, CONV_PAD - 2:CONV_PAD - 2 + T, :]
                + cw_ref[1:2, cs] * gbuf[:, CONV_PAD - 1:CONV_PAD - 1 + T, :]
                + cw_ref[2:3, cs] * g3)
        act = _gelu(conv).reshape(TM, FF_CHUNK) * up
        actbuf[:, cs] = act.astype(jnp.bfloat16)


def _run_prompt_layer(layer, depth, is_last, x, invcnt, small, mats, spatial, next_f32, stacked, zero_fill=()):
    (n1, pscale, vng, n2, cw, cb, pw, fn) = small
    (win, wout, wg, wu, wd) = mats
    sp, bfull = spatial
    TM = TM_PROMPT
    f32 = jnp.float32
    B, L, _ = x.shape
    S, T = 1, TM
    seq_steps = steps = L // TM
    n_tiles = B * steps
    grid = (n_tiles + 1,)
    cur = lambda s: jnp.minimum(s, n_tiles - 1)
    prev = lambda s: jnp.maximum(s - 1, 0)
    in_specs = [pl.BlockSpec((None, TM, D_MODEL), lambda s: (cur(s) // steps, cur(s) % steps, 0)),
                pl.BlockSpec((None, TM, POOL_WIDTH), lambda s: (jnp.minimum(cur(s) % steps, 1), 0, 0))]
    operands = [x, invcnt]
    out_shape = [jax.ShapeDtypeStruct((B, L, D_MODEL), f32),
                 jax.ShapeDtypeStruct((depth, B, POOL_BUF, POOL_WIDTH), f32),
                 jax.ShapeDtypeStruct((depth, B, CONV_BUF, D_FF), f32)]
    out_specs = [pl.BlockSpec((None, TM, D_MODEL), lambda s: (prev(s) // steps, prev(s) % steps, 0)),
                 pl.BlockSpec((None, 1, POOL_BUF, POOL_WIDTH), lambda s: (layer, cur(s) // steps, 0, 0)),
                 pl.BlockSpec((None, 1, CONV_BUF, D_FF), lambda s: (layer, cur(s) // steps, 0, 0))]

    in_specs += [
        _const_spec((depth, D_MODEL)),
        _const_spec((D_MODEL, IN_WIDTH)),
        _layer_spec((2, POOL_WIDTH // 2, POOL_WIDTH // 2), layer),
        _const_spec((depth, POOL_WIDTH)),
        _const_spec((depth, SG_WIDTH)),
        _layer_spec(sp.shape[1:], layer),
        _layer_spec(bfull.shape[1:], layer),
        _const_spec((D_MODEL, D_MODEL)),
        _const_spec((depth, D_MODEL)),
        _const_spec((D_MODEL, D_FF)),
        _const_spec((D_MODEL, D_FF)),
        _layer_spec((3, D_FF), layer),
        _const_spec((depth, D_FF)),
        _const_spec((D_FF, D_MODEL)),
        _const_spec((1, D_MODEL)),
    ]
    operands += [n1, win, pw, pscale, vng, sp, bfull, wout, n2, wg, wu, cw, cb, wd, fn]

    n_cast = 0
    if next_f32 is not None:
        for w in next_f32:
            rows, share = ((D_MODEL // n_tiles, 1) if w.shape[1] == D_MODEL
                           else (DOWN_CAST_ROWS, n_tiles * DOWN_CAST_ROWS // D_FF))
            cols = w.shape[2]
            in_specs.append(pl.BlockSpec((None, rows, cols),
                                         lambda s, share=share: (layer + 1, cur(s) // share, 0)))
            out_specs.append(pl.BlockSpec((rows, cols), lambda s, share=share: (cur(s) // share, 0)))
            out_shape.append(jax.ShapeDtypeStruct(w.shape[1:], jnp.bfloat16))
            operands.append(w)
            n_cast += 1

    for shape in zero_fill:
        assert n_tiles % shape[0] == 0 and len(shape) == 4
        per_layer = n_tiles // shape[0]
        blk = (1, shape[1], shape[2] // per_layer, shape[3])
        assert blk[2] * per_layer == shape[2]
        out_specs.append(pl.BlockSpec(
            blk, lambda s, per_layer=per_layer: (cur(s) // per_layer, 0, cur(s) % per_layer, 0)))
        out_shape.append(jax.ShapeDtypeStruct(shape, f32))

    aliases = {}
    for k, arr in enumerate(stacked):
        aliases[len(operands)] = 1 + k
        in_specs.append(pl.BlockSpec(memory_space=pl.ANY))
        operands.append(arr)
    n_alias = len(stacked)

    R = POOL_PAD + T
    scratch = [pltpu.VMEM((S, R, POOL_WIDTH), jnp.float32)] * 4
    scratch += [pltpu.VMEM((S, CONV_PAD + T, FF_CHUNK), jnp.float32),
                pltpu.VMEM((TM, D_MODEL), jnp.bfloat16),
                pltpu.VMEM((TM, D_MODEL), jnp.bfloat16),
                pltpu.VMEM((TM, D_FF), jnp.bfloat16),
                pltpu.VMEM((TM, D_MODEL), jnp.float32),
                pltpu.VMEM((CONV_PAD, D_FF), jnp.float32)]

    return pl.pallas_call(
        functools.partial(_prompt_body, layer, T, is_last, n_cast, n_alias, len(zero_fill), seq_steps, n_tiles),
        grid=grid,
        in_specs=in_specs,
        out_specs=out_specs,
        out_shape=out_shape,
        scratch_shapes=scratch,
        input_output_aliases=aliases,
        compiler_params=pltpu.CompilerParams(dimension_semantics=("arbitrary",),
                                             vmem_limit_bytes=VMEM_LIMIT_BYTES),
        name=f"prompt_layer{layer}",
    )(*operands)


def _sample_body(layer, T, seq_major_in, seq_major_out, is_last, x_ref, spool_ref, sconv_ref, wsp_ref, bsp_ref,
                 n1_ref, win_ref, pw_ref, pscale_ref, vng_ref, wout_ref, n2_ref, wg_ref, wu_ref, cw_ref, cb_ref,
                 wd_ref, fn_ref, pool_in_all, conv_in_all, v_in_all, y_ref, pool_out_ref, conv_out_ref, v_out_ref,
                 gbuf, mixbuf, h2buf, actbuf, wg_v, wu_v, wd_v, wsem):
    del pool_in_all, conv_in_all, v_in_all
    S = SAMPLE_SEQS
    TM = T * S

    ffn_weight_copies = [pltpu.make_async_copy(src, dst, wsem.at[i]) for i, (src, dst) in enumerate(
        ((wg_ref, wg_v), (wu_ref, wu_v), (wd_ref, wd_v)))]
    wg_ref, wu_ref, wd_ref = wg_v, wu_v, wd_v
    first_step = pl.program_id(0) == 0

    @pl.when(first_step)
    def _():
        for cp in ffn_weight_copies:
            cp.start()

    n1_ref, pscale_ref, vng_ref, n2_ref, cb_ref = (
        r.at[pl.ds(layer, 1)] for r in (n1_ref, pscale_ref, vng_ref, n2_ref, cb_ref))
    slab = lambda t: slice(t * S, (t + 1) * S)

    if seq_major_in:
        x = jnp.concatenate([x_ref[:, t, :] for t in range(T)], axis=0)
    else:
        x = x_ref[...].reshape(TM, D_MODEL)
    h = _rms(x, n1_ref[...]).astype(jnp.bfloat16)
    p = _bdot(h, win_ref[:, 0:POOL_WIDTH])
    u = _gelu(_bdot(h, win_ref[:, POOL_WIDTH:POOL_WIDTH + SG_WIDTH]))
    v = _rms(_gelu(_bdot(h, win_ref[:, POOL_WIDTH + SG_WIDTH:IN_WIDTH])), vng_ref[...])
    v_out_ref[...] = v.reshape(T, S, SG_WIDTH)

    pool_out_ref[0:POOL_BUF - T] = spool_ref[T:POOL_BUF]
    pool_out_ref[POOL_BUF - T:POOL_BUF] = p.reshape(T, S, POOL_WIDTH)
    G = POOL_GROUP_DIM
    for gi, w in enumerate(POOL_WINDOWS):
        gs = slice(gi * G, (gi + 1) * G)

        def row(j):
            return spool_ref[j, :, gs] if j < POOL_BUF else p[slab(j - POOL_BUF), gs]

        for t in range(T):
            acc = row(POOL_BUF + t)
            for k in range(1, w):
                acc = acc + row(POOL_BUF + t - k)
            d = acc * (1.0 / w) - p[slab(t), gs]
            h2buf[slab(t), gs] = d.astype(jnp.bfloat16)
    half = POOL_WIDTH // 2
    for k in range(2):
        ks = slice(k * half, (k + 1) * half)
        a = _bdot(h2buf[:, ks], pw_ref[k]) * pscale_ref[:, ks]
        mixbuf[:, ks] = a.astype(jnp.bfloat16)

    for hd in range(SG_HEADS):
        hs = slice(hd * SG_HEAD_DIM, (hd + 1) * SG_HEAD_DIM)
        for t in range(T):
            mixed = jnp.full((S, SG_HEAD_DIM), bsp_ref[layer, hd * T + t], jnp.float32)
            for t2 in range(t + 1):
                mixed = mixed + wsp_ref[layer, (hd * T + t) * T + t2] * v[slab(t2), hs]
            mixbuf[slab(t), POOL_WIDTH + hd * SG_HEAD_DIM:POOL_WIDTH + (hd + 1) * SG_HEAD_DIM] = (
                (u[slab(t), hs] * mixed).astype(jnp.bfloat16))

    x1 = x + _bdot(mixbuf[...], wout_ref[...])
    h2buf[...] = _rms(x1, n2_ref[...]).astype(jnp.bfloat16)

    @pl.when(first_step)
    def _():
        ffn_weight_copies[0].wait()
        ffn_weight_copies[1].wait()

    hist = CONV_BUF * S
    for c in range(D_FF // FF_CHUNK):
        cs = slice(c * FF_CHUNK, (c + 1) * FF_CHUNK)
        g = _bdot(h2buf[...], wg_ref[:, cs])
        up = _bdot(h2buf[...], wu_ref[:, cs])
        gbuf[0:hist, :] = sconv_ref[:, :, cs].reshape(hist, FF_CHUNK)
        gbuf[hist:hist + TM, :] = g
        conv_out_ref[:, :, cs] = g[TM - hist:TM, :].reshape(CONV_BUF, S, FF_CHUNK)
        conv = (cb_ref[:, cs]
                + cw_ref[0:1, cs] * gbuf[0:TM, :]
                + cw_ref[1:2, cs] * gbuf[S:S + TM, :]
                + cw_ref[2:3, cs] * g)
        actbuf[:, cs] = (_gelu(conv) * up).astype(jnp.bfloat16)

    @pl.when(first_step)
    def _():
        ffn_weight_copies[2].wait()

    out = x1 + _bdot(actbuf[...], wd_ref[...])
    if is_last:
        out = _rms(out, fn_ref[...])
    if seq_major_out:
        for t in range(T):
            y_ref[:, t, :] = out[slab(t)]
    else:
        y_ref[...] = out.reshape(T, S, D_MODEL)


def _run_sample_layer(layer, depth, is_last, x, seq_major_in, seq_major_out, state_pool_t, state_conv_t, wsp, bsp,
                      small, mats, stacked):
    (n1, pscale, vng, n2, cw, cb, pw, fn) = small
    (win, wout, wg, wu, wd) = mats
    n_seq, T = (x.shape[0], x.shape[1]) if seq_major_in else (x.shape[1], x.shape[0])
    S = SAMPLE_SEQS
    TM = T * S
    f32 = jnp.float32
    smem = pl.BlockSpec(memory_space=pltpu.SMEM)
    seq_major_spec = pl.BlockSpec((S, T, D_MODEL), lambda i: (i, 0, 0))
    time_major_spec = pl.BlockSpec((T, S, D_MODEL), lambda i: (0, i, 0))
    in_specs = [
        seq_major_spec if seq_major_in else time_major_spec,
        pl.BlockSpec((None, POOL_BUF, S, POOL_WIDTH), lambda i: (layer, 0, i, 0)),
        pl.BlockSpec((None, CONV_BUF, S, D_FF), lambda i: (layer, 0, i, 0)),
        smem, smem,
        _const_spec((depth, D_MODEL)),
        _const_spec((D_MODEL, IN_WIDTH)),
        _layer_spec((2, POOL_WIDTH // 2, POOL_WIDTH // 2), layer),
        _const_spec((depth, POOL_WIDTH)),
        _const_spec((depth, SG_WIDTH)),
        _const_spec((D_MODEL, D_MODEL)),
        _const_spec((depth, D_MODEL)),
        pl.BlockSpec(memory_space=pl.ANY),
        pl.BlockSpec(memory_space=pl.ANY),
        _layer_spec((3, D_FF), layer),
        _const_spec((depth, D_FF)),
        pl.BlockSpec(memory_space=pl.ANY),
        _const_spec((1, D_MODEL)),
    ]
    operands = [x, state_pool_t, state_conv_t, wsp, bsp, n1, win, pw, pscale, vng, wout, n2, wg, wu, cw, cb, wd, fn]
    aliases = {}
    for k, arr in enumerate(stacked):
        aliases[len(operands)] = 1 + k
        in_specs.append(pl.BlockSpec(memory_space=pl.ANY))
        operands.append(arr)
    out_shape = [jax.ShapeDtypeStruct((n_seq, T, D_MODEL) if seq_major_out else (T, n_seq, D_MODEL), f32),
                 jax.ShapeDtypeStruct((depth, POOL_BUF, n_seq, POOL_WIDTH), f32),
                 jax.ShapeDtypeStruct((depth, CONV_BUF, n_seq, D_FF), f32),
                 jax.ShapeDtypeStruct((depth, T, n_seq, SG_WIDTH), f32)]
    out_specs = [seq_major_spec if seq_major_out else time_major_spec,
                 pl.BlockSpec((None, POOL_BUF, S, POOL_WIDTH), lambda i: (layer, 0, i, 0)),
                 pl.BlockSpec((None, CONV_BUF, S, D_FF), lambda i: (layer, 0, i, 0)),
                 pl.BlockSpec((None, T, S, SG_WIDTH), lambda i: (layer, 0, i, 0))]
    scratch = [pltpu.VMEM(((CONV_BUF + T) * S, FF_CHUNK), f32),
               pltpu.VMEM((TM, D_MODEL), jnp.bfloat16),
               pltpu.VMEM((TM, D_MODEL), jnp.bfloat16),
               pltpu.VMEM((TM, D_FF), jnp.bfloat16),
               pltpu.VMEM((D_MODEL, D_FF), jnp.bfloat16),
               pltpu.VMEM((D_MODEL, D_FF), jnp.bfloat16),
               pltpu.VMEM((D_FF, D_MODEL), jnp.bfloat16),
               pltpu.SemaphoreType.DMA((3,))]
    return pl.pallas_call(
        functools.partial(_sample_body, layer, T, seq_major_in, seq_major_out, is_last),
        grid=(n_seq // S,),
        in_specs=in_specs,
        out_specs=out_specs,
        out_shape=out_shape,
        scratch_shapes=scratch,
        input_output_aliases=aliases,
        compiler_params=pltpu.CompilerParams(dimension_semantics=("arbitrary",),
                                             vmem_limit_bytes=VMEM_LIMIT_BYTES),
        name=f"sample_layer{layer}",
    )(*operands)


def _inv_counts(n_rows):
    win = np.repeat(np.asarray(POOL_WINDOWS, np.float32), POOL_GROUP_DIM)[None, :]
    pos = np.arange(n_rows, dtype=np.float32)[:, None]
    first = 1.0 / np.minimum(pos + 1.0, win)
    later = np.broadcast_to(1.0 / win, (n_rows, POOL_WIDTH))
    return jnp.asarray(np.stack([first, later]).astype(np.float32))


def _prep_body(pool_w_ref, wsp_ref, bt_ref, pw_ref, spp_ref, bfp_ref):
    bf = jnp.bfloat16
    G = POOL_GROUP_DIM
    for k in range(2):
        for a in range(2):
            for b in range(2):
                blk = pool_w_ref[2 * k + a].astype(bf) if a == b else jnp.zeros((G, G), bf)
                pw_ref[k, a * G:(a + 1) * G, b * G:(b + 1) * G] = blk
    row = lax.broadcasted_iota(jnp.int32, (CHUNK, CHUNK), 0)
    col = lax.broadcasted_iota(jnp.int32, (CHUNK, CHUNK), 1)
    causal = row >= col
    for h in range(SG_HEADS):
        spp_ref[h] = jnp.where(causal, wsp_ref[h], 0.0).astype(bf)
        bfp_ref[:, h * SG_HEAD_DIM:(h + 1) * SG_HEAD_DIM] = jnp.broadcast_to(
            bt_ref[:, h:h + 1], (CHUNK, SG_HEAD_DIM))


def _prep_operands(pool_w, w_spatial, b_spatial):
    depth = pool_w.shape[0]
    bf, f32 = jnp.bfloat16, jnp.float32
    per_layer = lambda *shape: pl.BlockSpec((None,) + shape, lambda l: (l,) + (0,) * len(shape))
    return pl.pallas_call(
        _prep_body,
        grid=(depth,),
        in_specs=[per_layer(len(POOL_WINDOWS), POOL_GROUP_DIM, POOL_GROUP_DIM),
                  per_layer(SG_HEADS, CHUNK, CHUNK),
                  per_layer(CHUNK, SG_HEADS)],
        out_specs=[per_layer(2, POOL_WIDTH // 2, POOL_WIDTH // 2), per_layer(SG_HEADS, CHUNK, CHUNK),
                   per_layer(CHUNK, SG_WIDTH)],
        out_shape=[jax.ShapeDtypeStruct((depth, 2, POOL_WIDTH // 2, POOL_WIDTH // 2), bf),
                   jax.ShapeDtypeStruct((depth, SG_HEADS, CHUNK, CHUNK), bf),
                   jax.ShapeDtypeStruct((depth, CHUNK, SG_WIDTH), f32)],
        compiler_params=pltpu.CompilerParams(dimension_semantics=("arbitrary",)),
        name="prep_operands",
    )(pool_w, w_spatial, jnp.swapaxes(b_spatial, 1, 2))


def kernel(x_prompt, x_sample, state_pool, state_conv, norm1_g, w_in, pool_w, pool_scale, v_norm_g, w_spatial, b_spatial, w_out, norm2_g, w_gate, w_up, conv_w, conv_b, w_down, final_norm_g):
    depth = w_in.shape[0]
    bf = jnp.bfloat16
    pw, sp_p, bias_p = _prep_operands(pool_w, w_spatial, b_spatial)
    small = (norm1_g, pool_scale, v_norm_g, norm2_g, conv_w, conv_b, pw, final_norm_g[None, :])
    big_f32 = (w_in, w_out, w_gate, w_up, w_down)
    invcnt = _inv_counts(TM_PROMPT)

    n_dec, t_dec, _ = x_sample.shape
    mats = tuple(w[0].astype(bf) for w in big_f32)
    state_pool_t = jnp.transpose(state_pool, (0, 2, 1, 3))
    state_conv_t = jnp.transpose(state_conv, (0, 2, 1, 3))
    wsp = w_spatial[:, :, :t_dec, :t_dec].reshape(depth, SG_HEADS * t_dec * t_dec)
    bsp = b_spatial[:, :, :t_dec].reshape(depth, SG_HEADS * t_dec)
    n_prompt = x_prompt.shape[0]
    f32 = jnp.float32
    stacked_p = (jnp.zeros((depth, n_prompt, POOL_BUF, POOL_WIDTH), f32),
                 jnp.zeros((depth, n_prompt, CONV_BUF, D_FF), f32))
    sample_stacked_shapes = ((depth, POOL_BUF, n_dec, POOL_WIDTH), (depth, CONV_BUF, n_dec, D_FF),
                             (depth, t_dec, n_dec, SG_WIDTH))
    stacked_s = None
    xp, xs = x_prompt, x_sample
    for i in range(depth):
        last = i == depth - 1
        res = _run_prompt_layer(i, depth, last, xp, invcnt, small, mats, (sp_p, bias_p),
                                None if last else big_f32, stacked_p,
                                zero_fill=sample_stacked_shapes if i == 0 else ())
        xp, stacked_p = res[0], res[1:3]
        if i == 0:
            stacked_s = tuple(res[len(res) - len(sample_stacked_shapes):])
        next_mats = tuple(res[3:3 + len(big_f32)])
        res_s = _run_sample_layer(i, depth, last, xs, i == 0, last, state_pool_t, state_conv_t, wsp, bsp,
                                  small, mats, stacked_s)
        xs, stacked_s = res_s[0], res_s[1:4]
        mats = next_mats
    pool_p, conv_p = stacked_p
    pool_s, conv_s, v_s = (jnp.transpose(a, (0, 2, 1, 3)) for a in stacked_s)
    return (xp, xs, pool_p, pool_s, conv_p, conv_s, v_s)
```

```python
import functools

import numpy as np
import jax
import jax.numpy as jnp
from jax import lax
from jax.experimental import pallas as pl
from jax.experimental.pallas import tpu as pltpu

D_MODEL = 1024
POOL_WIDTH = 512
POOL_WINDOWS = (2, 4, 8, 16)
POOL_GROUP_DIM = 128
POOL_BUF = 15
POOL_PAD = 16
SG_WIDTH = 512
SG_HEADS = 4
SG_HEAD_DIM = 128
CHUNK = 128
IN_WIDTH = POOL_WIDTH + 2 * SG_WIDTH
D_FF = 2816
CONV_BUF = 2
CONV_PAD = 8
EPS = 1e-6

TM_PROMPT = 512
SAMPLE_SEQS = 64
FF_CHUNK = 256
TAIL_PARTS = 4
TAIL_COLS = D_MODEL // TAIL_PARTS
DOWN_CAST_ROWS = 176
VMEM_LIMIT_BYTES = 58 * 1024 * 1024

_INV_SQRT2 = 0.7071067811865476


def _gelu(x):
    return 0.5 * x * (1.0 + lax.erf(x * _INV_SQRT2))


def _rms(x, g):
    ms = jnp.mean(x * x, axis=-1, keepdims=True)
    return x * lax.rsqrt(ms + EPS) * g


def _bdot(a, b):
    return jnp.dot(a, b, preferred_element_type=jnp.float32)


def _const_spec(shape):
    nd = len(shape)
    return pl.BlockSpec(shape, lambda *_: (0,) * nd)


def _layer_spec(shape, layer):
    nd = len(shape)
    return pl.BlockSpec((None,) + tuple(shape), lambda *_: (layer,) + (0,) * nd)


def _prompt_body(layer, T, is_last, n_cast, n_alias, n_zero, n_relay, seq_steps, n_tiles, *refs):
    S = 1
    refs = list(refs)
    x_ref = refs.pop(0)
    (invcnt_ref, n1_ref, win_ref, pw_ref, pscale_ref, vng_ref, sp_ref, bfull_ref, wout_ref, n2_ref,
     wg_ref, wu_ref, cw_ref, cb_ref, wd_ref, fn_ref) = refs[:16]
    refs = refs[16:]
    n1_ref, pscale_ref, vng_ref, n2_ref, cb_ref = (
        r.at[pl.ds(layer, 1)] for r in (n1_ref, pscale_ref, vng_ref, n2_ref, cb_ref))
    cast_in, refs = refs[:n_cast], refs[n_cast:]
    relay_in, refs = refs[:n_relay], refs[n_relay:]
    refs = refs[n_alias:]
    y_ref, pool_out_ref, conv_out_ref = refs[:3]
    refs = refs[3:]
    cast_out, refs = refs[:n_cast], refs[n_cast:]
    zero_out, refs = refs[:n_zero], refs[n_zero:]
    relay_out, refs = refs[:n_relay], refs[n_relay:]

    for dst in zero_out:
        dst[...] = jnp.zeros(dst.shape, dst.dtype)
    for src, dst in zip(relay_in, relay_out):
        for r in range(dst.shape[1]):
            dst[0, r] = src[0, :, r, :]
    pbuf, st8, st4, st2, gbuf, mixbuf, h2buf, actbuf, x1buf = refs[:9]

    for src, dst in zip(cast_in, cast_out):
        dst[...] = src[...].astype(jnp.bfloat16)

    def tail_begin():
        y_ref[...] = x1buf[...]

    def tail_part(k):
        cols = slice(k * TAIL_COLS, (k + 1) * TAIL_COLS)
        y_ref[:, cols] += _bdot(actbuf[...], wd_ref[:, cols])

    def tail_end():
        if is_last:
            y_ref[...] = _rms(y_ref[...], fn_ref[...])

    def tail_all():
        tail_begin()
        for k in range(TAIL_PARTS):
            tail_part(k)
        tail_end()

    front = functools.partial(
        _prompt_front, S, T, x_ref, invcnt_ref, n1_ref, win_ref, pw_ref, pscale_ref, vng_ref, sp_ref,
        bfull_ref, wout_ref, n2_ref, wg_ref, wu_ref, cw_ref, cb_ref, pool_out_ref, conv_out_ref,
        pbuf, st8, st4, st2, gbuf, mixbuf, h2buf, actbuf, x1buf)

    s = pl.program_id(0)
    gcarry = refs[9]
    j = lax.rem(s, seq_steps)

    @pl.when(s == 0)
    def _():
        x1buf[...] = jnp.zeros(x1buf.shape, x1buf.dtype)
        actbuf[...] = jnp.zeros(actbuf.shape, actbuf.dtype)

    @pl.when(j == 0)
    def _():
        pbuf[:, 0:POOL_PAD, :] = jnp.zeros((S, POOL_PAD, POOL_WIDTH), jnp.float32)
        gcarry[...] = jnp.zeros((CONV_PAD, D_FF), jnp.float32)

    @pl.when(j > 0)
    def _():
        pbuf[:, 0:POOL_PAD, :] = pbuf[:, T:T + POOL_PAD, :]

    @pl.when(s < n_tiles)
    def _():
        tail_begin()
        front(gcarry, tail_part)
        tail_end()

    @pl.when(s == n_tiles)
    def _():
        tail_all()


def _prompt_front(S, T, x_ref, invcnt_ref, n1_ref, win_ref, pw_ref, pscale_ref, vng_ref, sp_ref,
                  bfull_ref, wout_ref, n2_ref, wg_ref, wu_ref, cw_ref, cb_ref, pool_out_ref, conv_out_ref,
                  pbuf, st8, st4, st2, gbuf, mixbuf, h2buf, actbuf, x1buf, gcarry, fill):
    TM = S * T
    R = POOL_PAD + T
    x = x_ref[...]
    fill(0)
    h = _rms(x, n1_ref[...]).astype(jnp.bfloat16)

    p = _bdot(h, win_ref[:, 0:POOL_WIDTH])
    u = _gelu(_bdot(h, win_ref[:, POOL_WIDTH:POOL_WIDTH + SG_WIDTH]))
    v = _rms(_gelu(_bdot(h, win_ref[:, POOL_WIDTH + SG_WIDTH:IN_WIDTH])), vng_ref[...])
    fill(1)
    pbuf[:, POOL_PAD:R, :] = p.reshape(S, T, POOL_WIDTH)
    pool_out_ref[...] = pbuf[:, R - POOL_BUF:R, :]

    zeros8 = jnp.zeros((S, 8, POOL_WIDTH), jnp.float32)
    st8[:, 0:8, :] = zeros8
    st4[:, 0:8, :] = zeros8
    st2[:, 0:8, :] = zeros8
    G = POOL_GROUP_DIM
    st8[:, 8:R, 3 * G:4 * G] = pbuf[:, 8:R, 3 * G:4 * G] + pbuf[:, 0:R - 8, 3 * G:4 * G]
    st4[:, 8:R, 2 * G:3 * G] = pbuf[:, 8:R, 2 * G:3 * G] + pbuf[:, 4:R - 4, 2 * G:3 * G]
    st4[:, 8:R, 3 * G:4 * G] = st8[:, 8:R, 3 * G:4 * G] + st8[:, 4:R - 4, 3 * G:4 * G]
    st2[:, 8:R, 1 * G:2 * G] = pbuf[:, 8:R, 1 * G:2 * G] + pbuf[:, 6:R - 2, 1 * G:2 * G]
    st2[:, 8:R, 2 * G:4 * G] = st4[:, 8:R, 2 * G:4 * G] + st4[:, 6:R - 2, 2 * G:4 * G]
    sum0 = pbuf[:, POOL_PAD:R, 0:G] + pbuf[:, POOL_PAD - 1:R - 1, 0:G]
    sum123 = st2[:, POOL_PAD:R, G:4 * G] + st2[:, POOL_PAD - 1:R - 1, G:4 * G]
    wsum = jnp.concatenate([sum0, sum123], axis=-1).reshape(TM, POOL_WIDTH)
    dpool = wsum * invcnt_ref[...] - p
    d16 = dpool.astype(jnp.bfloat16)
    half = POOL_WIDTH // 2
    for k in range(2):
        ks = slice(k * half, (k + 1) * half)
        a = _bdot(d16[:, ks], pw_ref[k]) * pscale_ref[:, ks]
        mixbuf[:, ks] = a.astype(jnp.bfloat16)

    fill(2)

    v16 = v.astype(jnp.bfloat16)
    for hd in range(SG_HEADS):
        sl = slice(hd * SG_HEAD_DIM, (hd + 1) * SG_HEAD_DIM)
        osl = slice(POOL_WIDTH + hd * SG_HEAD_DIM, POOL_WIDTH + (hd + 1) * SG_HEAD_DIM)
        chunks = [slice(c * CHUNK, (c + 1) * CHUNK) for c in range(TM // CHUNK)]
        mixed_all = _bdot(sp_ref[hd], jnp.concatenate([v16[rs, sl] for rs in chunks], axis=1))
        for c, rs in enumerate(chunks):
            mixed = mixed_all[:, c * SG_HEAD_DIM:(c + 1) * SG_HEAD_DIM] + bfull_ref[:, sl]
            mixbuf[rs, osl] = (u[rs, sl] * mixed).astype(jnp.bfloat16)

    x1 = x + _bdot(mixbuf[...], wout_ref[...])
    x1buf[...] = x1
    fill(3)
    h2buf[...] = _rms(x1, n2_ref[...]).astype(jnp.bfloat16)

    for c in range(D_FF // FF_CHUNK):
        cs = slice(c * FF_CHUNK, (c + 1) * FF_CHUNK)
        g = _bdot(h2buf[...], wg_ref[:, cs])
        up = _bdot(h2buf[...], wu_ref[:, cs])
        g3 = g.reshape(S, T, FF_CHUNK)
        gbuf[0, 0:CONV_PAD, :] = gcarry[:, cs]
        gcarry[:, cs] = g3[0, T - CONV_PAD:T, :]
        gbuf[:, CONV_PAD:CONV_PAD + T, :] = g3
        conv_out_ref[:, :, cs] = g3[:, T - CONV_BUF:T, :]
        conv = (cb_ref[:, cs]
                + cw_ref[0:1, cs] * gbuf[:, CONV_PAD - 2:CONV_PAD - 2 + T, :]
                + cw_ref[1:2, cs] * gbuf[:, CONV_PAD - 1:CONV_PAD - 1 + T, :]
                + cw_ref[2:3, cs] * g3)
        act = _gelu(conv).reshape(TM, FF_CHUNK) * up
        actbuf[:, cs] = act.astype(jnp.bfloat16)


def _run_prompt_layer(layer, depth, is_last, x, invcnt, small, mats, spatial, next_f32, stacked, zero_fill=(),
                      relayout=()):
    (n1, pscale, vng, n2, cw, cb, pw, fn) = small
    (win, wout, wg, wu, wd) = mats
    sp, bfull = spatial
    TM = TM_PROMPT
    f32 = jnp.float32
    B, L, _ = x.shape
    S, T = 1, TM
    seq_steps = steps = L // TM
    n_tiles = B * steps
    grid = (n_tiles + 1,)
    cur = lambda s: jnp.minimum(s, n_tiles - 1)
    prev = lambda s: jnp.maximum(s - 1, 0)
    in_specs = [pl.BlockSpec((None, TM, D_MODEL), lambda s: (cur(s) // steps, cur(s) % steps, 0)),
                pl.BlockSpec((None, TM, POOL_WIDTH), lambda s: (jnp.minimum(cur(s) % steps, 1), 0, 0))]
    operands = [x, invcnt]
    out_shape = [jax.ShapeDtypeStruct((B, L, D_MODEL), f32),
                 jax.ShapeDtypeStruct((depth, B, POOL_BUF, POOL_WIDTH), f32),
                 jax.ShapeDtypeStruct((depth, B, CONV_BUF, D_FF), f32)]
    out_specs = [pl.BlockSpec((None, TM, D_MODEL), lambda s: (prev(s) // steps, prev(s) % steps, 0)),
                 pl.BlockSpec((None, 1, POOL_BUF, POOL_WIDTH), lambda s: (layer, cur(s) // steps, 0, 0)),
                 pl.BlockSpec((None, 1, CONV_BUF, D_FF), lambda s: (layer, cur(s) // steps, 0, 0))]

    in_specs += [
        _const_spec((depth, D_MODEL)),
        _const_spec((D_MODEL, IN_WIDTH)),
        _layer_spec((2, POOL_WIDTH // 2, POOL_WIDTH // 2), layer),
        _const_spec((depth, POOL_WIDTH)),
        _const_spec((depth, SG_WIDTH)),
        _layer_spec(sp.shape[1:], layer),
        _layer_spec(bfull.shape[1:], layer),
        _const_spec((D_MODEL, D_MODEL)),
        _const_spec((depth, D_MODEL)),
        _const_spec((D_MODEL, D_FF)),
        _const_spec((D_MODEL, D_FF)),
        _layer_spec((3, D_FF), layer),
        _const_spec((depth, D_FF)),
        _const_spec((D_FF, D_MODEL)),
        _const_spec((1, D_MODEL)),
    ]
    operands += [n1, win, pw, pscale, vng, sp, bfull, wout, n2, wg, wu, cw, cb, wd, fn]

    n_cast = 0
    if next_f32 is not None:
        for w in next_f32:
            rows, share = ((D_MODEL // n_tiles, 1) if w.shape[1] == D_MODEL
                           else (DOWN_CAST_ROWS, n_tiles * DOWN_CAST_ROWS // D_FF))
            cols = w.shape[2]
            in_specs.append(pl.BlockSpec((None, rows, cols),
                                         lambda s, share=share: (layer + 1, cur(s) // share, 0)))
            out_specs.append(pl.BlockSpec((rows, cols), lambda s, share=share: (cur(s) // share, 0)))
            out_shape.append(jax.ShapeDtypeStruct(w.shape[1:], jnp.bfloat16))
            operands.append(w)
            n_cast += 1

    for arr in relayout:
        n_lay, n_sq, n_rows, n_ch = arr.shape
        assert n_tiles % n_lay == 0
        per_layer = n_tiles // n_lay
        sq = n_sq // per_layer
        assert sq * per_layer == n_sq
        in_specs.append(pl.BlockSpec(
            (1, sq, n_rows, n_ch), lambda s, per_layer=per_layer: (cur(s) // per_layer, cur(s) % per_layer, 0, 0)))
        operands.append(arr)

    for shape in zero_fill:
        assert n_tiles % shape[0] == 0 and len(shape) == 4
        per_layer = n_tiles // shape[0]
        blk = (1, shape[1], shape[2] // per_layer, shape[3])
        assert blk[2] * per_layer == shape[2]
        out_specs.append(pl.BlockSpec(
            blk, lambda s, per_layer=per_layer: (cur(s) // per_layer, 0, cur(s) % per_layer, 0)))
        out_shape.append(jax.ShapeDtypeStruct(shape, f32))

    for arr in relayout:
        n_lay, n_sq, n_rows, n_ch = arr.shape
        per_layer = n_tiles // n_lay
        out_specs.append(pl.BlockSpec(
            (1, n_rows, n_sq // per_layer, n_ch),
            lambda s, per_layer=per_layer: (cur(s) // per_layer, 0, cur(s) % per_layer, 0)))
        out_shape.append(jax.ShapeDtypeStruct((n_lay, n_rows, n_sq, n_ch), arr.dtype))

    aliases = {}
    for k, arr in enumerate(stacked):
        aliases[len(operands)] = 1 + k
        in_specs.append(pl.BlockSpec(memory_space=pl.ANY))
        operands.append(arr)
    n_alias = len(stacked)

    R = POOL_PAD + T
    scratch = [pltpu.VMEM((S, R, POOL_WIDTH), jnp.float32)] * 4
    scratch += [pltpu.VMEM((S, CONV_PAD + T, FF_CHUNK), jnp.float32),
                pltpu.VMEM((TM, D_MODEL), jnp.bfloat16),
                pltpu.VMEM((TM, D_MODEL), jnp.bfloat16),
                pltpu.VMEM((TM, D_FF), jnp.bfloat16),
                pltpu.VMEM((TM, D_MODEL), jnp.float32),
                pltpu.VMEM((CONV_PAD, D_FF), jnp.float32)]

    return pl.pallas_call(
        functools.partial(_prompt_body, layer, T, is_last, n_cast, n_alias, len(zero_fill), len(relayout),
                          seq_steps, n_tiles),
        grid=grid,
        in_specs=in_specs,
        out_specs=out_specs,
        out_shape=out_shape,
        scratch_shapes=scratch,
        input_output_aliases=aliases,
        compiler_params=pltpu.CompilerParams(dimension_semantics=("arbitrary",),
                                             vmem_limit_bytes=VMEM_LIMIT_BYTES),
        name=f"prompt_layer{layer}",
    )(*operands)


def _sample_body(layer, T, seq_major_in, seq_major_out, is_last, x_ref, spool_ref, sconv_ref, wsp_ref, bsp_ref,
                 n1_ref, win_ref, pw_ref, pscale_ref, vng_ref, wout_ref, n2_ref, wg_ref, wu_ref, cw_ref, cb_ref,
                 wd_ref, fn_ref, pool_in_all, conv_in_all, v_in_all, y_ref, pool_out_ref, conv_out_ref, v_out_ref,
                 gbuf, mixbuf, h2buf, actbuf):
    del pool_in_all, conv_in_all, v_in_all
    S = SAMPLE_SEQS
    TM = T * S
    n1_ref, pscale_ref, vng_ref, n2_ref, cb_ref = (
        r.at[pl.ds(layer, 1)] for r in (n1_ref, pscale_ref, vng_ref, n2_ref, cb_ref))
    slab = lambda t: slice(t * S, (t + 1) * S)

    if seq_major_in:
        x = jnp.concatenate([x_ref[:, t, :] for t in range(T)], axis=0)
    else:
        x = x_ref[...].reshape(TM, D_MODEL)
    h = _rms(x, n1_ref[...]).astype(jnp.bfloat16)
    p = _bdot(h, win_ref[:, 0:POOL_WIDTH])
    u = _gelu(_bdot(h, win_ref[:, POOL_WIDTH:POOL_WIDTH + SG_WIDTH]))
    v = _rms(_gelu(_bdot(h, win_ref[:, POOL_WIDTH + SG_WIDTH:IN_WIDTH])), vng_ref[...])
    v_out_ref[...] = v.reshape(T, S, SG_WIDTH)

    pool_out_ref[0:POOL_BUF - T] = spool_ref[T:POOL_BUF]
    pool_out_ref[POOL_BUF - T:POOL_BUF] = p.reshape(T, S, POOL_WIDTH)
    G = POOL_GROUP_DIM
    for gi, w in enumerate(POOL_WINDOWS):
        gs = slice(gi * G, (gi + 1) * G)

        def row(j):
            return spool_ref[j, :, gs] if j < POOL_BUF else p[slab(j - POOL_BUF), gs]

        for t in range(T):
            acc = row(POOL_BUF + t)
            for k in range(1, w):
                acc = acc + row(POOL_BUF + t - k)
            d = acc * (1.0 / w) - p[slab(t), gs]
            h2buf[slab(t), gs] = d.astype(jnp.bfloat16)
    half = POOL_WIDTH // 2
    for k in range(2):
        ks = slice(k * half, (k + 1) * half)
        a = _bdot(h2buf[:, ks], pw_ref[k]) * pscale_ref[:, ks]
        mixbuf[:, ks] = a.astype(jnp.bfloat16)

    for hd in range(SG_HEADS):
        hs = slice(hd * SG_HEAD_DIM, (hd + 1) * SG_HEAD_DIM)
        for t in range(T):
            mixed = jnp.full((S, SG_HEAD_DIM), bsp_ref[layer, hd * T + t], jnp.float32)
            for t2 in range(t + 1):
                mixed = mixed + wsp_ref[layer, (hd * T + t) * T + t2] * v[slab(t2), hs]
            mixbuf[slab(t), POOL_WIDTH + hd * SG_HEAD_DIM:POOL_WIDTH + (hd + 1) * SG_HEAD_DIM] = (
                (u[slab(t), hs] * mixed).astype(jnp.bfloat16))

    x1 = x + _bdot(mixbuf[...], wout_ref[...])
    h2buf[...] = _rms(x1, n2_ref[...]).astype(jnp.bfloat16)

    hist = CONV_BUF * S
    for c in range(D_FF // FF_CHUNK):
        cs = slice(c * FF_CHUNK, (c + 1) * FF_CHUNK)
        g = _bdot(h2buf[...], wg_ref[:, cs])
        up = _bdot(h2buf[...], wu_ref[:, cs])
        gbuf[0:hist, :] = sconv_ref[:, :, cs].reshape(hist, FF_CHUNK)
        gbuf[hist:hist + TM, :] = g
        conv_out_ref[:, :, cs] = g[TM - hist:TM, :].reshape(CONV_BUF, S, FF_CHUNK)
        conv = (cb_ref[:, cs]
                + cw_ref[0:1, cs] * gbuf[0:TM, :]
                + cw_ref[1:2, cs] * gbuf[S:S + TM, :]
                + cw_ref[2:3, cs] * g)
        actbuf[:, cs] = (_gelu(conv) * up).astype(jnp.bfloat16)

    out = x1 + _bdot(actbuf[...], wd_ref[...])
    if is_last:
        out = _rms(out, fn_ref[...])
    if seq_major_out:
        for t in range(T):
            y_ref[:, t, :] = out[slab(t)]
    else:
        y_ref[...] = out.reshape(T, S, D_MODEL)


def _run_sample_layer(layer, depth, is_last, x, seq_major_in, seq_major_out, state_pool_t, state_conv_t, wsp, bsp,
                      small, mats, stacked):
    (n1, pscale, vng, n2, cw, cb, pw, fn) = small
    (win, wout, wg, wu, wd) = mats
    n_seq, T = (x.shape[0], x.shape[1]) if seq_major_in else (x.shape[1], x.shape[0])
    S = SAMPLE_SEQS
    TM = T * S
    f32 = jnp.float32
    smem = pl.BlockSpec(memory_space=pltpu.SMEM)
    seq_major_spec = pl.BlockSpec((S, T, D_MODEL), lambda i: (i, 0, 0))
    time_major_spec = pl.BlockSpec((T, S, D_MODEL), lambda i: (0, i, 0))
    in_specs = [
        seq_major_spec if seq_major_in else time_major_spec,
        pl.BlockSpec((None, POOL_BUF, S, POOL_WIDTH), lambda i: (layer, 0, i, 0)),
        pl.BlockSpec((None, CONV_BUF, S, D_FF), lambda i: (layer, 0, i, 0)),
        smem, smem,
        _const_spec((depth, D_MODEL)),
        _const_spec((D_MODEL, IN_WIDTH)),
        _layer_spec((2, POOL_WIDTH // 2, POOL_WIDTH // 2), layer),
        _const_spec((depth, POOL_WIDTH)),
        _const_spec((depth, SG_WIDTH)),
        _const_spec((D_MODEL, D_MODEL)),
        _const_spec((depth, D_MODEL)),
        _const_spec((D_MODEL, D_FF)),
        _const_spec((D_MODEL, D_FF)),
        _layer_spec((3, D_FF), layer),
        _const_spec((depth, D_FF)),
        _const_spec((D_FF, D_MODEL)),
        _const_spec((1, D_MODEL)),
    ]
    operands = [x, state_pool_t, state_conv_t, wsp, bsp, n1, win, pw, pscale, vng, wout, n2, wg, wu, cw, cb, wd, fn]
    aliases = {}
    for k, arr in enumerate(stacked):
        aliases[len(operands)] = 1 + k
        in_specs.append(pl.BlockSpec(memory_space=pl.ANY))
        operands.append(arr)
    out_shape = [jax.ShapeDtypeStruct((n_seq, T, D_MODEL) if seq_major_out else (T, n_seq, D_MODEL), f32),
                 jax.ShapeDtypeStruct((depth, POOL_BUF, n_seq, POOL_WIDTH), f32),
                 jax.ShapeDtypeStruct((depth, CONV_BUF, n_seq, D_FF), f32),
                 jax.ShapeDtypeStruct((depth, T, n_seq, SG_WIDTH), f32)]
    out_specs = [seq_major_spec if seq_major_out else time_major_spec,
                 pl.BlockSpec((None, POOL_BUF, S, POOL_WIDTH), lambda i: (layer, 0, i, 0)),
                 pl.BlockSpec((None, CONV_BUF, S, D_FF), lambda i: (layer, 0, i, 0)),
                 pl.BlockSpec((None, T, S, SG_WIDTH), lambda i: (layer, 0, i, 0))]
    scratch = [pltpu.VMEM(((CONV_BUF + T) * S, FF_CHUNK), f32),
               pltpu.VMEM((TM, D_MODEL), jnp.bfloat16),
               pltpu.VMEM((TM, D_MODEL), jnp.bfloat16),
               pltpu.VMEM((TM, D_FF), jnp.bfloat16)]
    return pl.pallas_call(
        functools.partial(_sample_body, layer, T, seq_major_in, seq_major_out, is_last),
        grid=(n_seq // S,),
        in_specs=in_specs,
        out_specs=out_specs,
        out_shape=out_shape,
        scratch_shapes=scratch,
        input_output_aliases=aliases,
        compiler_params=pltpu.CompilerParams(dimension_semantics=("arbitrary",),
                                             vmem_limit_bytes=VMEM_LIMIT_BYTES),
        name=f"sample_layer{layer}",
    )(*operands)


def _inv_counts(n_rows):
    win = np.repeat(np.asarray(POOL_WINDOWS, np.float32), POOL_GROUP_DIM)[None, :]
    pos = np.arange(n_rows, dtype=np.float32)[:, None]
    first = 1.0 / np.minimum(pos + 1.0, win)
    later = np.broadcast_to(1.0 / win, (n_rows, POOL_WIDTH))
    return jnp.asarray(np.stack([first, later]).astype(np.float32))


def _prep_body(pool_w_ref, wsp_ref, bt_ref, pw_ref, spp_ref, bfp_ref):
    bf = jnp.bfloat16
    G = POOL_GROUP_DIM
    for k in range(2):
        for a in range(2):
            for b in range(2):
                blk = pool_w_ref[2 * k + a].astype(bf) if a == b else jnp.zeros((G, G), bf)
                pw_ref[k, a * G:(a + 1) * G, b * G:(b + 1) * G] = blk
    row = lax.broadcasted_iota(jnp.int32, (CHUNK, CHUNK), 0)
    col = lax.broadcasted_iota(jnp.int32, (CHUNK, CHUNK), 1)
    causal = row >= col
    for h in range(SG_HEADS):
        spp_ref[h] = jnp.where(causal, wsp_ref[h], 0.0).astype(bf)
        bfp_ref[:, h * SG_HEAD_DIM:(h + 1) * SG_HEAD_DIM] = jnp.broadcast_to(
            bt_ref[:, h:h + 1], (CHUNK, SG_HEAD_DIM))


def _prep_operands(pool_w, w_spatial, b_spatial):
    depth = pool_w.shape[0]
    bf, f32 = jnp.bfloat16, jnp.float32
    per_layer = lambda *shape: pl.BlockSpec((None,) + shape, lambda l: (l,) + (0,) * len(shape))
    return pl.pallas_call(
        _prep_body,
        grid=(depth,),
        in_specs=[per_layer(len(POOL_WINDOWS), POOL_GROUP_DIM, POOL_GROUP_DIM),
                  per_layer(SG_HEADS, CHUNK, CHUNK),
                  per_layer(CHUNK, SG_HEADS)],
        out_specs=[per_layer(2, POOL_WIDTH // 2, POOL_WIDTH // 2), per_layer(SG_HEADS, CHUNK, CHUNK),
                   per_layer(CHUNK, SG_WIDTH)],
        out_shape=[jax.ShapeDtypeStruct((depth, 2, POOL_WIDTH // 2, POOL_WIDTH // 2), bf),
                   jax.ShapeDtypeStruct((depth, SG_HEADS, CHUNK, CHUNK), bf),
                   jax.ShapeDtypeStruct((depth, CHUNK, SG_WIDTH), f32)],
        compiler_params=pltpu.CompilerParams(dimension_semantics=("arbitrary",)),
        name="prep_operands",
    )(pool_w, w_spatial, jnp.swapaxes(b_spatial, 1, 2))


def kernel(x_prompt, x_sample, state_pool, state_conv, norm1_g, w_in, pool_w, pool_scale, v_norm_g, w_spatial, b_spatial, w_out, norm2_g, w_gate, w_up, conv_w, conv_b, w_down, final_norm_g):
    depth = w_in.shape[0]
    bf = jnp.bfloat16
    pw, sp_p, bias_p = _prep_operands(pool_w, w_spatial, b_spatial)
    small = (norm1_g, pool_scale, v_norm_g, norm2_g, conv_w, conv_b, pw, final_norm_g[None, :])
    big_f32 = (w_in, w_out, w_gate, w_up, w_down)
    invcnt = _inv_counts(TM_PROMPT)

    n_dec, t_dec, _ = x_sample.shape
    mats = tuple(w[0].astype(bf) for w in big_f32)
    state_pool_t = jnp.transpose(state_pool, (0, 2, 1, 3))
    wsp = w_spatial[:, :, :t_dec, :t_dec].reshape(depth, SG_HEADS * t_dec * t_dec)
    bsp = b_spatial[:, :, :t_dec].reshape(depth, SG_HEADS * t_dec)
    n_prompt = x_prompt.shape[0]
    f32 = jnp.float32
    stacked_p = (jnp.zeros((depth, n_prompt, POOL_BUF, POOL_WIDTH), f32),
                 jnp.zeros((depth, n_prompt, CONV_BUF, D_FF), f32))
    sample_stacked_shapes = ((depth, POOL_BUF, n_dec, POOL_WIDTH), (depth, CONV_BUF, n_dec, D_FF),
                             (depth, t_dec, n_dec, SG_WIDTH))
    stacked_s = None
    xp, xs = x_prompt, x_sample
    for i in range(depth):
        last = i == depth - 1
        res = _run_prompt_layer(i, depth, last, xp, invcnt, small, mats, (sp_p, bias_p),
                                None if last else big_f32, stacked_p,
                                zero_fill=sample_stacked_shapes if i == 0 else (),
                                relayout=(state_conv,) if i == 0 else ())
        xp, stacked_p = res[0], res[1:3]
        next_mats = tuple(res[3:3 + len(big_f32)])
        if i == 0:
            extra = res[3 + len(next_mats):]
            stacked_s, state_conv_t = tuple(extra[:len(sample_stacked_shapes)]), extra[len(sample_stacked_shapes)]
        res_s = _run_sample_layer(i, depth, last, xs, i == 0, last, state_pool_t, state_conv_t, wsp, bsp,
                                  small, mats, stacked_s)
        xs, stacked_s = res_s[0], res_s[1:4]
        mats = next_mats
    pool_p, conv_p = stacked_p
    pool_s, conv_s, v_s = (jnp.transpose(a, (0, 2, 1, 3)) for a in stacked_s)
    return (xp, xs, pool_p, pool_s, conv_p, conv_s, v_s)
```

```python
import functools

import numpy as np
import jax
import jax.numpy as jnp
from jax import lax
from jax.experimental import pallas as pl
from jax.experimental.pallas import tpu as pltpu

D_MODEL = 1024
POOL_WIDTH = 512
POOL_WINDOWS = (2, 4, 8, 16)
POOL_GROUP_DIM = 128
POOL_BUF = 15
POOL_PAD = 16
SG_WIDTH = 512
SG_HEADS = 4
SG_HEAD_DIM = 128
CHUNK = 128
IN_WIDTH = POOL_WIDTH + 2 * SG_WIDTH
D_FF = 2816
CONV_BUF = 2
CONV_PAD = 8
EPS = 1e-6

TM_PROMPT = 512
SAMPLE_SEQS = 64
FF_CHUNK = 256
TAIL_PARTS = 4
TAIL_COLS = D_MODEL // TAIL_PARTS
DOWN_CAST_ROWS = 176
VMEM_LIMIT_BYTES = 58 * 1024 * 1024

_INV_SQRT2 = 0.7071067811865476


def _gelu(x):
    return 0.5 * x * (1.0 + lax.erf(x * _INV_SQRT2))


def _rms(x, g):
    ms = jnp.mean(x * x, axis=-1, keepdims=True)
    return x * lax.rsqrt(ms + EPS) * g


def _bdot(a, b):
    return jnp.dot(a, b, preferred_element_type=jnp.float32)


def _const_spec(shape):
    nd = len(shape)
    return pl.BlockSpec(shape, lambda *_: (0,) * nd)


def _layer_spec(shape, layer):
    nd = len(shape)
    return pl.BlockSpec((None,) + tuple(shape), lambda *_: (layer,) + (0,) * nd)


def _prompt_body(layer, T, is_last, n_cast, n_alias, n_zero, n_relay, seq_steps, n_tiles, *refs):
    S = 1
    refs = list(refs)
    x_ref = refs.pop(0)
    (invcnt_ref, n1_ref, win_ref, pw_ref, pscale_ref, vng_ref, sp_ref, bfull_ref, wout_ref, n2_ref,
     wg_ref, wu_ref, cw_ref, cb_ref, wd_ref, fn_ref) = refs[:16]
    refs = refs[16:]
    n1_ref, pscale_ref, vng_ref, n2_ref, cb_ref = (
        r.at[pl.ds(layer, 1)] for r in (n1_ref, pscale_ref, vng_ref, n2_ref, cb_ref))
    cast_in, refs = refs[:n_cast], refs[n_cast:]
    relay_in, refs = refs[:n_relay], refs[n_relay:]
    refs = refs[n_alias:]
    y_ref, pool_out_ref, conv_out_ref = refs[:3]
    refs = refs[3:]
    cast_out, refs = refs[:n_cast], refs[n_cast:]
    zero_out, refs = refs[:n_zero], refs[n_zero:]
    relay_out, refs = refs[:n_relay], refs[n_relay:]

    for dst in zero_out:
        dst[...] = jnp.zeros(dst.shape, dst.dtype)
    for src, dst in zip(relay_in, relay_out):
        for r in range(dst.shape[1]):
            dst[0, r] = src[0, :, r, :]
    pbuf, st8, st4, st2, gbuf, mixbuf, h2buf, actbuf, x1buf = refs[:9]

    for src, dst in zip(cast_in, cast_out):
        dst[...] = src[...].astype(jnp.bfloat16)

    def tail_begin():
        y_ref[...] = x1buf[...]

    def tail_part(k):
        cols = slice(k * TAIL_COLS, (k + 1) * TAIL_COLS)
        y_ref[:, cols] += _bdot(actbuf[...], wd_ref[:, cols])

    def tail_end():
        if is_last:
            y_ref[...] = _rms(y_ref[...], fn_ref[...])

    def tail_all():
        tail_begin()
        for k in range(TAIL_PARTS):
            tail_part(k)
        tail_end()

    front = functools.partial(
        _prompt_front, S, T, x_ref, invcnt_ref, n1_ref, win_ref, pw_ref, pscale_ref, vng_ref, sp_ref,
        bfull_ref, wout_ref, n2_ref, wg_ref, wu_ref, cw_ref, cb_ref, pool_out_ref, conv_out_ref,
        pbuf, st8, st4, st2, gbuf, mixbuf, h2buf, actbuf, x1buf)

    s = pl.program_id(0)
    gcarry = refs[9]
    j = lax.rem(s, seq_steps)

    @pl.when(s == 0)
    def _():
        x1buf[...] = jnp.zeros(x1buf.shape, x1buf.dtype)
        actbuf[...] = jnp.zeros(actbuf.shape, actbuf.dtype)

    @pl.when(j == 0)
    def _():
        pbuf[:, 0:POOL_PAD, :] = jnp.zeros((S, POOL_PAD, POOL_WIDTH), jnp.float32)
        gcarry[...] = jnp.zeros((CONV_PAD, D_FF), jnp.float32)

    @pl.when(j > 0)
    def _():
        pbuf[:, 0:POOL_PAD, :] = pbuf[:, T:T + POOL_PAD, :]

    @pl.when(s < n_tiles)
    def _():
        tail_begin()
        front(gcarry, tail_part)
        tail_end()

    @pl.when(s == n_tiles)
    def _():
        tail_all()


def _prompt_front(S, T, x_ref, invcnt_ref, n1_ref, win_ref, pw_ref, pscale_ref, vng_ref, sp_ref,
                  bfull_ref, wout_ref, n2_ref, wg_ref, wu_ref, cw_ref, cb_ref, pool_out_ref, conv_out_ref,
                  pbuf, st8, st4, st2, gbuf, mixbuf, h2buf, actbuf, x1buf, gcarry, fill):
    TM = S * T
    R = POOL_PAD + T
    x = x_ref[...]
    fill(0)
    h = _rms(x, n1_ref[...]).astype(jnp.bfloat16)

    p = _bdot(h, win_ref[:, 0:POOL_WIDTH])
    u = _gelu(_bdot(h, win_ref[:, POOL_WIDTH:POOL_WIDTH + SG_WIDTH]))
    v = _rms(_gelu(_bdot(h, win_ref[:, POOL_WIDTH + SG_WIDTH:IN_WIDTH])), vng_ref[...])
    fill(1)
    pbuf[:, POOL_PAD:R, :] = p.reshape(S, T, POOL_WIDTH)
    pool_out_ref[...] = pbuf[:, R - POOL_BUF:R, :]

    zeros8 = jnp.zeros((S, 8, POOL_WIDTH), jnp.float32)
    st8[:, 0:8, :] = zeros8
    st4[:, 0:8, :] = zeros8
    st2[:, 0:8, :] = zeros8
    G = POOL_GROUP_DIM
    st8[:, 8:R, 3 * G:4 * G] = pbuf[:, 8:R, 3 * G:4 * G] + pbuf[:, 0:R - 8, 3 * G:4 * G]
    st4[:, 8:R, 2 * G:3 * G] = pbuf[:, 8:R, 2 * G:3 * G] + pbuf[:, 4:R - 4, 2 * G:3 * G]
    st4[:, 8:R, 3 * G:4 * G] = st8[:, 8:R, 3 * G:4 * G] + st8[:, 4:R - 4, 3 * G:4 * G]
    st2[:, 8:R, 1 * G:2 * G] = pbuf[:, 8:R, 1 * G:2 * G] + pbuf[:, 6:R - 2, 1 * G:2 * G]
    st2[:, 8:R, 2 * G:4 * G] = st4[:, 8:R, 2 * G:4 * G] + st4[:, 6:R - 2, 2 * G:4 * G]
    sum0 = pbuf[:, POOL_PAD:R, 0:G] + pbuf[:, POOL_PAD - 1:R - 1, 0:G]
    sum123 = st2[:, POOL_PAD:R, G:4 * G] + st2[:, POOL_PAD - 1:R - 1, G:4 * G]
    wsum = jnp.concatenate([sum0, sum123], axis=-1).reshape(TM, POOL_WIDTH)
    dpool = wsum * invcnt_ref[...] - p
    d16 = dpool.astype(jnp.bfloat16)
    half = POOL_WIDTH // 2
    for k in range(2):
        ks = slice(k * half, (k + 1) * half)
        a = _bdot(d16[:, ks], pw_ref[k]) * pscale_ref[:, ks]
        mixbuf[:, ks] = a.astype(jnp.bfloat16)

    fill(2)

    v16 = v.astype(jnp.bfloat16)
    for hd in range(SG_HEADS):
        sl = slice(hd * SG_HEAD_DIM, (hd + 1) * SG_HEAD_DIM)
        osl = slice(POOL_WIDTH + hd * SG_HEAD_DIM, POOL_WIDTH + (hd + 1) * SG_HEAD_DIM)
        chunks = [slice(c * CHUNK, (c + 1) * CHUNK) for c in range(TM // CHUNK)]
        mixed_all = _bdot(sp_ref[hd], jnp.concatenate([v16[rs, sl] for rs in chunks], axis=1))
        for c, rs in enumerate(chunks):
            mixed = mixed_all[:, c * SG_HEAD_DIM:(c + 1) * SG_HEAD_DIM] + bfull_ref[:, sl]
            mixbuf[rs, osl] = (u[rs, sl] * mixed).astype(jnp.bfloat16)

    x1 = x + _bdot(mixbuf[...], wout_ref[...])
    x1buf[...] = x1
    fill(3)
    h2buf[...] = _rms(x1, n2_ref[...]).astype(jnp.bfloat16)

    for c in range(D_FF // FF_CHUNK):
        cs = slice(c * FF_CHUNK, (c + 1) * FF_CHUNK)
        g = _bdot(h2buf[...], wg_ref[:, cs])
        up = _bdot(h2buf[...], wu_ref[:, cs])
        g3 = g.reshape(S, T, FF_CHUNK)
        gbuf[0, 0:CONV_PAD, :] = gcarry[:, cs]
        gcarry[:, cs] = g3[0, T - CONV_PAD:T, :]
        gbuf[:, CONV_PAD:CONV_PAD + T, :] = g3
        conv_out_ref[:, :, cs] = g3[:, T - CONV_BUF:T, :]
        conv = (cb_ref[:, cs]
                + cw_ref[0:1, cs] * gbuf[:, CONV_PAD - 2:CONV_PAD - 2 + T, :]
                + cw_ref[1:2, cs] * gbuf[:, CONV_PAD - 1:CONV_PAD - 1 + T, :]
                + cw_ref[2:3, cs] * g3)
        act = _gelu(conv).reshape(TM, FF_CHUNK) * up
        actbuf[:, cs] = act.astype(jnp.bfloat16)


def _run_prompt_layer(layer, depth, is_last, x, invcnt, small, mats, spatial, next_f32, stacked, zero_fill=(),
                      relayout=()):
    (n1, pscale, vng, n2, cw, cb, pw, fn) = small
    (win, wout, wg, wu, wd) = mats
    sp, bfull = spatial
    TM = TM_PROMPT
    f32 = jnp.float32
    B, L, _ = x.shape
    S, T = 1, TM
    seq_steps = steps = L // TM
    n_tiles = B * steps
    grid = (n_tiles + 1,)
    cur = lambda s: jnp.minimum(s, n_tiles - 1)
    prev = lambda s: jnp.maximum(s - 1, 0)
    in_specs = [pl.BlockSpec((None, TM, D_MODEL), lambda s: (cur(s) // steps, cur(s) % steps, 0)),
                pl.BlockSpec((None, TM, POOL_WIDTH), lambda s: (jnp.minimum(cur(s) % steps, 1), 0, 0))]
    operands = [x, invcnt]
    out_shape = [jax.ShapeDtypeStruct((B, L, D_MODEL), f32),
                 jax.ShapeDtypeStruct((depth, B, POOL_BUF, POOL_WIDTH), f32),
                 jax.ShapeDtypeStruct((depth, B, CONV_BUF, D_FF), f32)]
    out_specs = [pl.BlockSpec((None, TM, D_MODEL), lambda s: (prev(s) // steps, prev(s) % steps, 0)),
                 pl.BlockSpec((None, 1, POOL_BUF, POOL_WIDTH), lambda s: (layer, cur(s) // steps, 0, 0)),
                 pl.BlockSpec((None, 1, CONV_BUF, D_FF), lambda s: (layer, cur(s) // steps, 0, 0))]

    in_specs += [
        _const_spec((depth, D_MODEL)),
        _const_spec((D_MODEL, IN_WIDTH)),
        _layer_spec((2, POOL_WIDTH // 2, POOL_WIDTH // 2), layer),
        _const_spec((depth, POOL_WIDTH)),
        _const_spec((depth, SG_WIDTH)),
        _layer_spec(sp.shape[1:], layer),
        _layer_spec(bfull.shape[1:], layer),
        _const_spec((D_MODEL, D_MODEL)),
        _const_spec((depth, D_MODEL)),
        _const_spec((D_MODEL, D_FF)),
        _const_spec((D_MODEL, D_FF)),
        _layer_spec((3, D_FF), layer),
        _const_spec((depth, D_FF)),
        _const_spec((D_FF, D_MODEL)),
        _const_spec((1, D_MODEL)),
    ]
    operands += [n1, win, pw, pscale, vng, sp, bfull, wout, n2, wg, wu, cw, cb, wd, fn]

    n_cast = 0
    if next_f32 is not None:
        for w in next_f32:
            rows, share = ((D_MODEL // n_tiles, 1) if w.shape[1] == D_MODEL
                           else (DOWN_CAST_ROWS, n_tiles * DOWN_CAST_ROWS // D_FF))
            cols = w.shape[2]
            in_specs.append(pl.BlockSpec((None, rows, cols),
                                         lambda s, share=share: (layer + 1, cur(s) // share, 0)))
            out_specs.append(pl.BlockSpec((rows, cols), lambda s, share=share: (cur(s) // share, 0)))
            out_shape.append(jax.ShapeDtypeStruct(w.shape[1:], jnp.bfloat16))
            operands.append(w)
            n_cast += 1

    for arr in relayout:
        n_lay, n_sq, n_rows, n_ch = arr.shape
        assert n_tiles % n_lay == 0
        per_layer = n_tiles // n_lay
        sq = n_sq // per_layer
        assert sq * per_layer == n_sq
        in_specs.append(pl.BlockSpec(
            (1, sq, n_rows, n_ch), lambda s, per_layer=per_layer: (cur(s) // per_layer, cur(s) % per_layer, 0, 0)))
        operands.append(arr)

    for shape in zero_fill:
        assert n_tiles % shape[0] == 0 and len(shape) == 4
        per_layer = n_tiles // shape[0]
        blk = (1, shape[1], shape[2] // per_layer, shape[3])
        assert blk[2] * per_layer == shape[2]
        out_specs.append(pl.BlockSpec(
            blk, lambda s, per_layer=per_layer: (cur(s) // per_layer, 0, cur(s) % per_layer, 0)))
        out_shape.append(jax.ShapeDtypeStruct(shape, f32))

    for arr in relayout:
        n_lay, n_sq, n_rows, n_ch = arr.shape
        per_layer = n_tiles // n_lay
        out_specs.append(pl.BlockSpec(
            (1, n_rows, n_sq // per_layer, n_ch),
            lambda s, per_layer=per_layer: (cur(s) // per_layer, 0, cur(s) % per_layer, 0)))
        out_shape.append(jax.ShapeDtypeStruct((n_lay, n_rows, n_sq, n_ch), arr.dtype))

    aliases = {}
    for k, arr in enumerate(stacked):
        aliases[len(operands)] = 1 + k
        in_specs.append(pl.BlockSpec(memory_space=pl.ANY))
        operands.append(arr)
    n_alias = len(stacked)

    R = POOL_PAD + T
    scratch = [pltpu.VMEM((S, R, POOL_WIDTH), jnp.float32)] * 4
    scratch += [pltpu.VMEM((S, CONV_PAD + T, FF_CHUNK), jnp.float32),
                pltpu.VMEM((TM, D_MODEL), jnp.bfloat16),
                pltpu.VMEM((TM, D_MODEL), jnp.bfloat16),
                pltpu.VMEM((TM, D_FF), jnp.bfloat16),
                pltpu.VMEM((TM, D_MODEL), jnp.float32),
                pltpu.VMEM((CONV_PAD, D_FF), jnp.float32)]

    return pl.pallas_call(
        functools.partial(_prompt_body, layer, T, is_last, n_cast, n_alias, len(zero_fill), len(relayout),
                          seq_steps, n_tiles),
        grid=grid,
        in_specs=in_specs,
        out_specs=out_specs,
        out_shape=out_shape,
        scratch_shapes=scratch,
        input_output_aliases=aliases,
        compiler_params=pltpu.CompilerParams(dimension_semantics=("arbitrary",),
                                             vmem_limit_bytes=VMEM_LIMIT_BYTES),
        name=f"prompt_layer{layer}",
    )(*operands)


def _sample_body(layer, T, seq_major_in, seq_major_out, is_last, x_ref, spool_ref, sconv_ref, wsp_ref, bsp_ref,
                 n1_ref, win_ref, pw_ref, pscale_ref, vng_ref, wout_ref, n2_ref, wg_ref, wu_ref, cw_ref, cb_ref,
                 wd_ref, fn_ref, pool_in_all, conv_in_all, v_in_all, y_ref, pool_out_ref, conv_out_ref, v_out_ref,
                 gbuf, mixbuf, h2buf, actbuf):
    del pool_in_all, conv_in_all, v_in_all
    S = SAMPLE_SEQS
    TM = T * S
    n1_ref, pscale_ref, vng_ref, n2_ref, cb_ref = (
        r.at[pl.ds(layer, 1)] for r in (n1_ref, pscale_ref, vng_ref, n2_ref, cb_ref))
    slab = lambda t: slice(t * S, (t + 1) * S)

    if seq_major_in:
        x = jnp.concatenate([x_ref[:, t, :] for t in range(T)], axis=0)
    else:
        x = x_ref[...].reshape(TM, D_MODEL)
    h = _rms(x, n1_ref[...]).astype(jnp.bfloat16)
    p = _bdot(h, win_ref[:, 0:POOL_WIDTH])
    u = _gelu(_bdot(h, win_ref[:, POOL_WIDTH:POOL_WIDTH + SG_WIDTH]))
    v = _rms(_gelu(_bdot(h, win_ref[:, POOL_WIDTH + SG_WIDTH:IN_WIDTH])), vng_ref[...])
    v_out_ref[...] = v.reshape(T, S, SG_WIDTH)

    pool_out_ref[0:POOL_BUF - T] = spool_ref[T:POOL_BUF]
    pool_out_ref[POOL_BUF - T:POOL_BUF] = p.reshape(T, S, POOL_WIDTH)
    G = POOL_GROUP_DIM
    for gi, w in enumerate(POOL_WINDOWS):
        gs = slice(gi * G, (gi + 1) * G)

        def row(j):
            return spool_ref[j, :, gs] if j < POOL_BUF else p[slab(j - POOL_BUF), gs]

        for t in range(T):
            acc = row(POOL_BUF + t)
            for k in range(1, w):
                acc = acc + row(POOL_BUF + t - k)
            d = acc * (1.0 / w) - p[slab(t), gs]
            h2buf[slab(t), gs] = d.astype(jnp.bfloat16)
    half = POOL_WIDTH // 2
    for k in range(2):
        ks = slice(k * half, (k + 1) * half)
        a = _bdot(h2buf[:, ks], pw_ref[k]) * pscale_ref[:, ks]
        mixbuf[:, ks] = a.astype(jnp.bfloat16)

    for hd in range(SG_HEADS):
        hs = slice(hd * SG_HEAD_DIM, (hd + 1) * SG_HEAD_DIM)
        for t in range(T):
            mixed = jnp.full((S, SG_HEAD_DIM), bsp_ref[layer, hd * T + t], jnp.float32)
            for t2 in range(t + 1):
                mixed = mixed + wsp_ref[layer, (hd * T + t) * T + t2] * v[slab(t2), hs]
            mixbuf[slab(t), POOL_WIDTH + hd * SG_HEAD_DIM:POOL_WIDTH + (hd + 1) * SG_HEAD_DIM] = (
                (u[slab(t), hs] * mixed).astype(jnp.bfloat16))

    x1 = x + _bdot(mixbuf[...], wout_ref[...])
    h2buf[...] = _rms(x1, n2_ref[...]).astype(jnp.bfloat16)

    hist = CONV_BUF * S
    for c in range(D_FF // FF_CHUNK):
        cs = slice(c * FF_CHUNK, (c + 1) * FF_CHUNK)
        g = _bdot(h2buf[...], wg_ref[:, cs])
        up = _bdot(h2buf[...], wu_ref[:, cs])
        gbuf[0:hist, :] = sconv_ref[:, :, cs].reshape(hist, FF_CHUNK)
        gbuf[hist:hist + TM, :] = g
        conv_out_ref[:, :, cs] = g[TM - hist:TM, :].reshape(CONV_BUF, S, FF_CHUNK)
        conv = (cb_ref[:, cs]
                + cw_ref[0:1, cs] * gbuf[0:TM, :]
                + cw_ref[1:2, cs] * gbuf[S:S + TM, :]
                + cw_ref[2:3, cs] * g)
        actbuf[:, cs] = (_gelu(conv) * up).astype(jnp.bfloat16)

    out = x1 + _bdot(actbuf[...], wd_ref[...])
    if is_last:
        out = _rms(out, fn_ref[...])
    if seq_major_out:
        for t in range(T):
            y_ref[:, t, :] = out[slab(t)]
    else:
        y_ref[...] = out.reshape(T, S, D_MODEL)


def _run_sample_layer(layer, depth, is_last, x, seq_major_in, seq_major_out, state_pool_t, state_conv_t, wsp, bsp,
                      small, mats, stacked):
    (n1, pscale, vng, n2, cw, cb, pw, fn) = small
    (win, wout, wg, wu, wd) = mats
    n_seq, T = (x.shape[0], x.shape[1]) if seq_major_in else (x.shape[1], x.shape[0])
    S = SAMPLE_SEQS
    TM = T * S
    f32 = jnp.float32
    smem = pl.BlockSpec(memory_space=pltpu.SMEM)
    seq_major_spec = pl.BlockSpec((S, T, D_MODEL), lambda i: (i, 0, 0))
    time_major_spec = pl.BlockSpec((T, S, D_MODEL), lambda i: (0, i, 0))
    in_specs = [
        seq_major_spec if seq_major_in else time_major_spec,
        pl.BlockSpec((None, POOL_BUF, S, POOL_WIDTH), lambda i: (layer, 0, i, 0)),
        pl.BlockSpec((None, CONV_BUF, S, D_FF), lambda i: (layer, 0, i, 0)),
        smem, smem,
        _const_spec((depth, D_MODEL)),
        _const_spec((D_MODEL, IN_WIDTH)),
        _layer_spec((2, POOL_WIDTH // 2, POOL_WIDTH // 2), layer),
        _const_spec((depth, POOL_WIDTH)),
        _const_spec((depth, SG_WIDTH)),
        _const_spec((D_MODEL, D_MODEL)),
        _const_spec((depth, D_MODEL)),
        _const_spec((D_MODEL, D_FF)),
        _const_spec((D_MODEL, D_FF)),
        _layer_spec((3, D_FF), layer),
        _const_spec((depth, D_FF)),
        _const_spec((D_FF, D_MODEL)),
        _const_spec((1, D_MODEL)),
    ]
    operands = [x, state_pool_t, state_conv_t, wsp, bsp, n1, win, pw, pscale, vng, wout, n2, wg, wu, cw, cb, wd, fn]
    aliases = {}
    for k, arr in enumerate(stacked):
        aliases[len(operands)] = 1 + k
        in_specs.append(pl.BlockSpec(memory_space=pl.ANY))
        operands.append(arr)
    out_shape = [jax.ShapeDtypeStruct((n_seq, T, D_MODEL) if seq_major_out else (T, n_seq, D_MODEL), f32),
                 jax.ShapeDtypeStruct((depth, POOL_BUF, n_seq, POOL_WIDTH), f32),
                 jax.ShapeDtypeStruct((depth, CONV_BUF, n_seq, D_FF), f32),
                 jax.ShapeDtypeStruct((depth, T, n_seq, SG_WIDTH), f32)]
    out_specs = [seq_major_spec if seq_major_out else time_major_spec,
                 pl.BlockSpec((None, POOL_BUF, S, POOL_WIDTH), lambda i: (layer, 0, i, 0)),
                 pl.BlockSpec((None, CONV_BUF, S, D_FF), lambda i: (layer, 0, i, 0)),
                 pl.BlockSpec((None, T, S, SG_WIDTH), lambda i: (layer, 0, i, 0))]
    scratch = [pltpu.VMEM(((CONV_BUF + T) * S, FF_CHUNK), f32),
               pltpu.VMEM((TM, D_MODEL), jnp.bfloat16),
               pltpu.VMEM((TM, D_MODEL), jnp.bfloat16),
               pltpu.VMEM((TM, D_FF), jnp.bfloat16)]
    return pl.pallas_call(
        functools.partial(_sample_body, layer, T, seq_major_in, seq_major_out, is_last),
        grid=(n_seq // S,),
        in_specs=in_specs,
        out_specs=out_specs,
        out_shape=out_shape,
        scratch_shapes=scratch,
        input_output_aliases=aliases,
        compiler_params=pltpu.CompilerParams(dimension_semantics=("arbitrary",),
                                             vmem_limit_bytes=VMEM_LIMIT_BYTES),
        name=f"sample_layer{layer}",
    )(*operands)


def _inv_counts(n_rows):
    win = np.repeat(np.asarray(POOL_WINDOWS, np.float32), POOL_GROUP_DIM)[None, :]
    pos = np.arange(n_rows, dtype=np.float32)[:, None]
    first = 1.0 / np.minimum(pos + 1.0, win)
    later = np.broadcast_to(1.0 / win, (n_rows, POOL_WIDTH))
    return jnp.asarray(np.stack([first, later]).astype(np.float32))


def _prep_body(pool_w_ref, wsp_ref, bt_ref, pw_ref, spp_ref, bfp_ref):
    bf = jnp.bfloat16
    G = POOL_GROUP_DIM
    for k in range(2):
        for a in range(2):
            for b in range(2):
                blk = pool_w_ref[2 * k + a].astype(bf) if a == b else jnp.zeros((G, G), bf)
                pw_ref[k, a * G:(a + 1) * G, b * G:(b + 1) * G] = blk
    row = lax.broadcasted_iota(jnp.int32, (CHUNK, CHUNK), 0)
    col = lax.broadcasted_iota(jnp.int32, (CHUNK, CHUNK), 1)
    causal = row >= col
    for h in range(SG_HEADS):
        spp_ref[h] = jnp.where(causal, wsp_ref[h], 0.0).astype(bf)
        bfp_ref[:, h * SG_HEAD_DIM:(h + 1) * SG_HEAD_DIM] = jnp.broadcast_to(
            bt_ref[:, h:h + 1], (CHUNK, SG_HEAD_DIM))


def _prep_operands(pool_w, w_spatial, b_spatial):
    depth = pool_w.shape[0]
    bf, f32 = jnp.bfloat16, jnp.float32
    per_layer = lambda *shape: pl.BlockSpec((None,) + shape, lambda l: (l,) + (0,) * len(shape))
    return pl.pallas_call(
        _prep_body,
        grid=(depth,),
        in_specs=[per_layer(len(POOL_WINDOWS), POOL_GROUP_DIM, POOL_GROUP_DIM),
                  per_layer(SG_HEADS, CHUNK, CHUNK),
                  per_layer(CHUNK, SG_HEADS)],
        out_specs=[per_layer(2, POOL_WIDTH // 2, POOL_WIDTH // 2), per_layer(SG_HEADS, CHUNK, CHUNK),
                   per_layer(CHUNK, SG_WIDTH)],
        out_shape=[jax.ShapeDtypeStruct((depth, 2, POOL_WIDTH // 2, POOL_WIDTH // 2), bf),
                   jax.ShapeDtypeStruct((depth, SG_HEADS, CHUNK, CHUNK), bf),
                   jax.ShapeDtypeStruct((depth, CHUNK, SG_WIDTH), f32)],
        compiler_params=pltpu.CompilerParams(dimension_semantics=("arbitrary",)),
        name="prep_operands",
    )(pool_w, w_spatial, jnp.swapaxes(b_spatial, 1, 2))


FIRST_CAST_STEPS = 8


def _round_body(*refs):
    n = len(refs) // 2
    for src, dst in zip(refs[:n], refs[n:]):
        dst[...] = src[...].astype(jnp.bfloat16)


def _round_first_layer(weights):
    in_specs, out_specs, out_shape = [], [], []
    for w in weights:
        rows, cols = w.shape[1] // FIRST_CAST_STEPS, w.shape[2]
        assert rows * FIRST_CAST_STEPS == w.shape[1]
        in_specs.append(pl.BlockSpec((None, rows, cols), lambda i: (0, i, 0)))
        out_specs.append(pl.BlockSpec((rows, cols), lambda i: (i, 0)))
        out_shape.append(jax.ShapeDtypeStruct(w.shape[1:], jnp.bfloat16))
    return pl.pallas_call(
        _round_body,
        grid=(FIRST_CAST_STEPS,),
        in_specs=in_specs,
        out_specs=out_specs,
        out_shape=out_shape,
        compiler_params=pltpu.CompilerParams(dimension_semantics=("arbitrary",),
                                             vmem_limit_bytes=VMEM_LIMIT_BYTES),
        name="round_first_layer",
    )(*weights)


def kernel(x_prompt, x_sample, state_pool, state_conv, norm1_g, w_in, pool_w, pool_scale, v_norm_g, w_spatial, b_spatial, w_out, norm2_g, w_gate, w_up, conv_w, conv_b, w_down, final_norm_g):
    depth = w_in.shape[0]
    pw, sp_p, bias_p = _prep_operands(pool_w, w_spatial, b_spatial)
    small = (norm1_g, pool_scale, v_norm_g, norm2_g, conv_w, conv_b, pw, final_norm_g[None, :])
    big_f32 = (w_in, w_out, w_gate, w_up, w_down)
    invcnt = _inv_counts(TM_PROMPT)

    n_dec, t_dec, _ = x_sample.shape
    mats = tuple(_round_first_layer(big_f32))
    state_pool_t = jnp.transpose(state_pool, (0, 2, 1, 3))
    wsp = w_spatial[:, :, :t_dec, :t_dec].reshape(depth, SG_HEADS * t_dec * t_dec)
    bsp = b_spatial[:, :, :t_dec].reshape(depth, SG_HEADS * t_dec)
    n_prompt = x_prompt.shape[0]
    f32 = jnp.float32
    stacked_p = (jnp.zeros((depth, n_prompt, POOL_BUF, POOL_WIDTH), f32),
                 jnp.zeros((depth, n_prompt, CONV_BUF, D_FF), f32))
    sample_stacked_shapes = ((depth, POOL_BUF, n_dec, POOL_WIDTH), (depth, CONV_BUF, n_dec, D_FF),
                             (depth, t_dec, n_dec, SG_WIDTH))
    stacked_s = None
    xp, xs = x_prompt, x_sample
    for i in range(depth):
        last = i == depth - 1
        res = _run_prompt_layer(i, depth, last, xp, invcnt, small, mats, (sp_p, bias_p),
                                None if last else big_f32, stacked_p,
                                zero_fill=sample_stacked_shapes if i == 0 else (),
                                relayout=(state_conv,) if i == 0 else ())
        xp, stacked_p = res[0], res[1:3]
        next_mats = tuple(res[3:3 + len(big_f32)])
        if i == 0:
            extra = res[3 + len(next_mats):]
            stacked_s, state_conv_t = tuple(extra[:len(sample_stacked_shapes)]), extra[len(sample_stacked_shapes)]
        res_s = _run_sample_layer(i, depth, last, xs, i == 0, last, state_pool_t, state_conv_t, wsp, bsp,
                                  small, mats, stacked_s)
        xs, stacked_s = res_s[0], res_s[1:4]
        mats = next_mats
    pool_p, conv_p = stacked_p
    pool_s, conv_s, v_s = (jnp.transpose(a, (0, 2, 1, 3)) for a in stacked_s)
    return (xp, xs, pool_p, pool_s, conv_p, conv_s, v_s)
```

```python
import functools

import numpy as np
import jax
import jax.numpy as jnp
from jax import lax
from jax.experimental import pallas as pl
from jax.experimental.pallas import tpu as pltpu

D_MODEL = 1024
POOL_WIDTH = 512
POOL_WINDOWS = (2, 4, 8, 16)
POOL_GROUP_DIM = 128
POOL_BUF = 15
POOL_PAD = 16
SG_WIDTH = 512
SG_HEADS = 4
SG_HEAD_DIM = 128
CHUNK = 128
IN_WIDTH = POOL_WIDTH + 2 * SG_WIDTH
D_FF = 2816
CONV_BUF = 2
CONV_PAD = 8
EPS = 1e-6

TM_PROMPT = 512
SAMPLE_SEQS = 64
FF_CHUNK = 256
TAIL_PARTS = 4
TAIL_COLS = D_MODEL // TAIL_PARTS
DOWN_CAST_ROWS = 176
VMEM_LIMIT_BYTES = 58 * 1024 * 1024

_INV_SQRT2 = 0.7071067811865476


def _gelu(x):
    return 0.5 * x * (1.0 + lax.erf(x * _INV_SQRT2))


def _rms(x, g):
    ms = jnp.mean(x * x, axis=-1, keepdims=True)
    return x * lax.rsqrt(ms + EPS) * g


def _bdot(a, b):
    return jnp.dot(a, b, preferred_element_type=jnp.float32)


def _const_spec(shape):
    nd = len(shape)
    return pl.BlockSpec(shape, lambda *_: (0,) * nd)


def _layer_spec(shape, layer):
    nd = len(shape)
    return pl.BlockSpec((None,) + tuple(shape), lambda *_: (layer,) + (0,) * nd)


def _prompt_body(layer, T, is_last, n_cast, n_alias, n_zero, n_relay, seq_steps, n_tiles, *refs):
    S = 1
    refs = list(refs)
    x_ref = refs.pop(0)
    (invcnt_ref, n1_ref, win_ref, pw_ref, pscale_ref, vng_ref, sp_ref, bfull_ref, wout_ref, n2_ref,
     wg_ref, wu_ref, cw_ref, cb_ref, wd_ref, fn_ref) = refs[:16]
    refs = refs[16:]
    n1_ref, pscale_ref, vng_ref, n2_ref, cb_ref = (
        r.at[pl.ds(layer, 1)] for r in (n1_ref, pscale_ref, vng_ref, n2_ref, cb_ref))
    cast_in, refs = refs[:n_cast], refs[n_cast:]
    relay_in, refs = refs[:n_relay], refs[n_relay:]
    refs = refs[n_alias:]
    y_ref, pool_out_ref, conv_out_ref = refs[:3]
    refs = refs[3:]
    cast_out, refs = refs[:n_cast], refs[n_cast:]
    zero_out, refs = refs[:n_zero], refs[n_zero:]
    relay_out, refs = refs[:n_relay], refs[n_relay:]

    for dst in zero_out:
        dst[...] = jnp.zeros(dst.shape, dst.dtype)
    for src, dst in zip(relay_in, relay_out):
        for r in range(dst.shape[1]):
            dst[0, r] = src[0, :, r, :]
    pbuf, st8, st4, st2, gbuf, mixbuf, h2buf, actbuf, x1buf = refs[:9]

    for src, dst in zip(cast_in, cast_out):
        dst[...] = src[...].astype(jnp.bfloat16)

    def tail_begin():
        y_ref[...] = x1buf[...]

    def tail_part(k):
        cols = slice(k * TAIL_COLS, (k + 1) * TAIL_COLS)
        y_ref[:, cols] += _bdot(actbuf[...], wd_ref[:, cols])

    def tail_end():
        if is_last:
            y_ref[...] = _rms(y_ref[...], fn_ref[...])

    def tail_all():
        tail_begin()
        for k in range(TAIL_PARTS):
            tail_part(k)
        tail_end()

    front = functools.partial(
        _prompt_front, S, T, x_ref, invcnt_ref, n1_ref, win_ref, pw_ref, pscale_ref, vng_ref, sp_ref,
        bfull_ref, wout_ref, n2_ref, wg_ref, wu_ref, cw_ref, cb_ref, pool_out_ref, conv_out_ref,
        pbuf, st8, st4, st2, gbuf, mixbuf, h2buf, actbuf, x1buf)

    s = pl.program_id(0)
    gcarry = refs[9]
    j = lax.rem(s, seq_steps)

    @pl.when(s == 0)
    def _():
        x1buf[...] = jnp.zeros(x1buf.shape, x1buf.dtype)
        actbuf[...] = jnp.zeros(actbuf.shape, actbuf.dtype)

    @pl.when(j == 0)
    def _():
        pbuf[:, 0:POOL_PAD, :] = jnp.zeros((S, POOL_PAD, POOL_WIDTH), jnp.float32)
        gcarry[...] = jnp.zeros((CONV_PAD, D_FF), jnp.float32)

    @pl.when(j > 0)
    def _():
        pbuf[:, 0:POOL_PAD, :] = pbuf[:, T:T + POOL_PAD, :]

    @pl.when(s < n_tiles)
    def _():
        tail_begin()
        front(gcarry, tail_part)
        tail_end()

    @pl.when(s == n_tiles)
    def _():
        tail_all()


def _prompt_front(S, T, x_ref, invcnt_ref, n1_ref, win_ref, pw_ref, pscale_ref, vng_ref, sp_ref,
                  bfull_ref, wout_ref, n2_ref, wg_ref, wu_ref, cw_ref, cb_ref, pool_out_ref, conv_out_ref,
                  pbuf, st8, st4, st2, gbuf, mixbuf, h2buf, actbuf, x1buf, gcarry, fill):
    TM = S * T
    R = POOL_PAD + T
    x = x_ref[...]
    fill(0)
    h = _rms(x, n1_ref[...]).astype(jnp.bfloat16)

    p = _bdot(h, win_ref[:, 0:POOL_WIDTH])
    u = _gelu(_bdot(h, win_ref[:, POOL_WIDTH:POOL_WIDTH + SG_WIDTH]))
    v = _rms(_gelu(_bdot(h, win_ref[:, POOL_WIDTH + SG_WIDTH:IN_WIDTH])), vng_ref[...])
    fill(1)
    pbuf[:, POOL_PAD:R, :] = p.reshape(S, T, POOL_WIDTH)
    pool_out_ref[...] = pbuf[:, R - POOL_BUF:R, :]

    zeros8 = jnp.zeros((S, 8, POOL_WIDTH), jnp.float32)
    st8[:, 0:8, :] = zeros8
    st4[:, 0:8, :] = zeros8
    st2[:, 0:8, :] = zeros8
    G = POOL_GROUP_DIM
    st8[:, 8:R, 3 * G:4 * G] = pbuf[:, 8:R, 3 * G:4 * G] + pbuf[:, 0:R - 8, 3 * G:4 * G]
    st4[:, 8:R, 2 * G:3 * G] = pbuf[:, 8:R, 2 * G:3 * G] + pbuf[:, 4:R - 4, 2 * G:3 * G]
    st4[:, 8:R, 3 * G:4 * G] = st8[:, 8:R, 3 * G:4 * G] + st8[:, 4:R - 4, 3 * G:4 * G]
    st2[:, 8:R, 1 * G:2 * G] = pbuf[:, 8:R, 1 * G:2 * G] + pbuf[:, 6:R - 2, 1 * G:2 * G]
    st2[:, 8:R, 2 * G:4 * G] = st4[:, 8:R, 2 * G:4 * G] + st4[:, 6:R - 2, 2 * G:4 * G]
    sum0 = pbuf[:, POOL_PAD:R, 0:G] + pbuf[:, POOL_PAD - 1:R - 1, 0:G]
    sum123 = st2[:, POOL_PAD:R, G:4 * G] + st2[:, POOL_PAD - 1:R - 1, G:4 * G]
    wsum = jnp.concatenate([sum0, sum123], axis=-1).reshape(TM, POOL_WIDTH)
    dpool = wsum * invcnt_ref[...] - p
    d16 = dpool.astype(jnp.bfloat16)
    half = POOL_WIDTH // 2
    for k in range(2):
        ks = slice(k * half, (k + 1) * half)
        a = _bdot(d16[:, ks], pw_ref[k]) * pscale_ref[:, ks]
        mixbuf[:, ks] = a.astype(jnp.bfloat16)

    fill(2)

    v16 = v.astype(jnp.bfloat16)
    for hd in range(SG_HEADS):
        sl = slice(hd * SG_HEAD_DIM, (hd + 1) * SG_HEAD_DIM)
        osl = slice(POOL_WIDTH + hd * SG_HEAD_DIM, POOL_WIDTH + (hd + 1) * SG_HEAD_DIM)
        chunks = [slice(c * CHUNK, (c + 1) * CHUNK) for c in range(TM // CHUNK)]
        mixed_all = _bdot(sp_ref[hd], jnp.concatenate([v16[rs, sl] for rs in chunks], axis=1))
        for c, rs in enumerate(chunks):
            mixed = mixed_all[:, c * SG_HEAD_DIM:(c + 1) * SG_HEAD_DIM] + bfull_ref[:, sl]
            mixbuf[rs, osl] = (u[rs, sl] * mixed).astype(jnp.bfloat16)

    x1 = x + _bdot(mixbuf[...], wout_ref[...])
    x1buf[...] = x1
    fill(3)
    h2buf[...] = _rms(x1, n2_ref[...]).astype(jnp.bfloat16)

    for c in range(D_FF // FF_CHUNK):
        cs = slice(c * FF_CHUNK, (c + 1) * FF_CHUNK)
        g = _bdot(h2buf[...], wg_ref[:, cs])
        up = _bdot(h2buf[...], wu_ref[:, cs])
        g3 = g.reshape(S, T, FF_CHUNK)
        gbuf[0, 0:CONV_PAD, :] = gcarry[:, cs]
        gcarry[:, cs] = g3[0, T - CONV_PAD:T, :]
        gbuf[:, CONV_PAD:CONV_PAD + T, :] = g3
        conv_out_ref[:, :, cs] = g3[:, T - CONV_BUF:T, :]
        conv = (cb_ref[:, cs]
                + cw_ref[0:1, cs] * gbuf[:, CONV_PAD - 2:CONV_PAD - 2 + T, :]
                + cw_ref[1:2, cs] * gbuf[:, CONV_PAD - 1:CONV_PAD - 1 + T, :]
                + cw_ref[2:3, cs] * g3)
        act = _gelu(conv).reshape(TM, FF_CHUNK) * up
        actbuf[:, cs] = act.astype(jnp.bfloat16)


def _run_prompt_layer(layer, depth, is_last, x, invcnt, small, mats, spatial, next_f32, stacked, zero_fill=(),
                      relayout=()):
    (n1, pscale, vng, n2, cw, cb, pw, fn) = small
    (win, wout, wg, wu, wd) = mats
    sp, bfull = spatial
    TM = TM_PROMPT
    f32 = jnp.float32
    B, L, _ = x.shape
    S, T = 1, TM
    seq_steps = steps = L // TM
    n_tiles = B * steps
    grid = (n_tiles + 1,)
    cur = lambda s: jnp.minimum(s, n_tiles - 1)
    prev = lambda s: jnp.maximum(s - 1, 0)
    in_specs = [pl.BlockSpec((None, TM, D_MODEL), lambda s: (cur(s) // steps, cur(s) % steps, 0)),
                pl.BlockSpec((None, TM, POOL_WIDTH), lambda s: (jnp.minimum(cur(s) % steps, 1), 0, 0))]
    operands = [x, invcnt]
    out_shape = [jax.ShapeDtypeStruct((B, L, D_MODEL), f32),
                 jax.ShapeDtypeStruct((depth, B, POOL_BUF, POOL_WIDTH), f32),
                 jax.ShapeDtypeStruct((depth, B, CONV_BUF, D_FF), f32)]
    out_specs = [pl.BlockSpec((None, TM, D_MODEL), lambda s: (prev(s) // steps, prev(s) % steps, 0)),
                 pl.BlockSpec((None, 1, POOL_BUF, POOL_WIDTH), lambda s: (layer, cur(s) // steps, 0, 0)),
                 pl.BlockSpec((None, 1, CONV_BUF, D_FF), lambda s: (layer, cur(s) // steps, 0, 0))]

    in_specs += [
        _const_spec((depth, D_MODEL)),
        _const_spec((D_MODEL, IN_WIDTH)),
        _layer_spec((2, POOL_WIDTH // 2, POOL_WIDTH // 2), layer),
        _const_spec((depth, POOL_WIDTH)),
        _const_spec((depth, SG_WIDTH)),
        _layer_spec(sp.shape[1:], layer),
        _layer_spec(bfull.shape[1:], layer),
        _const_spec((D_MODEL, D_MODEL)),
        _const_spec((depth, D_MODEL)),
        _const_spec((D_MODEL, D_FF)),
        _const_spec((D_MODEL, D_FF)),
        _layer_spec((3, D_FF), layer),
        _const_spec((depth, D_FF)),
        _const_spec((D_FF, D_MODEL)),
        _const_spec((1, D_MODEL)),
    ]
    operands += [n1, win, pw, pscale, vng, sp, bfull, wout, n2, wg, wu, cw, cb, wd, fn]

    n_cast = 0
    if next_f32 is not None:
        for w in next_f32:
            rows, share = ((D_MODEL // n_tiles, 1) if w.shape[1] == D_MODEL
                           else (DOWN_CAST_ROWS, n_tiles * DOWN_CAST_ROWS // D_FF))
            cols = w.shape[2]
            in_specs.append(pl.BlockSpec((None, rows, cols),
                                         lambda s, share=share: (layer + 1, cur(s) // share, 0)))
            out_specs.append(pl.BlockSpec((rows, cols), lambda s, share=share: (cur(s) // share, 0)))
            out_shape.append(jax.ShapeDtypeStruct(w.shape[1:], jnp.bfloat16))
            operands.append(w)
            n_cast += 1

    for arr in relayout:
        n_lay, n_sq, n_rows, n_ch = arr.shape
        assert n_tiles % n_lay == 0
        per_layer = n_tiles // n_lay
        sq = n_sq // per_layer
        assert sq * per_layer == n_sq
        in_specs.append(pl.BlockSpec(
            (1, sq, n_rows, n_ch), lambda s, per_layer=per_layer: (cur(s) // per_layer, cur(s) % per_layer, 0, 0)))
        operands.append(arr)

    for shape in zero_fill:
        assert n_tiles % shape[0] == 0 and len(shape) == 4
        per_layer = n_tiles // shape[0]
        blk = (1, shape[1], shape[2] // per_layer, shape[3])
        assert blk[2] * per_layer == shape[2]
        out_specs.append(pl.BlockSpec(
            blk, lambda s, per_layer=per_layer: (cur(s) // per_layer, 0, cur(s) % per_layer, 0)))
        out_shape.append(jax.ShapeDtypeStruct(shape, f32))

    for arr in relayout:
        n_lay, n_sq, n_rows, n_ch = arr.shape
        per_layer = n_tiles // n_lay
        out_specs.append(pl.BlockSpec(
            (1, n_rows, n_sq // per_layer, n_ch),
            lambda s, per_layer=per_layer: (cur(s) // per_layer, 0, cur(s) % per_layer, 0)))
        out_shape.append(jax.ShapeDtypeStruct((n_lay, n_rows, n_sq, n_ch), arr.dtype))

    aliases = {}
    for k, arr in enumerate(stacked):
        aliases[len(operands)] = 1 + k
        in_specs.append(pl.BlockSpec(memory_space=pl.ANY))
        operands.append(arr)
    n_alias = len(stacked)

    R = POOL_PAD + T
    scratch = [pltpu.VMEM((S, R, POOL_WIDTH), jnp.float32)] * 4
    scratch += [pltpu.VMEM((S, CONV_PAD + T, FF_CHUNK), jnp.float32),
                pltpu.VMEM((TM, D_MODEL), jnp.bfloat16),
                pltpu.VMEM((TM, D_MODEL), jnp.bfloat16),
                pltpu.VMEM((TM, D_FF), jnp.bfloat16),
                pltpu.VMEM((TM, D_MODEL), jnp.float32),
                pltpu.VMEM((CONV_PAD, D_FF), jnp.float32)]

    return pl.pallas_call(
        functools.partial(_prompt_body, layer, T, is_last, n_cast, n_alias, len(zero_fill), len(relayout),
                          seq_steps, n_tiles),
        grid=grid,
        in_specs=in_specs,
        out_specs=out_specs,
        out_shape=out_shape,
        scratch_shapes=scratch,
        input_output_aliases=aliases,
        compiler_params=pltpu.CompilerParams(dimension_semantics=("arbitrary",),
                                             vmem_limit_bytes=VMEM_LIMIT_BYTES),
        name=f"prompt_layer{layer}",
    )(*operands)


def _sample_body(layer, T, seq_major_in, seq_major_out, is_last, x_ref, spool_ref, sconv_ref, wsp_ref, bsp_ref,
                 n1_ref, win_ref, pw_ref, pscale_ref, vng_ref, wout_ref, n2_ref, wg_ref, wu_ref, cw_ref, cb_ref,
                 wd_ref, fn_ref, pool_in_all, conv_in_all, v_in_all, y_ref, pool_out_ref, conv_out_ref, v_out_ref,
                 mixbuf, h2buf, actbuf):
    del pool_in_all, conv_in_all, v_in_all
    S = SAMPLE_SEQS
    TM = T * S
    n1_ref, pscale_ref, vng_ref, n2_ref, cb_ref = (
        r.at[pl.ds(layer, 1)] for r in (n1_ref, pscale_ref, vng_ref, n2_ref, cb_ref))
    slab = lambda t: slice(t * S, (t + 1) * S)

    if seq_major_in:
        x = jnp.concatenate([x_ref[:, t, :] for t in range(T)], axis=0)
    else:
        x = x_ref[...].reshape(TM, D_MODEL)
    h = _rms(x, n1_ref[...]).astype(jnp.bfloat16)
    p = _bdot(h, win_ref[:, 0:POOL_WIDTH])
    v = _rms(_gelu(_bdot(h, win_ref[:, POOL_WIDTH + SG_WIDTH:IN_WIDTH])), vng_ref[...])
    u = _gelu(_bdot(h, win_ref[:, POOL_WIDTH:POOL_WIDTH + SG_WIDTH]))
    v_out_ref[...] = v.reshape(T, S, SG_WIDTH)

    pool_out_ref[0:POOL_BUF - T] = spool_ref[T:POOL_BUF]
    pool_out_ref[POOL_BUF - T:POOL_BUF] = p.reshape(T, S, POOL_WIDTH)
    G = POOL_GROUP_DIM
    for gi, w in enumerate(POOL_WINDOWS):
        gs = slice(gi * G, (gi + 1) * G)

        def row(j):
            return spool_ref[j, :, gs] if j < POOL_BUF else p[slab(j - POOL_BUF), gs]

        for t in range(T):
            acc = row(POOL_BUF + t)
            for k in range(1, w):
                acc = acc + row(POOL_BUF + t - k)
            d = acc * (1.0 / w) - p[slab(t), gs]
            h2buf[slab(t), gs] = d.astype(jnp.bfloat16)
    half = POOL_WIDTH // 2
    for k in range(2):
        ks = slice(k * half, (k + 1) * half)
        a = _bdot(h2buf[:, ks], pw_ref[k]) * pscale_ref[:, ks]
        mixbuf[:, ks] = a.astype(jnp.bfloat16)

    for hd in range(SG_HEADS):
        hs = slice(hd * SG_HEAD_DIM, (hd + 1) * SG_HEAD_DIM)
        for t in range(T):
            mixed = jnp.full((S, SG_HEAD_DIM), bsp_ref[layer, hd * T + t], jnp.float32)
            for t2 in range(t + 1):
                mixed = mixed + wsp_ref[layer, (hd * T + t) * T + t2] * v[slab(t2), hs]
            mixbuf[slab(t), POOL_WIDTH + hd * SG_HEAD_DIM:POOL_WIDTH + (hd + 1) * SG_HEAD_DIM] = (
                (u[slab(t), hs] * mixed).astype(jnp.bfloat16))

    x1 = x + _bdot(mixbuf[...], wout_ref[...])
    h2buf[...] = _rms(x1, n2_ref[...]).astype(jnp.bfloat16)

    hist = CONV_BUF * S
    for c in range(D_FF // FF_CHUNK):
        cs = slice(c * FF_CHUNK, (c + 1) * FF_CHUNK)
        g = _bdot(h2buf[...], wg_ref[:, cs])
        up = _bdot(h2buf[...], wu_ref[:, cs])
        past = sconv_ref[:, :, cs].reshape(hist, FF_CHUNK)
        conv_out_ref[:, :, cs] = g[TM - hist:TM, :].reshape(CONV_BUF, S, FF_CHUNK)
        back2 = jnp.concatenate([past, g[0:TM - 2 * S, :]], axis=0)
        back1 = jnp.concatenate([past[S:hist, :], g[0:TM - S, :]], axis=0)
        conv = cb_ref[:, cs] + cw_ref[0:1, cs] * back2 + cw_ref[1:2, cs] * back1 + cw_ref[2:3, cs] * g
        actbuf[:, cs] = (_gelu(conv) * up).astype(jnp.bfloat16)

    out = x1 + _bdot(actbuf[...], wd_ref[...])
    if is_last:
        out = _rms(out, fn_ref[...])
    if seq_major_out:
        for t in range(T):
            y_ref[:, t, :] = out[slab(t)]
    else:
        y_ref[...] = out.reshape(T, S, D_MODEL)


def _run_sample_layer(layer, depth, is_last, x, seq_major_in, seq_major_out, state_pool_t, state_conv_t, wsp, bsp,
                      small, mats, stacked):
    (n1, pscale, vng, n2, cw, cb, pw, fn) = small
    (win, wout, wg, wu, wd) = mats
    n_seq, T = (x.shape[0], x.shape[1]) if seq_major_in else (x.shape[1], x.shape[0])
    S = SAMPLE_SEQS
    TM = T * S
    f32 = jnp.float32
    smem = pl.BlockSpec(memory_space=pltpu.SMEM)
    seq_major_spec = pl.BlockSpec((S, T, D_MODEL), lambda i: (i, 0, 0))
    time_major_spec = pl.BlockSpec((T, S, D_MODEL), lambda i: (0, i, 0))
    in_specs = [
        seq_major_spec if seq_major_in else time_major_spec,
        pl.BlockSpec((None, POOL_BUF, S, POOL_WIDTH), lambda i: (layer, 0, i, 0)),
        pl.BlockSpec((None, CONV_BUF, S, D_FF), lambda i: (layer, 0, i, 0)),
        smem, smem,
        _const_spec((depth, D_MODEL)),
        _const_spec((D_MODEL, IN_WIDTH)),
        _layer_spec((2, POOL_WIDTH // 2, POOL_WIDTH // 2), layer),
        _const_spec((depth, POOL_WIDTH)),
        _const_spec((depth, SG_WIDTH)),
        _const_spec((D_MODEL, D_MODEL)),
        _const_spec((depth, D_MODEL)),
        _const_spec((D_MODEL, D_FF)),
        _const_spec((D_MODEL, D_FF)),
        _layer_spec((3, D_FF), layer),
        _const_spec((depth, D_FF)),
        _const_spec((D_FF, D_MODEL)),
        _const_spec((1, D_MODEL)),
    ]
    operands = [x, state_pool_t, state_conv_t, wsp, bsp, n1, win, pw, pscale, vng, wout, n2, wg, wu, cw, cb, wd, fn]
    aliases = {}
    for k, arr in enumerate(stacked):
        aliases[len(operands)] = 1 + k
        in_specs.append(pl.BlockSpec(memory_space=pl.ANY))
        operands.append(arr)
    out_shape = [jax.ShapeDtypeStruct((n_seq, T, D_MODEL) if seq_major_out else (T, n_seq, D_MODEL), f32),
                 jax.ShapeDtypeStruct((depth, POOL_BUF, n_seq, POOL_WIDTH), f32),
                 jax.ShapeDtypeStruct((depth, CONV_BUF, n_seq, D_FF), f32),
                 jax.ShapeDtypeStruct((depth, T, n_seq, SG_WIDTH), f32)]
    out_specs = [seq_major_spec if seq_major_out else time_major_spec,
                 pl.BlockSpec((None, POOL_BUF, S, POOL_WIDTH), lambda i: (layer, 0, i, 0)),
                 pl.BlockSpec((None, CONV_BUF, S, D_FF), lambda i: (layer, 0, i, 0)),
                 pl.BlockSpec((None, T, S, SG_WIDTH), lambda i: (layer, 0, i, 0))]
    scratch = [pltpu.VMEM((TM, D_MODEL), jnp.bfloat16),
               pltpu.VMEM((TM, D_MODEL), jnp.bfloat16),
               pltpu.VMEM((TM, D_FF), jnp.bfloat16)]
    return pl.pallas_call(
        functools.partial(_sample_body, layer, T, seq_major_in, seq_major_out, is_last),
        grid=(n_seq // S,),
        in_specs=in_specs,
        out_specs=out_specs,
        out_shape=out_shape,
        scratch_shapes=scratch,
        input_output_aliases=aliases,
        compiler_params=pltpu.CompilerParams(dimension_semantics=("arbitrary",),
                                             vmem_limit_bytes=VMEM_LIMIT_BYTES),
        name=f"sample_layer{layer}",
    )(*operands)


def _inv_counts(n_rows):
    win = np.repeat(np.asarray(POOL_WINDOWS, np.float32), POOL_GROUP_DIM)[None, :]
    pos = np.arange(n_rows, dtype=np.float32)[:, None]
    first = 1.0 / np.minimum(pos + 1.0, win)
    later = np.broadcast_to(1.0 / win, (n_rows, POOL_WIDTH))
    return jnp.asarray(np.stack([first, later]).astype(np.float32))


def _prep_body(pool_w_ref, wsp_ref, bt_ref, pw_ref, spp_ref, bfp_ref):
    bf = jnp.bfloat16
    G = POOL_GROUP_DIM
    for k in range(2):
        for a in range(2):
            for b in range(2):
                blk = pool_w_ref[2 * k + a].astype(bf) if a == b else jnp.zeros((G, G), bf)
                pw_ref[k, a * G:(a + 1) * G, b * G:(b + 1) * G] = blk
    row = lax.broadcasted_iota(jnp.int32, (CHUNK, CHUNK), 0)
    col = lax.broadcasted_iota(jnp.int32, (CHUNK, CHUNK), 1)
    causal = row >= col
    for h in range(SG_HEADS):
        spp_ref[h] = jnp.where(causal, wsp_ref[h], 0.0).astype(bf)
        bfp_ref[:, h * SG_HEAD_DIM:(h + 1) * SG_HEAD_DIM] = jnp.broadcast_to(
            bt_ref[:, h:h + 1], (CHUNK, SG_HEAD_DIM))


SETUP_STEPS = 8


def _setup_body(depth, n_w, *refs):
    pool_w_ref, wsp_ref, bt_ref = refs[:3]
    w_in = refs[3:3 + n_w]
    pw_ref, spp_ref, bfp_ref, zpool_ref, zconv_ref = refs[3 + n_w:8 + n_w]
    w_out = refs[8 + n_w:]
    for src, dst in zip(w_in, w_out):
        dst[...] = src[...].astype(jnp.bfloat16)

    @pl.when(pl.program_id(0) < depth)
    def _():
        _prep_body(pool_w_ref, wsp_ref, bt_ref, pw_ref, spp_ref, bfp_ref)
        zpool_ref[...] = jnp.zeros(zpool_ref.shape, zpool_ref.dtype)
        zconv_ref[...] = jnp.zeros(zconv_ref.shape, zconv_ref.dtype)


def _setup(pool_w, w_spatial, b_spatial, weights, n_prompt):
    depth = pool_w.shape[0]
    assert depth <= SETUP_STEPS
    bf, f32 = jnp.bfloat16, jnp.float32
    per_layer = lambda *shape: pl.BlockSpec(
        (None,) + shape, lambda i: (jnp.minimum(i, depth - 1),) + (0,) * len(shape))
    in_specs = [per_layer(len(POOL_WINDOWS), POOL_GROUP_DIM, POOL_GROUP_DIM),
                per_layer(SG_HEADS, CHUNK, CHUNK),
                per_layer(CHUNK, SG_HEADS)]
    out_specs = [per_layer(2, POOL_WIDTH // 2, POOL_WIDTH // 2), per_layer(SG_HEADS, CHUNK, CHUNK),
                 per_layer(CHUNK, SG_WIDTH), per_layer(n_prompt, POOL_BUF, POOL_WIDTH),
                 per_layer(n_prompt, CONV_BUF, D_FF)]
    out_shape = [jax.ShapeDtypeStruct((depth, 2, POOL_WIDTH // 2, POOL_WIDTH // 2), bf),
                 jax.ShapeDtypeStruct((depth, SG_HEADS, CHUNK, CHUNK), bf),
                 jax.ShapeDtypeStruct((depth, CHUNK, SG_WIDTH), f32),
                 jax.ShapeDtypeStruct((depth, n_prompt, POOL_BUF, POOL_WIDTH), f32),
                 jax.ShapeDtypeStruct((depth, n_prompt, CONV_BUF, D_FF), f32)]
    for w in weights:
        rows, cols = w.shape[1] // SETUP_STEPS, w.shape[2]
        assert rows * SETUP_STEPS == w.shape[1]
        in_specs.append(pl.BlockSpec((None, rows, cols), lambda i: (0, i, 0)))
        out_specs.append(pl.BlockSpec((rows, cols), lambda i: (i, 0)))
        out_shape.append(jax.ShapeDtypeStruct(w.shape[1:], bf))
    res = pl.pallas_call(
        functools.partial(_setup_body, depth, len(weights)),
        grid=(SETUP_STEPS,),
        in_specs=in_specs,
        out_specs=out_specs,
        out_shape=out_shape,
        compiler_params=pltpu.CompilerParams(dimension_semantics=("arbitrary",),
                                             vmem_limit_bytes=VMEM_LIMIT_BYTES),
        name="setup",
    )(pool_w, w_spatial, jnp.swapaxes(b_spatial, 1, 2), *weights)
    return res[:3], tuple(res[3:5]), tuple(res[5:])


def kernel(x_prompt, x_sample, state_pool, state_conv, norm1_g, w_in, pool_w, pool_scale, v_norm_g, w_spatial, b_spatial, w_out, norm2_g, w_gate, w_up, conv_w, conv_b, w_down, final_norm_g):
    depth = w_in.shape[0]
    big_f32 = (w_in, w_out, w_gate, w_up, w_down)
    (pw, sp_p, bias_p), stacked_p, mats = _setup(pool_w, w_spatial, b_spatial, big_f32, x_prompt.shape[0])
    small = (norm1_g, pool_scale, v_norm_g, norm2_g, conv_w, conv_b, pw, final_norm_g[None, :])
    invcnt = _inv_counts(TM_PROMPT)

    n_dec, t_dec, _ = x_sample.shape
    state_pool_t = jnp.transpose(state_pool, (0, 2, 1, 3))
    wsp = w_spatial[:, :, :t_dec, :t_dec].reshape(depth, SG_HEADS * t_dec * t_dec)
    bsp = b_spatial[:, :, :t_dec].reshape(depth, SG_HEADS * t_dec)
    sample_stacked_shapes = ((depth, POOL_BUF, n_dec, POOL_WIDTH), (depth, CONV_BUF, n_dec, D_FF),
                             (depth, t_dec, n_dec, SG_WIDTH))
    stacked_s = None
    xp, xs = x_prompt, x_sample
    for i in range(depth):
        last = i == depth - 1
        res = _run_prompt_layer(i, depth, last, xp, invcnt, small, mats, (sp_p, bias_p),
                                None if last else big_f32, stacked_p,
                                zero_fill=sample_stacked_shapes if i == 0 else (),
                                relayout=(state_conv,) if i == 0 else ())
        xp, stacked_p = res[0], res[1:3]
        next_mats = tuple(res[3:3 + len(big_f32)])
        if i == 0:
            extra = res[3 + len(next_mats):]
            stacked_s, state_conv_t = tuple(extra[:len(sample_stacked_shapes)]), extra[len(sample_stacked_shapes)]
        res_s = _run_sample_layer(i, depth, last, xs, i == 0, last, state_pool_t, state_conv_t, wsp, bsp,
                                  small, mats, stacked_s)
        xs, stacked_s = res_s[0], res_s[1:4]
        mats = next_mats
    pool_p, conv_p = stacked_p
    pool_s, conv_s, v_s = (jnp.transpose(a, (0, 2, 1, 3)) for a in stacked_s)
    return (xp, xs, pool_p, pool_s, conv_p, conv_s, v_s)
```
